```python
import jax, jax.numpy as jnp
from jax import lax
import numpy as np

D_MODEL = 1024
BATCH = 1
SEQ = 16384
DEPTH = 4

GRID_W = 64
CTX_LEN = 256
D_MIX = D_MODEL
D_RNN = D_MIX // 2
RNN_BLOCKS = 8
RNN_BLOCK = D_RNN // RNN_BLOCKS
CONV_W = 4
LRU_C = 8.0
N_HEADS = 8
N_KV_HEADS = 2
HEAD_DIM = (D_MIX - D_RNN) // N_HEADS
D_ATTN = N_HEADS * HEAD_DIM
D_KV = N_KV_HEADS * HEAD_DIM
WINDOW = 128
BLOCK_Q = 128
ROPE_BASE = 10000.0
ROPE_AXIS_DIM = HEAD_DIM // 2
D_FF = 3 * D_MODEL
N_EXPERTS = 8
TOP_K = 2
D_FF_EXPERT = D_FF // TOP_K
N_DENSE = (DEPTH + 1) // 2
N_MOE = DEPTH // 2
D_IN = 2 * D_RNN + D_ATTN + 2 * D_KV
EPS = 1e-6
NEG_INF = -1e30

kernel_name = "hybrid_rglru_swa_moe_dit"


def rmsnorm(x, g):
    x32 = x.astype(jnp.float32)
    y = x32 * lax.rsqrt(jnp.mean(x32 * x32, axis=-1, keepdims=True) + EPS)
    return (y * g.astype(jnp.float32)).astype(x.dtype)


def adaln(cond, w_mod, b_mod):
    m = jax.nn.silu(cond) @ w_mod + b_mod
    return jnp.split(m[..., None, :], 6, axis=-1)


def modulate(h, shift, scale):
    return h * (1 + scale) + shift


def centred_dwconv(x, w, b):
    L = x.shape[1]
    left = CONV_W // 2
    right = CONV_W - 1 - left
    xp = jnp.pad(x, ((0, 0), (left, right), (0, 0)))
    return sum((xp[:, k:k + L] * w[k] for k in range(CONV_W)), b)


def block_diag_linear(x, w, b):
    xb = x.reshape(x.shape[:-1] + (RNN_BLOCKS, RNN_BLOCK))
    return jnp.einsum('blgi,gij->blgj', xb, w).reshape(x.shape) + b


def rglru_gates(u, w_r, w_i, b_r, b_i, lam):
    r = jax.nn.sigmoid(block_diag_linear(u, w_r, b_r).astype(jnp.float32))
    i = jax.nn.sigmoid(block_diag_linear(u, w_i, b_i).astype(jnp.float32))
    log_a = -LRU_C * r * jax.nn.softplus(-lam.astype(jnp.float32))
    a = jnp.exp(log_a)
    mult = jnp.sqrt(jnp.maximum(-jnp.expm1(2.0 * log_a), 0.0))
    return a, mult * (i * u.astype(jnp.float32))


def _combine(left, right):
    a_l, b_l = left
    a_r, b_r = right
    return a_l * a_r, a_r * b_l + b_r


def linear_scan(a, b, h0):
    b = b.at[:, 0].add(a[:, 0] * h0)
    _, h = lax.associative_scan(_combine, (a, b), axis=1)
    return h


def rglru_bidir(u_ctx, u_lat, w_rg, b_rg, lam):
    B = u_lat.shape[0]
    h0 = jnp.zeros((B, D_RNN), jnp.float32)
    outs_c, outs_l = [], []
    for d, reverse in enumerate((False, True)):
        uc = jnp.flip(u_ctx, axis=1) if reverse else u_ctx
        ul = jnp.flip(u_lat, axis=1) if reverse else u_lat
        a_c, b_c = rglru_gates(uc, w_rg[d, 0], w_rg[d, 1], b_rg[d, 0], b_rg[d, 1], lam[d])
        h_c = linear_scan(a_c, b_c, h0)
        a_l, b_l = rglru_gates(ul, w_rg[d, 0], w_rg[d, 1], b_rg[d, 0], b_rg[d, 1], lam[d])
        h_l = linear_scan(a_l, b_l, h_c[:, -1])
        if reverse:
            h_c = jnp.flip(h_c, axis=1)
            h_l = jnp.flip(h_l, axis=1)
        outs_c.append(h_c)
        outs_l.append(h_l)
    return (outs_c[0] + outs_c[1]).astype(u_ctx.dtype), (outs_l[0] + outs_l[1]).astype(u_lat.dtype)


def axial_rope_tables(n_rows):
    freqs = ROPE_BASE ** (-jnp.arange(0, ROPE_AXIS_DIM, 2, dtype=jnp.float32) / ROPE_AXIS_DIM)
    rows = jnp.repeat(jnp.arange(n_rows, dtype=jnp.float32), GRID_W)
    cols = jnp.tile(jnp.arange(GRID_W, dtype=jnp.float32), n_rows)
    ang = jnp.stack([rows[:, None] * freqs, cols[:, None] * freqs], axis=1)
    return jnp.cos(ang), jnp.sin(ang)


def apply_axial_rope(x, cos, sin):
    B, L, H, _ = x.shape
    xr = x.astype(jnp.float32).reshape(B, L, H, 2, 2, ROPE_AXIS_DIM // 2)
    x1, x2 = xr[..., 0, :], xr[..., 1, :]
    c = cos[None, :, None]
    s = sin[None, :, None]
    out = jnp.stack([x1 * c - x2 * s, x1 * s + x2 * c], axis=-2)
    return out.reshape(B, L, H, HEAD_DIM).astype(x.dtype)


def latent_window_attention(q, k, v, k_ctx, v_ctx, sink):
    B, L = q.shape[:2]
    C = k_ctx.shape[1]
    nb = L // BLOCK_Q
    G = N_HEADS // N_KV_HEADS
    scale = HEAD_DIM ** -0.5
    qb = q.reshape(B, nb, BLOCK_Q, N_KV_HEADS, G, HEAD_DIM)

    def band(t):
        tp = jnp.pad(t, ((0, 0), (BLOCK_Q, BLOCK_Q), (0, 0), (0, 0)))
        tp = tp.reshape(B, nb + 2, BLOCK_Q, N_KV_HEADS, HEAD_DIM)
        return jnp.concatenate([tp[:, :-2], tp[:, 1:-1], tp[:, 2:]], axis=2)

    kb, vb = band(k), band(v)
    s_band = jnp.einsum('bnqhgd,bnkhd->bnhgqk', qb, kb, preferred_element_type=jnp.float32) * scale
    qpos = jnp.arange(nb)[:, None, None] * BLOCK_Q + jnp.arange(BLOCK_Q)[None, :, None]
    kpos = jnp.arange(nb)[:, None, None] * BLOCK_Q - BLOCK_Q + jnp.arange(3 * BLOCK_Q)[None, None, :]
    valid = (jnp.abs(kpos - qpos) <= WINDOW) & (kpos >= 0) & (kpos < L)
    s_band = jnp.where(valid[None, :, None, None], s_band, NEG_INF)
    s_ctx = jnp.einsum('bnqhgd,bchd->bnhgqc', qb, k_ctx, preferred_element_type=jnp.float32) * scale
    s_sink = jnp.broadcast_to(sink.astype(jnp.float32).reshape(N_KV_HEADS, G)[None, None, :, :, None, None],
                              s_ctx.shape[:-1] + (1,))
    p = jax.nn.softmax(jnp.concatenate([s_sink, s_ctx, s_band], axis=-1), axis=-1)
    p_ctx = p[..., 1:1 + C].astype(v.dtype)
    p_band = p[..., 1 + C:].astype(v.dtype)
    o = (jnp.einsum('bnhgqc,bchd->bnqhgd', p_ctx, v_ctx)
         + jnp.einsum('bnhgqk,bnkhd->bnqhgd', p_band, vb))
    return o.reshape(B, L, D_ATTN)


def context_attention(q, k, v, sink):
    B, C = q.shape[:2]
    G = N_HEADS // N_KV_HEADS
    qg = q.reshape(B, C, N_KV_HEADS, G, HEAD_DIM)
    s = jnp.einsum('bqhgd,bkhd->bhgqk', qg, k, preferred_element_type=jnp.float32) * HEAD_DIM ** -0.5
    s_sink = jnp.broadcast_to(sink.astype(jnp.float32).reshape(N_KV_HEADS, G)[None, :, :, None, None],
                              s.shape[:-1] + (1,))
    p = jax.nn.softmax(jnp.concatenate([s_sink, s], axis=-1), axis=-1)
    o = jnp.einsum('bhgqk,bkhd->bqhgd', p[..., 1:].astype(v.dtype), v)
    return o.reshape(B, C, D_ATTN)


def merge_groups(y_rnn, y_attn, g_grp, w_out):
    y = jnp.concatenate([rmsnorm(y_rnn, g_grp[:D_RNN]), rmsnorm(y_attn, g_grp[D_RNN:])], axis=-1)
    return y @ w_out


def hybrid_mixer(h_ctx, h_lat, w_in, conv_w, conv_b, w_rg, b_rg, lam, sink, g_grp, w_out,
                 rope_cos, rope_sin, with_ctx_out):
    split_at = [D_RNN, 2 * D_RNN, 2 * D_RNN + D_ATTN, 2 * D_RNN + D_ATTN + D_KV]
    xr_c, gr_c, q_c, k_c, v_c = jnp.split(h_ctx @ w_in, split_at, axis=-1)
    xr_l, gr_l, q_l, k_l, v_l = jnp.split(h_lat @ w_in, split_at, axis=-1)
    u_c = centred_dwconv(xr_c, conv_w, conv_b)
    u_l = centred_dwconv(xr_l, conv_w, conv_b)
    s_c, s_l = rglru_bidir(u_c, u_l, w_rg, b_rg, lam)
    y_rnn_l = s_l * jax.nn.gelu(gr_l)
    heads = lambda t, n: t.reshape(t.shape[:2] + (n, HEAD_DIM))
    kc, vc = heads(k_c, N_KV_HEADS), heads(v_c, N_KV_HEADS)
    q_lat = apply_axial_rope(heads(q_l, N_HEADS), rope_cos, rope_sin)
    k_lat = apply_axial_rope(heads(k_l, N_KV_HEADS), rope_cos, rope_sin)
    o_l = latent_window_attention(q_lat, k_lat, heads(v_l, N_KV_HEADS), kc, vc, sink)
    out_l = merge_groups(y_rnn_l, o_l, g_grp, w_out)
    if with_ctx_out:
        y_rnn_c = s_c * jax.nn.gelu(gr_c)
        o_c = context_attention(heads(q_c, N_HEADS), kc, vc, sink)
        out_c = merge_groups(y_rnn_c, o_c, g_grp, w_out)
    else:
        out_c = None
    return out_c, out_l


def swiglu(h, wg, wu, wd):
    return (jax.nn.silu(h @ wg) * (h @ wu)) @ wd


def moe_swiglu(h, w_router, wg, wu, wd):
    logits = (h @ w_router).astype(jnp.float32)
    top_vals, top_idx = lax.top_k(logits, TOP_K)
    weights = jax.nn.softmax(top_vals, axis=-1)
    combine = jnp.sum(jax.nn.one_hot(top_idx, N_EXPERTS, dtype=jnp.float32) * weights[..., None], axis=-2)
    combine = combine.astype(h.dtype)
    out = combine[..., 0:1] * swiglu(h, wg[0], wu[0], wd[0])
    for e in range(1, N_EXPERTS):
        out = out + combine[..., e:e + 1] * swiglu(h, wg[e], wu[e], wd[e])
    return out


def setup_inputs(seed: int = 0) -> dict:
    key = jax.random.key(seed)
    ks = iter(jax.random.split(key, 32))
    f32 = jnp.float32

    def nrm(shape, scale):
        return jax.random.normal(next(ks), shape, f32) * scale

    x = nrm((BATCH, SEQ, D_MODEL), 1.0)
    c = nrm((BATCH, D_MODEL), 1.0)
    ctx = nrm((BATCH, CTX_LEN, D_MODEL), 1.0)
    c_ctx = nrm((D_MODEL,), 1.0)
    w_mod = nrm((DEPTH, D_MODEL, 6 * D_MODEL), 0.5 * D_MODEL ** -0.5)
    b_mod = nrm((DEPTH, 6 * D_MODEL), 0.02)
    g_mix = 1.0 + nrm((DEPTH, D_MODEL), 0.02)
    g_ffn = 1.0 + nrm((DEPTH, D_MODEL), 0.02)
    w_in = nrm((DEPTH, D_MODEL, D_IN), D_MODEL ** -0.5)
    conv_w = nrm((DEPTH, CONV_W, D_RNN), CONV_W ** -0.5)
    conv_b = nrm((DEPTH, D_RNN), 0.02)
    w_rg = nrm((DEPTH, 2, 2, RNN_BLOCKS, RNN_BLOCK, RNN_BLOCK), RNN_BLOCK ** -0.5)
    b_rg = nrm((DEPTH, 2, 2, D_RNN), 0.1)
    u = jax.random.uniform(next(ks), (DEPTH, 2, D_RNN), f32, 0.9, 0.999)
    s = u ** (1.0 / LRU_C)
    lam = jnp.log(s) - jnp.log1p(-s)
    sink = nrm((DEPTH, N_HEADS), 0.5)
    g_grp = 1.0 + nrm((DEPTH, D_MIX), 0.02)
    w_out = nrm((DEPTH, D_MIX, D_MODEL), D_MIX ** -0.5)
    w_ffn_gate = nrm((N_DENSE, D_MODEL, D_FF), D_MODEL ** -0.5)
    w_ffn_up = nrm((N_DENSE, D_MODEL, D_FF), D_MODEL ** -0.5)
    w_ffn_down = nrm((N_DENSE, D_FF, D_MODEL), D_FF ** -0.5)
    w_router = nrm((N_MOE, D_MODEL, N_EXPERTS), D_MODEL ** -0.5)
    w_exp_gate = nrm((N_MOE, N_EXPERTS, D_MODEL, D_FF_EXPERT), D_MODEL ** -0.5)
    w_exp_up = nrm((N_MOE, N_EXPERTS, D_MODEL, D_FF_EXPERT), D_MODEL ** -0.5)
    w_exp_down = nrm((N_MOE, N_EXPERTS, D_FF_EXPERT, D_MODEL), D_FF_EXPERT ** -0.5)
    g_final = 1.0 + nrm((D_MODEL,), 0.02)
    return {"x": x, "c": c, "ctx": ctx, "c_ctx": c_ctx, "w_mod": w_mod, "b_mod": b_mod,
            "g_mix": g_mix, "g_ffn": g_ffn, "w_in": w_in, "conv_w": conv_w, "conv_b": conv_b,
            "w_rg": w_rg, "b_rg": b_rg, "lam": lam, "sink": sink, "g_grp": g_grp, "w_out": w_out,
            "w_ffn_gate": w_ffn_gate, "w_ffn_up": w_ffn_up, "w_ffn_down": w_ffn_down,
            "w_router": w_router, "w_exp_gate": w_exp_gate, "w_exp_up": w_exp_up,
            "w_exp_down": w_exp_down, "g_final": g_final}


def reference(x, c, ctx, c_ctx, w_mod, b_mod, g_mix, g_ffn, w_in, conv_w, conv_b, w_rg, b_rg, lam,
              sink, g_grp, w_out, w_ffn_gate, w_ffn_up, w_ffn_down, w_router, w_exp_gate, w_exp_up,
              w_exp_down, g_final):
    n_rows = x.shape[1] // GRID_W
    rope_cos, rope_sin = axial_rope_tables(n_rows)
    xc = ctx
    for l in range(DEPTH):
        last = l == DEPTH - 1
        sh_m, sc_m, gt_m, sh_f, sc_f, gt_f = adaln(c, w_mod[l], b_mod[l])
        csh_m, csc_m, cgt_m, csh_f, csc_f, cgt_f = adaln(c_ctx, w_mod[l], b_mod[l])
        h_lat = modulate(rmsnorm(x, g_mix[l]), sh_m, sc_m)
        h_ctx = modulate(rmsnorm(xc, g_mix[l]), csh_m, csc_m)
        o_ctx, o_lat = hybrid_mixer(h_ctx, h_lat, w_in[l], conv_w[l], conv_b[l], w_rg[l], b_rg[l], lam[l],
                                    sink[l], g_grp[l], w_out[l], rope_cos, rope_sin, not last)
        x = x + gt_m * o_lat
        if not last:
            xc = xc + cgt_m * o_ctx
        if l % 2 == 0:
            j = l // 2
            ffn = lambda h: swiglu(h, w_ffn_gate[j], w_ffn_up[j], w_ffn_down[j])
        else:
            j = l // 2
            ffn = lambda h: moe_swiglu(h, w_router[j], w_exp_gate[j], w_exp_up[j], w_exp_down[j])
        x = x + gt_f * ffn(modulate(rmsnorm(x, g_ffn[l]), sh_f, sc_f))
        if not last:
            xc = xc + cgt_f * ffn(modulate(rmsnorm(xc, g_ffn[l]), csh_f, csc_f))
    return rmsnorm(x, g_final)
```

```python
import functools

import jax
import jax.numpy as jnp
from jax import lax
from jax.experimental import pallas as pl
from jax.experimental.pallas import tpu as pltpu

F32 = jnp.float32
BF16 = jnp.bfloat16

D_MODEL = 1024
D_RNN = 512
D_ATTN = 512
D_KV = 128
HEAD_DIM = 64
N_HEADS = 8
N_KV_HEADS = 2
HEADS_PER_KV = N_HEADS // N_KV_HEADS
D_IN = 2 * D_RNN + D_ATTN + 2 * D_KV
RNN_BLOCK = 64
RNN_CHUNK = 256
CONV_W = 4
LRU_C = 8.0
WINDOW = 128
BLOCK_Q = 128
GRID_W = 64
ROPE_BASE = 10000.0
N_EXPERTS = 8
EPS = 1e-6
NEG_INF = -1e30

LANES = 128
SUBLANES = 8
VMEM_LIMIT = 56 * 1024 * 1024

SH_M, SC_M, GT_M, SH_F, SC_F, GT_F = (k * D_MODEL for k in range(6))


def _params(*sem):
    return pltpu.CompilerParams(dimension_semantics=sem, vmem_limit_bytes=VMEM_LIMIT)


def _rms(x):
    return x * lax.rsqrt(jnp.mean(x * x, axis=-1, keepdims=True) + EPS)


def _adaln_kernel(c_ref, w_ref, b_ref, o_ref):
    c = c_ref[...]
    s = c * jax.nn.sigmoid(c)
    o_ref[0] = jnp.dot(s, w_ref[0], preferred_element_type=F32,
                       precision=lax.Precision.HIGHEST) + b_ref[0]


def adaln_all(cond8, w_mod, b_mod):
    depth = w_mod.shape[0]
    nc = 1536
    return pl.pallas_call(
        _adaln_kernel,
        grid=(depth, 6 * D_MODEL // nc),
        in_specs=[
            pl.BlockSpec((SUBLANES, D_MODEL), lambda l, j: (0, 0)),
            pl.BlockSpec((1, D_MODEL, nc), lambda l, j: (l, 0, j)),
            pl.BlockSpec((1, 1, nc), lambda l, j: (l, 0, j)),
        ],
        out_specs=pl.BlockSpec((1, SUBLANES, nc), lambda l, j: (l, 0, j)),
        out_shape=jax.ShapeDtypeStruct((depth, SUBLANES, 6 * D_MODEL), F32),
        compiler_params=_params("arbitrary", "arbitrary"),
        name="adaln",
    )(cond8, w_mod, b_mod.reshape(depth, 1, 6 * D_MODEL))


ROPE_ROWS = SUBLANES * GRID_W


def _rope_kernel(f_ref, c_ref, s_ref):
    i = pl.program_id(0)
    lane = lax.broadcasted_iota(jnp.int32, (GRID_W, LANES), 1)
    col_axis = ((lane >> 5) & 1) == 1
    sign = jnp.where(((lane >> 4) & 1) == 0, -1.0, 1.0).astype(F32)
    freq = f_ref[...]
    rowpos = (i * SUBLANES + lax.broadcasted_iota(jnp.int32, (SUBLANES, LANES), 0)).astype(F32)
    colpos = lax.broadcasted_iota(jnp.int32, (GRID_W, LANES), 0).astype(F32)
    ang_r = rowpos * freq
    ang_c = colpos * freq
    cr, sr = jnp.cos(ang_r), jnp.sin(ang_r)
    cc, sc = jnp.cos(ang_c), jnp.sin(ang_c) * sign
    for g in range(SUBLANES):
        rows = slice(g * GRID_W, (g + 1) * GRID_W)
        c_ref[rows, :] = jnp.where(col_axis, cc, jnp.broadcast_to(cr[g:g + 1, :], (GRID_W, LANES)))
        s_ref[rows, :] = jnp.where(col_axis, sc, jnp.broadcast_to(sr[g:g + 1, :], (GRID_W, LANES)) * sign)


def rope_tables(seq):
    axis_dim = HEAD_DIM // 2
    freqs = ROPE_BASE ** (-jnp.arange(0, axis_dim, 2, dtype=F32) / axis_dim)
    freq_lane = jnp.tile(freqs, LANES // freqs.shape[0]).reshape(1, LANES)
    return pl.pallas_call(
        _rope_kernel,
        grid=(seq // ROPE_ROWS,),
        in_specs=[pl.BlockSpec((1, LANES), lambda i: (0, 0))],
        out_specs=[pl.BlockSpec((ROPE_ROWS, LANES), lambda i: (i, 0))] * 2,
        out_shape=[jax.ShapeDtypeStruct((seq, LANES), F32)] * 2,
        compiler_params=_params("arbitrary"),
        name="rope_tables",
    )(freq_lane)


def _rope_chunk(x, cos, sin):
    lane = lax.broadcasted_iota(jnp.int32, x.shape, 1)
    first_half = ((lane >> 4) & 1) == 0
    partner = jnp.where(first_half, pltpu.roll(x, LANES - 16, axis=1), pltpu.roll(x, 16, axis=1))
    return x * cos + partner * sin


def _in_proj_kernel(*refs, row, rope):
    if rope:
        x_ref, mod_ref, g_ref, w_ref, cos_ref, sin_ref, xr_ref, gr_ref, q_ref, kv_ref = refs
    else:
        x_ref, mod_ref, g_ref, w_ref, xr_ref, gr_ref, q_ref, kv_ref = refs
    x = x_ref[...]
    shift = mod_ref[0, row:row + 1, SH_M:SH_M + D_MODEL]
    scale = mod_ref[0, row:row + 1, SC_M:SC_M + D_MODEL]
    h = (_rms(x) * g_ref[...]) * (1.0 + scale) + shift
    p = jnp.dot(h.astype(BF16), w_ref[...], preferred_element_type=F32)
    xr_ref[...] = p[:, 0:D_RNN]
    gr_ref[...] = p[:, D_RNN:2 * D_RNN]
    q0 = 2 * D_RNN
    if rope:
        cos, sin = cos_ref[...], sin_ref[...]
    for c in range(D_ATTN // LANES):
        qc = p[:, q0 + c * LANES:q0 + (c + 1) * LANES]
        if rope:
            qc = _rope_chunk(qc, cos, sin)
        q_ref[:, c * LANES:(c + 1) * LANES] = (qc * (HEAD_DIM ** -0.5)).astype(BF16)
    k = p[:, q0 + D_ATTN:q0 + D_ATTN + D_KV]
    v = p[:, q0 + D_ATTN + D_KV:q0 + D_ATTN + 2 * D_KV]
    if rope:
        k = _rope_chunk(k, cos, sin)
    kv_ref[:, 0:LANES] = k.astype(BF16)
    kv_ref[:, LANES:2 * LANES] = pltpu.roll(k, HEAD_DIM, axis=1).astype(BF16)
    kv_ref[:, 2 * LANES:3 * LANES] = v.astype(BF16)
    kv_ref[:, 3 * LANES:4 * LANES] = pltpu.roll(v, HEAD_DIM, axis=1).astype(BF16)


def in_proj(x, mod, g_mix, w_in, cos, sin, *, row, tm):
    n = x.shape[0]
    rope = cos is not None
    tile = lambda w: pl.BlockSpec((tm, w), lambda i: (i, 0))
    in_specs = [
        tile(D_MODEL),
        pl.BlockSpec((1, SUBLANES, 6 * D_MODEL), lambda i: (0, 0, 0)),
        pl.BlockSpec((1, D_MODEL), lambda i: (0, 0)),
        pl.BlockSpec((D_MODEL, D_IN), lambda i: (0, 0)),
    ]
    args = [x, mod, g_mix, w_in]
    if rope:
        in_specs += [tile(LANES), tile(LANES)]
        args += [cos, sin]
    return pl.pallas_call(
        functools.partial(_in_proj_kernel, row=row, rope=rope),
        grid=(n // tm,),
        in_specs=in_specs,
        out_specs=[tile(D_RNN), tile(D_RNN), tile(D_ATTN), tile(4 * LANES)],
        out_shape=[
            jax.ShapeDtypeStruct((n, D_RNN), F32),
            jax.ShapeDtypeStruct((n, D_RNN), F32),
            jax.ShapeDtypeStruct((n, D_ATTN), BF16),
            jax.ShapeDtypeStruct((n, 4 * LANES), BF16),
        ],
        compiler_params=_params("arbitrary"),
        name="in_proj",
    )(*args)


def _rnn_kernel(pf_ref, mf_ref, nf_ref, pb_ref, mb_ref, nb_ref, cw_ref, cb_ref, wg_ref, bg_ref,
                lam_ref, h0_ref, hf_ref, hb_ref, hl_ref, ext_ref, a_ref, b_ref, cf_ref, cbk_ref,
                *, tm, nt):
    i = pl.program_id(0)
    ngroups = tm // SUBLANES

    @pl.when(i == 0)
    def _():
        cf_ref[...] = jnp.broadcast_to(h0_ref[0:1, :], (SUBLANES, D_RNN))
        cbk_ref[...] = jnp.broadcast_to(h0_ref[1:2, :], (SUBLANES, D_RNN))

    softplus_neg_lam = jax.nn.softplus(-lam_ref[...])
    sub = lax.broadcasted_iota(jnp.int32, (tm, RNN_CHUNK), 0) & (SUBLANES - 1)

    def run(d, prev_ref, main_ref, next_ref, at_start, at_end, out_ref, carry_ref):
        ext_ref[0:SUBLANES, :] = jnp.where(at_start, 0.0, prev_ref[...])
        ext_ref[SUBLANES:tm + SUBLANES, :] = main_ref[...]
        ext_ref[tm + SUBLANES:tm + 2 * SUBLANES, :] = jnp.where(at_end, 0.0, next_ref[...])
        u = cb_ref[...]
        for k in range(CONV_W):
            start = SUBLANES - CONV_W // 2 + k
            u = u + ext_ref[start:start + tm, :] * cw_ref[k:k + 1, :]
        ub = u.astype(BF16)
        for c in range(D_RNN // RNN_CHUNK):
            ch = slice(c * RNN_CHUNK, (c + 1) * RNN_CHUNK)
            g = jnp.dot(ub[:, ch], wg_ref[d, c], preferred_element_type=F32)
            r = jax.nn.sigmoid(g[:, :RNN_CHUNK] + bg_ref[d, 0:1, ch])
            ig = jax.nn.sigmoid(g[:, RNN_CHUNK:] + bg_ref[d, 1:2, ch])
            log_a = (-LRU_C * r) * softplus_neg_lam[d:d + 1, ch]
            a = jnp.exp(log_a)
            mult = jnp.sqrt(jnp.maximum(1.0 - a * a, 0.0))
            b = mult * (ig * u[:, ch])
            for dd in (1, 2, 4):
                if d == 0:
                    a_sh, b_sh, ok = pltpu.roll(a, dd, axis=0), pltpu.roll(b, dd, axis=0), sub >= dd
                else:
                    a_sh, b_sh = pltpu.roll(a, tm - dd, axis=0), pltpu.roll(b, tm - dd, axis=0)
                    ok = sub < SUBLANES - dd
                b = a * jnp.where(ok, b_sh, 0.0) + b
                a = a * jnp.where(ok, a_sh, 1.0)
            a_ref[:, ch] = a
            b_ref[:, ch] = b

        def body(g, carry):
            gg = g if d == 0 else ngroups - 1 - g
            off = pl.multiple_of(gg * SUBLANES, SUBLANES)
            h = b_ref[pl.ds(off, SUBLANES), :] + a_ref[pl.ds(off, SUBLANES), :] * carry
            out_ref[pl.ds(off, SUBLANES), :] = h
            edge = h[SUBLANES - 1:SUBLANES, :] if d == 0 else h[0:1, :]
            return jnp.broadcast_to(edge, (SUBLANES, D_RNN))

        carry_ref[...] = lax.fori_loop(0, ngroups, body, carry_ref[...], unroll=8)

    run(0, pf_ref, mf_ref, nf_ref, i == 0, i == nt - 1, hf_ref, cf_ref)
    run(1, pb_ref, mb_ref, nb_ref, i == nt - 1, i == 0, hb_ref, cbk_ref)

    @pl.when(i == nt - 1)
    def _():
        rows = lax.broadcasted_iota(jnp.int32, (SUBLANES, D_RNN), 0)
        hl_ref[...] = jnp.where(rows == 0, cf_ref[...], jnp.where(rows == 1, cbk_ref[...], 0.0))


def rnn_bidir(xr, conv_w, conv_b, wg, bg, lam, h0, *, tm):
    n = xr.shape[0]
    nt = n // tm
    per = tm // SUBLANES
    last8 = n // SUBLANES - 1
    fwd = lambda i: i
    bwd = lambda i: nt - 1 - i
    main = lambda t: pl.BlockSpec((tm, D_RNN), lambda i: (t(i), 0))
    prev = lambda t: pl.BlockSpec((SUBLANES, D_RNN), lambda i: (jnp.maximum(t(i) * per - 1, 0), 0))
    nxt = lambda t: pl.BlockSpec((SUBLANES, D_RNN), lambda i: (jnp.minimum((t(i) + 1) * per, last8), 0))
    whole = lambda a: pl.BlockSpec(a.shape, lambda i: (0,) * a.ndim)
    return pl.pallas_call(
        functools.partial(_rnn_kernel, tm=tm, nt=nt),
        grid=(nt,),
        in_specs=[prev(fwd), main(fwd), nxt(fwd), prev(bwd), main(bwd), nxt(bwd),
                  whole(conv_w), whole(conv_b), whole(wg), whole(bg), whole(lam), whole(h0)],
        out_specs=[main(fwd), main(bwd), pl.BlockSpec((SUBLANES, D_RNN), lambda i: (0, 0))],
        out_shape=[
            jax.ShapeDtypeStruct((n, D_RNN), F32),
            jax.ShapeDtypeStruct((n, D_RNN), F32),
            jax.ShapeDtypeStruct((SUBLANES, D_RNN), F32),
        ],
        scratch_shapes=[
            pltpu.VMEM((tm + 2 * SUBLANES, D_RNN), F32),
            pltpu.VMEM((tm, D_RNN), F32),
            pltpu.VMEM((tm, D_RNN), F32),
            pltpu.VMEM((SUBLANES, D_RNN), F32),
            pltpu.VMEM((SUBLANES, D_RNN), F32),
        ],
        compiler_params=_params("arbitrary"),
        name="rnn_bidir",
    )(xr, xr, xr, xr, xr, xr, conv_w, conv_b, wg, bg, lam, h0)


def _attn_kernel(*refs, nb, band, n_ctx):
    if band:
        sink_ref, q_ref, kvx_ref, kvp_ref, kvc_ref, kvn_ref, o_ref = refs
        kv = jnp.concatenate([kvx_ref[...], kvp_ref[...], kvc_ref[...], kvn_ref[...]], axis=0)
    else:
        sink_ref, q_ref, kvx_ref, o_ref = refs
        kv = kvx_ref[...]
    i = pl.program_id(0)
    nk = kv.shape[0]
    lane = lax.broadcasted_iota(jnp.int32, (nk, LANES), 1)
    low = lane < HEAD_DIM
    zero = jnp.zeros((), BF16)
    k, ks, v, vs = (kv[:, j * LANES:(j + 1) * LANES] for j in range(4))
    k_low = (jnp.where(low, k, zero), jnp.where(low, ks, zero))
    k_high = (jnp.where(low, zero, ks), jnp.where(low, zero, k))
    v_low = (jnp.where(low, v, zero), jnp.where(low, vs, zero))
    v_high = (jnp.where(low, zero, vs), jnp.where(low, zero, v))

    rows2 = 2 * BLOCK_Q
    r = lax.broadcasted_iota(jnp.int32, (rows2, nk), 0) & (BLOCK_Q - 1)
    if band:
        jb = lax.broadcasted_iota(jnp.int32, (rows2, nk), 1) - n_ctx
        valid = (jb < 0) | ((jb >= r) & (jb <= r + 2 * WINDOW)
                            & ((jb >= BLOCK_Q) | (i > 0)) & ((jb < 2 * BLOCK_Q) | (i < nb - 1)))
    upper = lax.broadcasted_iota(jnp.int32, (rows2, 1), 0) >= BLOCK_Q

    def probs(q2, kmat, sink_a, sink_b):
        s = lax.dot_general(q2, kmat, (((1,), (1,)), ((), ())), preferred_element_type=F32)
        if band:
            s = jnp.where(valid, s, NEG_INF)
        sink = jnp.where(upper, sink_b, sink_a)
        m = jnp.maximum(jnp.max(s, axis=-1, keepdims=True), sink)
        p = jnp.exp(s - m)
        den = jnp.sum(p, axis=-1, keepdims=True) + jnp.exp(sink - m)
        return (p / den).astype(BF16)

    for hk in range(N_KV_HEADS):
        c0, c1 = 2 * hk, 2 * hk + 1
        q2 = jnp.concatenate([q_ref[:, c0 * LANES:(c0 + 1) * LANES],
                              q_ref[:, c1 * LANES:(c1 + 1) * LANES]], axis=0)
        h0 = hk * HEADS_PER_KV
        p_low = probs(q2, k_low[hk], sink_ref[h0], sink_ref[h0 + 2])
        p_high = probs(q2, k_high[hk], sink_ref[h0 + 1], sink_ref[h0 + 3])
        o2 = (jnp.dot(p_low, v_low[hk], preferred_element_type=F32)
              + jnp.dot(p_high, v_high[hk], preferred_element_type=F32))
        o_ref[:, c0 * LANES:(c0 + 1) * LANES] = o2[:BLOCK_Q]
        o_ref[:, c1 * LANES:(c1 + 1) * LANES] = o2[BLOCK_Q:]


def attention(q, kv, kv_ctx, sink, *, band):
    n = q.shape[0]
    nb = n // BLOCK_Q
    n_ctx = kv_ctx.shape[0]
    in_specs = [
        pl.BlockSpec(memory_space=pltpu.SMEM),
        pl.BlockSpec((BLOCK_Q, D_ATTN), lambda i: (i, 0)),
        pl.BlockSpec((n_ctx, 4 * LANES), lambda i: (0, 0)),
    ]
    args = [sink, q, kv_ctx]
    if band:
        blk = lambda f: pl.BlockSpec((BLOCK_Q, 4 * LANES), lambda i: (f(i), 0))
        in_specs += [blk(lambda i: jnp.maximum(i - 1, 0)), blk(lambda i: i),
                     blk(lambda i: jnp.minimum(i + 1, nb - 1))]
        args += [kv, kv, kv]
    return pl.pallas_call(
        functools.partial(_attn_kernel, nb=nb, band=band, n_ctx=n_ctx),
        grid=(nb,),
        in_specs=in_specs,
        out_specs=pl.BlockSpec((BLOCK_Q, D_ATTN), lambda i: (i, 0)),
        out_shape=jax.ShapeDtypeStruct((n, D_ATTN), F32),
        compiler_params=_params("arbitrary"),
        name="attention",
    )(*args)


def _merge_kernel(*refs, row, moe):
    if moe:
        (x_ref, hf_ref, hb_ref, gr_ref, o_ref, mod_ref, gg_ref, wo_ref, gf_ref, wrh_ref, wrl_ref,
         xo_ref, h2_ref, comb_ref) = refs
    else:
        x_ref, hf_ref, hb_ref, gr_ref, o_ref, mod_ref, gg_ref, wo_ref, gf_ref, xo_ref, h2_ref = refs
    mod = lambda off: mod_ref[0, row:row + 1, off:off + D_MODEL]
    y_rnn = (hf_ref[...] + hb_ref[...]) * jax.nn.gelu(gr_ref[...])
    y = jnp.concatenate([_rms(y_rnn) * gg_ref[:, :D_RNN], _rms(o_ref[...]) * gg_ref[:, D_RNN:]], axis=1)
    out = jnp.dot(y.astype(BF16), wo_ref[...], preferred_element_type=F32)
    x = x_ref[...] + mod(GT_M) * out
    xo_ref[...] = x
    h2 = (_rms(x) * gf_ref[...]) * (1.0 + mod(SC_F)) + mod(SH_F)
    h2_ref[...] = h2.astype(BF16)
    if moe:
        hi = h2.astype(BF16)
        lo = (h2 - hi.astype(F32)).astype(BF16)
        dot = lambda a, b: jnp.dot(a, b, preferred_element_type=F32)
        logits = dot(hi, wrh_ref[...]) + (dot(hi, wrl_ref[...]) + dot(lo, wrh_ref[...]))
        lane = lax.broadcasted_iota(jnp.int32, logits.shape, 1)
        logits = jnp.where(lane < N_EXPERTS, logits, -jnp.inf)
        v1 = jnp.max(logits, axis=-1, keepdims=True)
        i1 = jnp.min(jnp.where(logits == v1, lane, LANES), axis=-1, keepdims=True)
        rest = jnp.where(lane == i1, -jnp.inf, logits)
        v2 = jnp.max(rest, axis=-1, keepdims=True)
        i2 = jnp.min(jnp.where(rest == v2, lane, LANES), axis=-1, keepdims=True)
        e2 = jnp.exp(v2 - v1)
        w1 = 1.0 / (1.0 + e2)
        w2 = e2 / (1.0 + e2)
        comb_ref[...] = jnp.where(lane == i1, w1, 0.0) + jnp.where(lane == i2, w2, 0.0)


def merge(x, hf, hb, gr, o, mod, g_grp, w_out, g_ffn, router, *, row, tm):
    n = x.shape[0]
    moe = router is not None
    tile = lambda w: pl.BlockSpec((tm, w), lambda i: (i, 0))
    whole = lambda a: pl.BlockSpec(a.shape, lambda i: (0,) * a.ndim)
    in_specs = [tile(D_MODEL), tile(D_RNN), tile(D_RNN), tile(D_RNN), tile(D_ATTN),
                whole(mod), whole(g_grp), whole(w_out), whole(g_ffn)]
    args = [x, hf, hb, gr, o, mod, g_grp, w_out, g_ffn]
    out_specs = [tile(D_MODEL), tile(D_MODEL)]
    out_shape = [jax.ShapeDtypeStruct((n, D_MODEL), F32), jax.ShapeDtypeStruct((n, D_MODEL), BF16)]
    if moe:
        in_specs += [whole(router[0]), whole(router[1])]
        args += list(router)
        out_specs.append(tile(LANES))
        out_shape.append(jax.ShapeDtypeStruct((n, LANES), F32))
    return pl.pallas_call(
        functools.partial(_merge_kernel, row=row, moe=moe),
        grid=(n // tm,),
        in_specs=in_specs,
        out_specs=out_specs,
        out_shape=out_shape,
        compiler_params=_params("arbitrary"),
        name="merge",
    )(*args)


def _ffn_kernel(*refs, row, n_exp, nf, moe, final):
    refs = list(refs)
    h_ref, x_ref, mod_ref = refs[:3]
    pos = 3
    comb_ref = None
    if moe:
        comb_ref = refs[pos]
        pos += 1
    gfin_ref = None
    if final:
        gfin_ref = refs[pos]
        pos += 1
    wg_ref, wu_ref, wd_ref, o_ref, acc_ref = refs[pos:pos + 5]
    tot_ref = refs[pos + 5] if moe else acc_ref
    e = pl.program_id(1)
    f = pl.program_id(2)

    @pl.when(f == 0)
    def _():
        acc_ref[...] = jnp.zeros_like(acc_ref)

    if moe:
        @pl.when((e == 0) & (f == 0))
        def _():
            tot_ref[...] = jnp.zeros_like(tot_ref)

    h = h_ref[...]
    a = jnp.dot(h, wg_ref[0], preferred_element_type=F32)
    b = jnp.dot(h, wu_ref[0], preferred_element_type=F32)
    act = (a * jax.nn.sigmoid(a)) * b
    acc_ref[...] += jnp.dot(act.astype(BF16), wd_ref[0], preferred_element_type=F32)

    if moe:
        @pl.when(f == nf - 1)
        def _():
            comb = comb_ref[...]
            lane = lax.broadcasted_iota(jnp.int32, comb.shape, 1)
            ce = jnp.sum(jnp.where(lane == e, comb, 0.0), axis=-1, keepdims=True)
            tot_ref[...] += ce * acc_ref[...]

    @pl.when((e == n_exp - 1) & (f == nf - 1))
    def _():
        x = x_ref[...] + mod_ref[0, row:row + 1, GT_F:GT_F + D_MODEL] * tot_ref[...]
        if final:
            x = _rms(x) * gfin_ref[...]
        o_ref[...] = x


def ffn(h2, x, mod, comb, g_final, wg, wu, wd, *, row, tm, tf):
    n = x.shape[0]
    n_exp, _, d_ff = wg.shape
    nf = d_ff // tf
    moe = comb is not None
    final = g_final is not None
    tile = lambda w: pl.BlockSpec((tm, w), lambda t, e, f: (t, 0))
    in_specs = [tile(D_MODEL), tile(D_MODEL), pl.BlockSpec(mod.shape, lambda t, e, f: (0, 0, 0))]
    args = [h2, x, mod]
    if moe:
        in_specs.append(tile(LANES))
        args.append(comb)
    if final:
        in_specs.append(pl.BlockSpec((1, D_MODEL), lambda t, e, f: (0, 0)))
        args.append(g_final)
    in_specs += [
        pl.BlockSpec((1, D_MODEL, tf), lambda t, e, f: (e, 0, f)),
        pl.BlockSpec((1, D_MODEL, tf), lambda t, e, f: (e, 0, f)),
        pl.BlockSpec((1, tf, D_MODEL), lambda t, e, f: (e, f, 0)),
    ]
    args += [wg, wu, wd]
    scratch = [pltpu.VMEM((tm, D_MODEL), F32)]
    if moe:
        scratch.append(pltpu.VMEM((tm, D_MODEL), F32))
    return pl.pallas_call(
        functools.partial(_ffn_kernel, row=row, n_exp=n_exp, nf=nf, moe=moe, final=final),
        grid=(n // tm, n_exp, nf),
        in_specs=in_specs,
        out_specs=tile(D_MODEL),
        out_shape=jax.ShapeDtypeStruct((n, D_MODEL), F32),
        scratch_shapes=scratch,
        compiler_params=_params("arbitrary", "arbitrary", "arbitrary"),
        name="ffn",
    )(*args)


def _gate_weights(w_rg):
    depth = w_rg.shape[0]
    per = RNN_CHUNK // RNN_BLOCK
    wr = w_rg.reshape(depth, 2, 2, D_RNN // RNN_CHUNK, per, RNN_BLOCK, RNN_BLOCK)
    bd = jnp.zeros((depth, 2, 2, D_RNN // RNN_CHUNK, per, RNN_BLOCK, per, RNN_BLOCK), w_rg.dtype)
    for g in range(per):
        bd = bd.at[:, :, :, :, g, :, g, :].set(wr[:, :, :, :, g])
    bd = bd.reshape(depth, 2, 2, D_RNN // RNN_CHUNK, RNN_CHUNK, RNN_CHUNK)
    return jnp.concatenate([bd[:, :, 0], bd[:, :, 1]], axis=-1).astype(BF16)


def _tile_rows(n, want):
    return want if n % want == 0 else n


def kernel(x, c, ctx, c_ctx, w_mod, b_mod, g_mix, g_ffn, w_in, conv_w, conv_b, w_rg, b_rg, lam,
           sink, g_grp, w_out, w_ffn_gate, w_ffn_up, w_ffn_down, w_router, w_exp_gate, w_exp_up,
           w_exp_down, g_final):
    assert x.shape[0] == 1 and ctx.shape[0] == 1
    depth = w_mod.shape[0]
    seq, n_ctx = x.shape[1], ctx.shape[1]
    xl, xc = x[0], ctx[0]

    cond8 = jnp.zeros((SUBLANES, D_MODEL), F32).at[0].set(c[0]).at[1].set(c_ctx)
    mods = adaln_all(cond8, w_mod, b_mod)
    cos, sin = rope_tables(seq)

    w_in_b = w_in.astype(BF16)
    w_out_b = w_out.astype(BF16)
    wg_gate = _gate_weights(w_rg)
    ffn_w = (w_ffn_gate.astype(BF16), w_ffn_up.astype(BF16), w_ffn_down.astype(BF16))
    exp_w = (w_exp_gate.astype(BF16), w_exp_up.astype(BF16), w_exp_down.astype(BF16))
    wr_pad = jnp.zeros((w_router.shape[0], D_MODEL, LANES), F32).at[:, :, :N_EXPERTS].set(w_router)
    wr_hi = wr_pad.astype(BF16)
    wr_lo = (wr_pad - wr_hi.astype(F32)).astype(BF16)
    zeros_h0 = jnp.zeros((SUBLANES, D_RNN), F32)

    tm = _tile_rows(seq, 512)
    tm_ffn = _tile_rows(seq, 1024)

    for l in range(depth):
        last = l == depth - 1
        moe = l % 2 == 1
        j = l // 2
        mod = mods[l:l + 1]
        gm, gf, gg = g_mix[l:l + 1], g_ffn[l:l + 1], g_grp[l:l + 1]

        xr_c, gr_c, q_c, kv_c = in_proj(xc, mod, gm, w_in_b[l], None, None, row=1, tm=n_ctx)
        xr_l, gr_l, q_l, kv_l = in_proj(xl, mod, gm, w_in_b[l], cos, sin, row=0, tm=tm)

        rnn_args = (conv_w[l], conv_b[l:l + 1], wg_gate[l], b_rg[l], lam[l])
        hf_c, hb_c, h_ctx_end = rnn_bidir(xr_c, *rnn_args, zeros_h0, tm=n_ctx)
        hf_l, hb_l, _ = rnn_bidir(xr_l, *rnn_args, h_ctx_end, tm=tm)

        o_l = attention(q_l, kv_l, kv_c, sink[l], band=True)

        router = (wr_hi[j], wr_lo[j]) if moe else None
        weights = tuple(w[j] for w in exp_w) if moe else tuple(w[j:j + 1] for w in ffn_w)
        tf = 512

        res = merge(xl, hf_l, hb_l, gr_l, o_l, mod, gg, w_out_b[l], gf, router, row=0, tm=tm)
        xl, h2 = res[0], res[1]
        comb = res[2] if moe else None
        xl = ffn(h2, xl, mod, comb, g_final.reshape(1, D_MODEL) if last else None, *weights,
                 row=0, tm=tm_ffn, tf=tf)

        if not last:
            o_c = attention(q_c, kv_c, kv_c, sink[l], band=False)
            res = merge(xc, hf_c, hb_c, gr_c, o_c, mod, gg, w_out_b[l], gf, router, row=1, tm=n_ctx)
            xc, h2c = res[0], res[1]
            comb_c = res[2] if moe else None
            xc = ffn(h2c, xc, mod, comb_c, None, *weights, row=1, tm=n_ctx, tf=tf)

    return xl[None]
```

```python
import functools

import jax
import jax.numpy as jnp
from jax import lax
from jax.experimental import pallas as pl
from jax.experimental.pallas import tpu as pltpu

F32 = jnp.float32
BF16 = jnp.bfloat16

D_MODEL = 1024
D_RNN = 512
D_ATTN = 512
D_KV = 128
HEAD_DIM = 64
N_HEADS = 8
N_KV_HEADS = 2
HEADS_PER_KV = N_HEADS // N_KV_HEADS
D_IN = 2 * D_RNN + D_ATTN + 2 * D_KV
RNN_BLOCK = 64
RNN_CHUNK = 256
CONV_W = 4
LRU_C = 8.0
WINDOW = 128
BLOCK_Q = 128
GRID_W = 64
ROPE_BASE = 10000.0
N_EXPERTS = 8
EPS = 1e-6
NEG_INF = -1e30

LANES = 128
SUBLANES = 8
VMEM_LIMIT = 56 * 1024 * 1024

SH_M, SC_M, GT_M, SH_F, SC_F, GT_F = (k * D_MODEL for k in range(6))


def _params(*sem):
    return pltpu.CompilerParams(dimension_semantics=sem, vmem_limit_bytes=VMEM_LIMIT)


def _rms(x):
    return x * lax.rsqrt(jnp.mean(x * x, axis=-1, keepdims=True) + EPS)


def _adaln_kernel(c_ref, w_ref, b_ref, o_ref):
    c = c_ref[...]
    s = c * jax.nn.sigmoid(c)
    o_ref[0] = jnp.dot(s, w_ref[0], preferred_element_type=F32,
                       precision=lax.Precision.HIGHEST) + b_ref[0]


def adaln_all(cond8, w_mod, b_mod):
    depth = w_mod.shape[0]
    nc = 1536
    return pl.pallas_call(
        _adaln_kernel,
        grid=(depth, 6 * D_MODEL // nc),
        in_specs=[
            pl.BlockSpec((SUBLANES, D_MODEL), lambda l, j: (0, 0)),
            pl.BlockSpec((1, D_MODEL, nc), lambda l, j: (l, 0, j)),
            pl.BlockSpec((1, 1, nc), lambda l, j: (l, 0, j)),
        ],
        out_specs=pl.BlockSpec((1, SUBLANES, nc), lambda l, j: (l, 0, j)),
        out_shape=jax.ShapeDtypeStruct((depth, SUBLANES, 6 * D_MODEL), F32),
        compiler_params=_params("arbitrary", "arbitrary"),
        name="adaln",
    )(cond8, w_mod, b_mod.reshape(depth, 1, 6 * D_MODEL))


ROPE_ROWS = SUBLANES * GRID_W


def _rope_kernel(f_ref, c_ref, s_ref):
    i = pl.program_id(0)
    lane = lax.broadcasted_iota(jnp.int32, (GRID_W, LANES), 1)
    col_axis = ((lane >> 5) & 1) == 1
    sign = jnp.where(((lane >> 4) & 1) == 0, -1.0, 1.0).astype(F32)
    freq = f_ref[...]
    rowpos = (i * SUBLANES + lax.broadcasted_iota(jnp.int32, (SUBLANES, LANES), 0)).astype(F32)
    colpos = lax.broadcasted_iota(jnp.int32, (GRID_W, LANES), 0).astype(F32)
    ang_r = rowpos * freq
    ang_c = colpos * freq
    cr, sr = jnp.cos(ang_r), jnp.sin(ang_r)
    cc, sc = jnp.cos(ang_c), jnp.sin(ang_c) * sign
    for g in range(SUBLANES):
        rows = slice(g * GRID_W, (g + 1) * GRID_W)
        c_ref[rows, :] = jnp.where(col_axis, cc, jnp.broadcast_to(cr[g:g + 1, :], (GRID_W, LANES)))
        s_ref[rows, :] = jnp.where(col_axis, sc, jnp.broadcast_to(sr[g:g + 1, :], (GRID_W, LANES)) * sign)


def rope_tables(seq):
    axis_dim = HEAD_DIM // 2
    freqs = ROPE_BASE ** (-jnp.arange(0, axis_dim, 2, dtype=F32) / axis_dim)
    freq_lane = jnp.tile(freqs, LANES // freqs.shape[0]).reshape(1, LANES)
    return pl.pallas_call(
        _rope_kernel,
        grid=(seq // ROPE_ROWS,),
        in_specs=[pl.BlockSpec((1, LANES), lambda i: (0, 0))],
        out_specs=[pl.BlockSpec((ROPE_ROWS, LANES), lambda i: (i, 0))] * 2,
        out_shape=[jax.ShapeDtypeStruct((seq, LANES), F32)] * 2,
        compiler_params=_params("arbitrary"),
        name="rope_tables",
    )(freq_lane)


def _rope_chunk(x, cos, sin):
    lane = lax.broadcasted_iota(jnp.int32, x.shape, 1)
    first_half = ((lane >> 4) & 1) == 0
    partner = jnp.where(first_half, pltpu.roll(x, LANES - 16, axis=1), pltpu.roll(x, 16, axis=1))
    return x * cos + partner * sin


def _in_proj_kernel(*refs, row, rope):
    if rope:
        x_ref, mod_ref, g_ref, w_ref, cos_ref, sin_ref, xr_ref, gr_ref, q_ref, kv_ref = refs
    else:
        x_ref, mod_ref, g_ref, w_ref, xr_ref, gr_ref, q_ref, kv_ref = refs
    x = x_ref[...]
    shift = mod_ref[0, row:row + 1, SH_M:SH_M + D_MODEL]
    scale = mod_ref[0, row:row + 1, SC_M:SC_M + D_MODEL]
    h = (_rms(x) * g_ref[...]) * (1.0 + scale) + shift
    p = jnp.dot(h.astype(BF16), w_ref[...], preferred_element_type=F32)
    xr_ref[...] = p[:, 0:D_RNN]
    gr_ref[...] = p[:, D_RNN:2 * D_RNN]
    q0 = 2 * D_RNN
    if rope:
        cos, sin = cos_ref[...], sin_ref[...]
    for c in range(D_ATTN // LANES):
        qc = p[:, q0 + c * LANES:q0 + (c + 1) * LANES]
        if rope:
            qc = _rope_chunk(qc, cos, sin)
        q_ref[:, c * LANES:(c + 1) * LANES] = (qc * (HEAD_DIM ** -0.5)).astype(BF16)
    k = p[:, q0 + D_ATTN:q0 + D_ATTN + D_KV]
    v = p[:, q0 + D_ATTN + D_KV:q0 + D_ATTN + 2 * D_KV]
    if rope:
        k = _rope_chunk(k, cos, sin)
    kv_ref[:, 0:LANES] = k.astype(BF16)
    kv_ref[:, LANES:2 * LANES] = pltpu.roll(k, HEAD_DIM, axis=1).astype(BF16)
    kv_ref[:, 2 * LANES:3 * LANES] = v.astype(BF16)
    kv_ref[:, 3 * LANES:4 * LANES] = pltpu.roll(v, HEAD_DIM, axis=1).astype(BF16)


def in_proj(x, mod, g_mix, w_in, cos, sin, *, row, tm):
    n = x.shape[0]
    rope = cos is not None
    tile = lambda w: pl.BlockSpec((tm, w), lambda i: (i, 0))
    in_specs = [
        tile(D_MODEL),
        pl.BlockSpec((1, SUBLANES, 6 * D_MODEL), lambda i: (0, 0, 0)),
        pl.BlockSpec((1, D_MODEL), lambda i: (0, 0)),
        pl.BlockSpec((D_MODEL, D_IN), lambda i: (0, 0)),
    ]
    args = [x, mod, g_mix, w_in]
    if rope:
        in_specs += [tile(LANES), tile(LANES)]
        args += [cos, sin]
    return pl.pallas_call(
        functools.partial(_in_proj_kernel, row=row, rope=rope),
        grid=(n // tm,),
        in_specs=in_specs,
        out_specs=[tile(D_RNN), tile(D_RNN), tile(D_ATTN), tile(4 * LANES)],
        out_shape=[
            jax.ShapeDtypeStruct((n, D_RNN), F32),
            jax.ShapeDtypeStruct((n, D_RNN), F32),
            jax.ShapeDtypeStruct((n, D_ATTN), BF16),
            jax.ShapeDtypeStruct((n, 4 * LANES), BF16),
        ],
        compiler_params=_params("arbitrary"),
        name="in_proj",
    )(*args)


def _rnn_kernel(pf_ref, mf_ref, nf_ref, pb_ref, mb_ref, nb_ref, cw_ref, cb_ref, wg_ref, bg_ref,
                lam_ref, h0_ref, hf_ref, hb_ref, hl_ref, ext_ref, a_ref, b_ref, cf_ref, cbk_ref,
                *, tm, nt):
    i = pl.program_id(0)
    ngroups = tm // SUBLANES

    @pl.when(i == 0)
    def _():
        cf_ref[...] = jnp.broadcast_to(h0_ref[0:1, :], (SUBLANES, D_RNN))
        cbk_ref[...] = jnp.broadcast_to(h0_ref[1:2, :], (SUBLANES, D_RNN))

    softplus_neg_lam = jax.nn.softplus(-lam_ref[...])
    sub = lax.broadcasted_iota(jnp.int32, (tm, RNN_CHUNK), 0) & (SUBLANES - 1)

    def run(d, prev_ref, main_ref, next_ref, at_start, at_end, out_ref, carry_ref):
        ext_ref[0:SUBLANES, :] = jnp.where(at_start, 0.0, prev_ref[...])
        ext_ref[SUBLANES:tm + SUBLANES, :] = main_ref[...]
        ext_ref[tm + SUBLANES:tm + 2 * SUBLANES, :] = jnp.where(at_end, 0.0, next_ref[...])
        u = cb_ref[...]
        for k in range(CONV_W):
            start = SUBLANES - CONV_W // 2 + k
            u = u + ext_ref[start:start + tm, :] * cw_ref[k:k + 1, :]
        ub = u.astype(BF16)
        for c in range(D_RNN // RNN_CHUNK):
            ch = slice(c * RNN_CHUNK, (c + 1) * RNN_CHUNK)
            g = jnp.dot(ub[:, ch], wg_ref[d, c], preferred_element_type=F32)
            r = jax.nn.sigmoid(g[:, :RNN_CHUNK] + bg_ref[d, 0:1, ch])
            ig = jax.nn.sigmoid(g[:, RNN_CHUNK:] + bg_ref[d, 1:2, ch])
            log_a = (-LRU_C * r) * softplus_neg_lam[d:d + 1, ch]
            a = jnp.exp(log_a)
            mult = jnp.sqrt(jnp.maximum(1.0 - a * a, 0.0))
            b = mult * (ig * u[:, ch])
            for dd in (1, 2, 4):
                if d == 0:
                    a_sh, b_sh, ok = pltpu.roll(a, dd, axis=0), pltpu.roll(b, dd, axis=0), sub >= dd
                else:
                    a_sh, b_sh = pltpu.roll(a, tm - dd, axis=0), pltpu.roll(b, tm - dd, axis=0)
                    ok = sub < SUBLANES - dd
                b = a * jnp.where(ok, b_sh, 0.0) + b
                a = a * jnp.where(ok, a_sh, 1.0)
            a_ref[:, ch] = a
            b_ref[:, ch] = b

        def body(g, carry):
            gg = g if d == 0 else ngroups - 1 - g
            off = pl.multiple_of(gg * SUBLANES, SUBLANES)
            h = b_ref[pl.ds(off, SUBLANES), :] + a_ref[pl.ds(off, SUBLANES), :] * carry
            out_ref[pl.ds(off, SUBLANES), :] = h
            edge = h[SUBLANES - 1:SUBLANES, :] if d == 0 else h[0:1, :]
            return jnp.broadcast_to(edge, (SUBLANES, D_RNN))

        carry_ref[...] = lax.fori_loop(0, ngroups, body, carry_ref[...], unroll=8)

    run(0, pf_ref, mf_ref, nf_ref, i == 0, i == nt - 1, hf_ref, cf_ref)
    run(1, pb_ref, mb_ref, nb_ref, i == nt - 1, i == 0, hb_ref, cbk_ref)

    @pl.when(i == nt - 1)
    def _():
        rows = lax.broadcasted_iota(jnp.int32, (SUBLANES, D_RNN), 0)
        hl_ref[...] = jnp.where(rows == 0, cf_ref[...], jnp.where(rows == 1, cbk_ref[...], 0.0))


def rnn_bidir(xr, conv_w, conv_b, wg, bg, lam, h0, *, tm):
    n = xr.shape[0]
    nt = n // tm
    per = tm // SUBLANES
    last8 = n // SUBLANES - 1
    fwd = lambda i: i
    bwd = lambda i: nt - 1 - i
    main = lambda t: pl.BlockSpec((tm, D_RNN), lambda i: (t(i), 0))
    prev = lambda t: pl.BlockSpec((SUBLANES, D_RNN), lambda i: (jnp.maximum(t(i) * per - 1, 0), 0))
    nxt = lambda t: pl.BlockSpec((SUBLANES, D_RNN), lambda i: (jnp.minimum((t(i) + 1) * per, last8), 0))
    whole = lambda a: pl.BlockSpec(a.shape, lambda i: (0,) * a.ndim)
    return pl.pallas_call(
        functools.partial(_rnn_kernel, tm=tm, nt=nt),
        grid=(nt,),
        in_specs=[prev(fwd), main(fwd), nxt(fwd), prev(bwd), main(bwd), nxt(bwd),
                  whole(conv_w), whole(conv_b), whole(wg), whole(bg), whole(lam), whole(h0)],
        out_specs=[main(fwd), main(bwd), pl.BlockSpec((SUBLANES, D_RNN), lambda i: (0, 0))],
        out_shape=[
            jax.ShapeDtypeStruct((n, D_RNN), F32),
            jax.ShapeDtypeStruct((n, D_RNN), F32),
            jax.ShapeDtypeStruct((SUBLANES, D_RNN), F32),
        ],
        scratch_shapes=[
            pltpu.VMEM((tm + 2 * SUBLANES, D_RNN), F32),
            pltpu.VMEM((tm, D_RNN), F32),
            pltpu.VMEM((tm, D_RNN), F32),
            pltpu.VMEM((SUBLANES, D_RNN), F32),
            pltpu.VMEM((SUBLANES, D_RNN), F32),
        ],
        compiler_params=_params("arbitrary"),
        name="rnn_bidir",
    )(xr, xr, xr, xr, xr, xr, conv_w, conv_b, wg, bg, lam, h0)


def _attn_kernel(*refs, nb, band, n_ctx):
    if band:
        sink_ref, q_ref, kvx_ref, kvp_ref, kvc_ref, kvn_ref, o_ref = refs
        kv = jnp.concatenate([kvx_ref[...], kvp_ref[...], kvc_ref[...], kvn_ref[...]], axis=0)
    else:
        sink_ref, q_ref, kvx_ref, o_ref = refs
        kv = kvx_ref[...]
    i = pl.program_id(0)
    nk = kv.shape[0]
    lane = lax.broadcasted_iota(jnp.int32, (nk, LANES), 1)
    low = lane < HEAD_DIM
    zero = jnp.zeros((), BF16)
    k, ks, v, vs = (kv[:, j * LANES:(j + 1) * LANES] for j in range(4))
    k_low = (jnp.where(low, k, zero), jnp.where(low, ks, zero))
    k_high = (jnp.where(low, zero, ks), jnp.where(low, zero, k))
    v_low = (jnp.where(low, v, zero), jnp.where(low, vs, zero))
    v_high = (jnp.where(low, zero, vs), jnp.where(low, zero, v))

    rows2 = 2 * BLOCK_Q
    r = lax.broadcasted_iota(jnp.int32, (rows2, nk), 0) & (BLOCK_Q - 1)
    if band:
        jb = lax.broadcasted_iota(jnp.int32, (rows2, nk), 1) - n_ctx
        valid = (jb < 0) | ((jb >= r) & (jb <= r + 2 * WINDOW)
                            & ((jb >= BLOCK_Q) | (i > 0)) & ((jb < 2 * BLOCK_Q) | (i < nb - 1)))
    upper = lax.broadcasted_iota(jnp.int32, (rows2, 1), 0) >= BLOCK_Q

    def probs(q2, kmat, sink_a, sink_b):
        s = lax.dot_general(q2, kmat, (((1,), (1,)), ((), ())), preferred_element_type=F32)
        if band:
            s = jnp.where(valid, s, NEG_INF)
        sink = jnp.where(upper, sink_b, sink_a)
        m = jnp.maximum(jnp.max(s, axis=-1, keepdims=True), sink)
        p = jnp.exp(s - m)
        den = jnp.sum(p, axis=-1, keepdims=True) + jnp.exp(sink - m)
        return (p / den).astype(BF16)

    for hk in range(N_KV_HEADS):
        c0, c1 = 2 * hk, 2 * hk + 1
        q2 = jnp.concatenate([q_ref[:, c0 * LANES:(c0 + 1) * LANES],
                              q_ref[:, c1 * LANES:(c1 + 1) * LANES]], axis=0)
        h0 = hk * HEADS_PER_KV
        p_low = probs(q2, k_low[hk], sink_ref[h0], sink_ref[h0 + 2])
        p_high = probs(q2, k_high[hk], sink_ref[h0 + 1], sink_ref[h0 + 3])
        o2 = (jnp.dot(p_low, v_low[hk], preferred_element_type=F32)
              + jnp.dot(p_high, v_high[hk], preferred_element_type=F32))
        o_ref[:, c0 * LANES:(c0 + 1) * LANES] = o2[:BLOCK_Q]
        o_ref[:, c1 * LANES:(c1 + 1) * LANES] = o2[BLOCK_Q:]


def attention(q, kv, kv_ctx, sink, *, band):
    n = q.shape[0]
    nb = n // BLOCK_Q
    n_ctx = kv_ctx.shape[0]
    in_specs = [
        pl.BlockSpec(memory_space=pltpu.SMEM),
        pl.BlockSpec((BLOCK_Q, D_ATTN), lambda i: (i, 0)),
        pl.BlockSpec((n_ctx, 4 * LANES), lambda i: (0, 0)),
    ]
    args = [sink, q, kv_ctx]
    if band:
        blk = lambda f: pl.BlockSpec((BLOCK_Q, 4 * LANES), lambda i: (f(i), 0))
        in_specs += [blk(lambda i: jnp.maximum(i - 1, 0)), blk(lambda i: i),
                     blk(lambda i: jnp.minimum(i + 1, nb - 1))]
        args += [kv, kv, kv]
    return pl.pallas_call(
        functools.partial(_attn_kernel, nb=nb, band=band, n_ctx=n_ctx),
        grid=(nb,),
        in_specs=in_specs,
        out_specs=pl.BlockSpec((BLOCK_Q, D_ATTN), lambda i: (i, 0)),
        out_shape=jax.ShapeDtypeStruct((n, D_ATTN), F32),
        compiler_params=_params("arbitrary"),
        name="attention",
    )(*args)


def _merge_kernel(*refs, row, moe):
    if moe == "comb":
        (x_ref, hf_ref, hb_ref, gr_ref, o_ref, mod_ref, gg_ref, wo_ref, gf_ref, wrh_ref, wrl_ref,
         xo_ref, h2_ref, comb_ref) = refs
    elif moe == "route":
        (x_ref, hf_ref, hb_ref, gr_ref, o_ref, mod_ref, gg_ref, wo_ref, gf_ref, wrh_ref, wrl_ref,
         xo_ref, h2_ref, mi_ref, mf_ref, cnt_ref, run_ref) = refs
    else:
        x_ref, hf_ref, hb_ref, gr_ref, o_ref, mod_ref, gg_ref, wo_ref, gf_ref, xo_ref, h2_ref = refs
    mod = lambda off: mod_ref[0, row:row + 1, off:off + D_MODEL]
    y_rnn = (hf_ref[...] + hb_ref[...]) * jax.nn.gelu(gr_ref[...])
    y = jnp.concatenate([_rms(y_rnn) * gg_ref[:, :D_RNN], _rms(o_ref[...]) * gg_ref[:, D_RNN:]], axis=1)
    out = jnp.dot(y.astype(BF16), wo_ref[...], preferred_element_type=F32)
    x = x_ref[...] + mod(GT_M) * out
    xo_ref[...] = x
    h2 = (_rms(x) * gf_ref[...]) * (1.0 + mod(SC_F)) + mod(SH_F)
    h2_ref[...] = h2.astype(h2_ref.dtype)
    if moe:
        hi = h2.astype(BF16)
        lo = (h2 - hi.astype(F32)).astype(BF16)
        dot = lambda a, b: jnp.dot(a, b, preferred_element_type=F32)
        logits = dot(hi, wrh_ref[...]) + (dot(hi, wrl_ref[...]) + dot(lo, wrh_ref[...]))
        lane = lax.broadcasted_iota(jnp.int32, logits.shape, 1)
        logits = jnp.where(lane < N_EXPERTS, logits, -jnp.inf)
        v1 = jnp.max(logits, axis=-1, keepdims=True)
        i1 = jnp.min(jnp.where(logits == v1, lane, LANES), axis=-1, keepdims=True)
        rest = jnp.where(lane == i1, -jnp.inf, logits)
        v2 = jnp.max(rest, axis=-1, keepdims=True)
        i2 = jnp.min(jnp.where(rest == v2, lane, LANES), axis=-1, keepdims=True)
        e2 = jnp.exp(v2 - v1)
        w1 = 1.0 / (1.0 + e2)
        w2 = e2 / (1.0 + e2)
        if moe == "comb":
            comb_ref[...] = jnp.where(lane == i1, w1, 0.0) + jnp.where(lane == i2, w2, 0.0)
        else:
            tm = logits.shape[0]

            @pl.when(pl.program_id(0) == 0)
            def _():
                run_ref[...] = jnp.zeros_like(run_ref)

            chosen = (lane == i1) | (lane == i2)
            before = (lax.broadcasted_iota(jnp.int32, (tm, tm), 0)
                      > lax.broadcasted_iota(jnp.int32, (tm, tm), 1))
            prefix = dot(jnp.where(before, 1.0, 0.0).astype(BF16), jnp.where(chosen, 1.0, 0.0).astype(BF16))
            rank = prefix + run_ref[...]
            r1 = jnp.sum(jnp.where(lane == i1, rank, 0.0), axis=-1, keepdims=True).astype(jnp.int32)
            r2 = jnp.sum(jnp.where(lane == i2, rank, 0.0), axis=-1, keepdims=True).astype(jnp.int32)
            run_ref[...] += jnp.sum(jnp.where(chosen, 1.0, 0.0), axis=0, keepdims=True)
            mi_ref[...] = jnp.where(lane == 0, i1, jnp.where(lane == 1, i2,
                                    jnp.where(lane == 2, r1, jnp.where(lane == 3, r2, 0))))
            mf_ref[...] = jnp.where(lane == 0, w1, jnp.where(lane == 1, w2, 0.0))
            cnt_ref[...] = jnp.broadcast_to(run_ref[...], cnt_ref.shape).astype(jnp.int32)


def merge(x, hf, hb, gr, o, mod, g_grp, w_out, g_ffn, router, *, row, tm, moe):
    n = x.shape[0]
    tile = lambda w: pl.BlockSpec((tm, w), lambda i: (i, 0))
    whole = lambda a: pl.BlockSpec(a.shape, lambda i: (0,) * a.ndim)
    in_specs = [tile(D_MODEL), tile(D_RNN), tile(D_RNN), tile(D_RNN), tile(D_ATTN),
                whole(mod), whole(g_grp), whole(w_out), whole(g_ffn)]
    args = [x, hf, hb, gr, o, mod, g_grp, w_out, g_ffn]
    out_specs = [tile(D_MODEL), tile(D_MODEL)]
    h2_dtype = F32 if moe == "route" else BF16
    out_shape = [jax.ShapeDtypeStruct((n, D_MODEL), F32), jax.ShapeDtypeStruct((n, D_MODEL), h2_dtype)]
    scratch = []
    if moe:
        in_specs += [whole(router[0]), whole(router[1])]
        args += list(router)
    if moe == "comb":
        out_specs.append(tile(LANES))
        out_shape.append(jax.ShapeDtypeStruct((n, LANES), F32))
    elif moe == "route":
        out_specs += [tile(LANES), tile(LANES), pl.BlockSpec((SUBLANES, LANES), lambda i: (0, 0))]
        out_shape += [jax.ShapeDtypeStruct((n, LANES), jnp.int32), jax.ShapeDtypeStruct((n, LANES), F32),
                      jax.ShapeDtypeStruct((SUBLANES, LANES), jnp.int32)]
        scratch = [pltpu.VMEM((1, LANES), F32)]
    return pl.pallas_call(
        functools.partial(_merge_kernel, row=row, moe=moe),
        grid=(n // tm,),
        in_specs=in_specs,
        out_specs=out_specs,
        out_shape=out_shape,
        scratch_shapes=scratch,
        compiler_params=_params("arbitrary"),
        name="merge",
    )(*args)


def _ffn_kernel(*refs, row, n_exp, nf, moe, final):
    refs = list(refs)
    h_ref, x_ref, mod_ref = refs[:3]
    pos = 3
    comb_ref = None
    if moe:
        comb_ref = refs[pos]
        pos += 1
    gfin_ref = None
    if final:
        gfin_ref = refs[pos]
        pos += 1
    wg_ref, wu_ref, wd_ref, o_ref, acc_ref = refs[pos:pos + 5]
    tot_ref = refs[pos + 5] if moe else acc_ref
    e = pl.program_id(1)
    f = pl.program_id(2)

    @pl.when(f == 0)
    def _():
        acc_ref[...] = jnp.zeros_like(acc_ref)

    if moe:
        @pl.when((e == 0) & (f == 0))
        def _():
            tot_ref[...] = jnp.zeros_like(tot_ref)

    h = h_ref[...]
    a = jnp.dot(h, wg_ref[0], preferred_element_type=F32)
    b = jnp.dot(h, wu_ref[0], preferred_element_type=F32)
    act = (a * jax.nn.sigmoid(a)) * b
    acc_ref[...] += jnp.dot(act.astype(BF16), wd_ref[0], preferred_element_type=F32)

    if moe:
        @pl.when(f == nf - 1)
        def _():
            comb = comb_ref[...]
            lane = lax.broadcasted_iota(jnp.int32, comb.shape, 1)
            ce = jnp.sum(jnp.where(lane == e, comb, 0.0), axis=-1, keepdims=True)
            tot_ref[...] += ce * acc_ref[...]

    @pl.when((e == n_exp - 1) & (f == nf - 1))
    def _():
        x = x_ref[...] + mod_ref[0, row:row + 1, GT_F:GT_F + D_MODEL] * tot_ref[...]
        if final:
            x = _rms(x) * gfin_ref[...]
        o_ref[...] = x


def ffn(h2, x, mod, comb, g_final, wg, wu, wd, *, row, tm, tf):
    n = x.shape[0]
    n_exp, _, d_ff = wg.shape
    nf = d_ff // tf
    moe = comb is not None
    final = g_final is not None
    tile = lambda w: pl.BlockSpec((tm, w), lambda t, e, f: (t, 0))
    in_specs = [tile(D_MODEL), tile(D_MODEL), pl.BlockSpec(mod.shape, lambda t, e, f: (0, 0, 0))]
    args = [h2, x, mod]
    if moe:
        in_specs.append(tile(LANES))
        args.append(comb)
    if final:
        in_specs.append(pl.BlockSpec((1, D_MODEL), lambda t, e, f: (0, 0)))
        args.append(g_final)
    in_specs += [
        pl.BlockSpec((1, D_MODEL, tf), lambda t, e, f: (e, 0, f)),
        pl.BlockSpec((1, D_MODEL, tf), lambda t, e, f: (e, 0, f)),
        pl.BlockSpec((1, tf, D_MODEL), lambda t, e, f: (e, f, 0)),
    ]
    args += [wg, wu, wd]
    scratch = [pltpu.VMEM((tm, D_MODEL), F32)]
    if moe:
        scratch.append(pltpu.VMEM((tm, D_MODEL), F32))
    return pl.pallas_call(
        functools.partial(_ffn_kernel, row=row, n_exp=n_exp, nf=nf, moe=moe, final=final),
        grid=(n // tm, n_exp, nf),
        in_specs=in_specs,
        out_specs=tile(D_MODEL),
        out_shape=jax.ShapeDtypeStruct((n, D_MODEL), F32),
        scratch_shapes=scratch,
        compiler_params=_params("arbitrary", "arbitrary", "arbitrary"),
        name="ffn",
    )(*args)


MOE_TILE = 512


def _route_tables(cnt, meta_i, n_pad):
    i32 = jnp.int32
    counts = cnt[0, :N_EXPERTS]
    tiles = (counts + MOE_TILE - 1) // MOE_TILE
    cum = jnp.cumsum(tiles)
    starts = (cum - tiles) * MOE_TILE
    experts = jnp.arange(N_EXPERTS, dtype=i32)
    start_of = lambda e: jnp.sum(jnp.where(e[:, None] == experts[None, :], starts[None, :], 0), axis=1)
    pos1 = (meta_i[:, 2] + start_of(meta_i[:, 0])).astype(i32)
    pos2 = (meta_i[:, 3] + start_of(meta_i[:, 1])).astype(i32)
    n_tiles = cum[-1]
    t = jnp.arange(n_pad // MOE_TILE, dtype=i32)
    te = jnp.sum((t[:, None] >= cum[None, :]).astype(i32), axis=1)
    used = t < n_tiles
    tile_expert = jnp.where(used, te, jnp.max(jnp.where(used, te, 0))).astype(i32)
    zero_start = jnp.concatenate([starts + counts, (n_tiles * MOE_TILE)[None]]).astype(i32)
    zero_len = jnp.concatenate([tiles * MOE_TILE - counts, (n_pad - n_tiles * MOE_TILE)[None]]).astype(i32)
    return pos1, pos2, tile_expert, n_tiles.reshape(1).astype(i32), zero_start, zero_len


def _dispatch_kernel(pos1_ref, pos2_ref, zs_ref, zl_ref, h_ref, x_hbm, zero_ref, sem, *, tm):
    i = pl.program_id(0)

    def row_copy(src, r, p, s):
        return pltpu.make_async_copy(src.at[pl.ds(r, 1)], x_hbm.at[pl.ds(p, 1)], sem.at[s])

    @pl.when(i == 0)
    def _():
        zero_ref[...] = jnp.zeros_like(zero_ref)
        for k in range(N_EXPERTS + 1):
            def issue(j, c):
                row_copy(zero_ref, 0, zs_ref[k] + j, 2).start()
                return c
            lax.fori_loop(0, zl_ref[k], issue, 0)
        for k in range(N_EXPERTS + 1):
            def drain(j, c):
                row_copy(zero_ref, 0, zs_ref[k] + j, 2).wait()
                return c
            lax.fori_loop(0, zl_ref[k], drain, 0)

    base = i * tm

    def issue(r, c):
        row_copy(h_ref, r, pos1_ref[base + r], 0).start()
        row_copy(h_ref, r, pos2_ref[base + r], 1).start(priority=1)
        return c

    lax.fori_loop(0, tm, issue, 0, unroll=8)

    def drain(r, c):
        row_copy(h_ref, r, pos1_ref[base + r], 0).wait()
        row_copy(h_ref, r, pos2_ref[base + r], 1).wait()
        return c

    lax.fori_loop(0, tm, drain, 0, unroll=8)


def moe_dispatch(pos1, pos2, zero_start, zero_len, h2, n_pad, *, tm):
    n = h2.shape[0]
    return pl.pallas_call(
        functools.partial(_dispatch_kernel, tm=tm),
        grid_spec=pltpu.PrefetchScalarGridSpec(
            num_scalar_prefetch=4,
            grid=(n // tm,),
            in_specs=[pl.BlockSpec((tm, D_MODEL), lambda i, *_: (i, 0))],
            out_specs=pl.BlockSpec(memory_space=pl.ANY),
            scratch_shapes=[pltpu.VMEM((SUBLANES, D_MODEL), F32), pltpu.SemaphoreType.DMA((3,))],
        ),
        out_shape=jax.ShapeDtypeStruct((n_pad, D_MODEL), F32),
        compiler_params=_params("arbitrary"),
        name="moe_dispatch",
    )(pos1, pos2, zero_start, zero_len, h2)


def _grouped_ffn_kernel(te_ref, nt_ref, x_ref, wg_ref, wu_ref, wd_ref, y_ref):
    t = pl.program_id(0)

    @pl.when(t < nt_ref[0])
    def _():
        x = x_ref[...].astype(BF16)
        a = jnp.dot(x, wg_ref[0], preferred_element_type=F32)
        b = jnp.dot(x, wu_ref[0], preferred_element_type=F32)
        act = (a * jax.nn.sigmoid(a)) * b
        y_ref[...] = jnp.dot(act.astype(BF16), wd_ref[0], preferred_element_type=F32)

    @pl.when(t >= nt_ref[0])
    def _():
        y_ref[...] = jnp.zeros_like(y_ref)


def grouped_ffn(tile_expert, n_tiles, xs, wg, wu, wd):
    n_pad = xs.shape[0]
    d_ff = wg.shape[2]
    rows = pl.BlockSpec((MOE_TILE, D_MODEL), lambda t, te, nt: (t, 0))
    return pl.pallas_call(
        _grouped_ffn_kernel,
        grid_spec=pltpu.PrefetchScalarGridSpec(
            num_scalar_prefetch=2,
            grid=(n_pad // MOE_TILE,),
            in_specs=[rows,
                      pl.BlockSpec((1, D_MODEL, d_ff), lambda t, te, nt: (te[t], 0, 0)),
                      pl.BlockSpec((1, D_MODEL, d_ff), lambda t, te, nt: (te[t], 0, 0)),
                      pl.BlockSpec((1, d_ff, D_MODEL), lambda t, te, nt: (te[t], 0, 0))],
            out_specs=rows,
        ),
        out_shape=jax.ShapeDtypeStruct((n_pad, D_MODEL), F32),
        compiler_params=_params("arbitrary"),
        name="grouped_ffn",
    )(tile_expert, n_tiles, xs, wg, wu, wd)


def _combine_kernel(*refs, row, tm, final):
    if final:
        pos1_ref, pos2_ref, x_ref, mf_ref, mod_ref, gfin_ref, y_hbm, o_ref, b1_ref, b2_ref, sem = refs
    else:
        pos1_ref, pos2_ref, x_ref, mf_ref, mod_ref, y_hbm, o_ref, b1_ref, b2_ref, sem = refs
    base = pl.program_id(0) * tm

    def row_copy(p, dst, r, s):
        return pltpu.make_async_copy(y_hbm.at[pl.ds(p, 1)], dst.at[pl.ds(r, 1)], sem.at[s])

    def issue(r, c):
        row_copy(pos1_ref[base + r], b1_ref, r, 0).start()
        row_copy(pos2_ref[base + r], b2_ref, r, 1).start(priority=1)
        return c

    lax.fori_loop(0, tm, issue, 0, unroll=8)

    def drain(r, c):
        row_copy(pos1_ref[base + r], b1_ref, r, 0).wait()
        row_copy(pos2_ref[base + r], b2_ref, r, 1).wait()
        return c

    lax.fori_loop(0, tm, drain, 0, unroll=8)

    tot = mf_ref[:, 0:1] * b1_ref[...] + mf_ref[:, 1:2] * b2_ref[...]
    x = x_ref[...] + mod_ref[0, row:row + 1, GT_F:GT_F + D_MODEL] * tot
    if final:
        x = _rms(x) * gfin_ref[...]
    o_ref[...] = x


def moe_combine(pos1, pos2, x, meta_f, mod, g_final, ys, *, row, tm):
    n = x.shape[0]
    final = g_final is not None
    tile = lambda w: pl.BlockSpec((tm, w), lambda i, *_: (i, 0))
    in_specs = [tile(D_MODEL), tile(LANES), pl.BlockSpec(mod.shape, lambda i, *_: (0, 0, 0))]
    args = [x, meta_f, mod]
    if final:
        in_specs.append(pl.BlockSpec((1, D_MODEL), lambda i, *_: (0, 0)))
        args.append(g_final)
    in_specs.append(pl.BlockSpec(memory_space=pl.ANY))
    args.append(ys)
    return pl.pallas_call(
        functools.partial(_combine_kernel, row=row, tm=tm, final=final),
        grid_spec=pltpu.PrefetchScalarGridSpec(
            num_scalar_prefetch=2,
            grid=(n // tm,),
            in_specs=in_specs,
            out_specs=tile(D_MODEL),
            scratch_shapes=[pltpu.VMEM((tm, D_MODEL), F32), pltpu.VMEM((tm, D_MODEL), F32),
                            pltpu.SemaphoreType.DMA((2,))],
        ),
        out_shape=jax.ShapeDtypeStruct((n, D_MODEL), F32),
        compiler_params=_params("arbitrary"),
        name="moe_combine",
    )(pos1, pos2, *args)


def _gate_weights(w_rg):
    depth = w_rg.shape[0]
    per = RNN_CHUNK // RNN_BLOCK
    wr = w_rg.reshape(depth, 2, 2, D_RNN // RNN_CHUNK, per, RNN_BLOCK, RNN_BLOCK)
    bd = jnp.zeros((depth, 2, 2, D_RNN // RNN_CHUNK, per, RNN_BLOCK, per, RNN_BLOCK), w_rg.dtype)
    for g in range(per):
        bd = bd.at[:, :, :, :, g, :, g, :].set(wr[:, :, :, :, g])
    bd = bd.reshape(depth, 2, 2, D_RNN // RNN_CHUNK, RNN_CHUNK, RNN_CHUNK)
    return jnp.concatenate([bd[:, :, 0], bd[:, :, 1]], axis=-1).astype(BF16)


def _tile_rows(n, want):
    return want if n % want == 0 else n


def kernel(x, c, ctx, c_ctx, w_mod, b_mod, g_mix, g_ffn, w_in, conv_w, conv_b, w_rg, b_rg, lam,
           sink, g_grp, w_out, w_ffn_gate, w_ffn_up, w_ffn_down, w_router, w_exp_gate, w_exp_up,
           w_exp_down, g_final):
    assert x.shape[0] == 1 and ctx.shape[0] == 1
    depth = w_mod.shape[0]
    seq, n_ctx = x.shape[1], ctx.shape[1]
    xl, xc = x[0], ctx[0]

    cond8 = jnp.zeros((SUBLANES, D_MODEL), F32).at[0].set(c[0]).at[1].set(c_ctx)
    mods = adaln_all(cond8, w_mod, b_mod)
    cos, sin = rope_tables(seq)

    w_in_b = w_in.astype(BF16)
    w_out_b = w_out.astype(BF16)
    wg_gate = _gate_weights(w_rg)
    ffn_w = (w_ffn_gate.astype(BF16), w_ffn_up.astype(BF16), w_ffn_down.astype(BF16))
    exp_w = (w_exp_gate.astype(BF16), w_exp_up.astype(BF16), w_exp_down.astype(BF16))
    wr_pad = jnp.zeros((w_router.shape[0], D_MODEL, LANES), F32).at[:, :, :N_EXPERTS].set(w_router)
    wr_hi = wr_pad.astype(BF16)
    wr_lo = (wr_pad - wr_hi.astype(F32)).astype(BF16)
    zeros_h0 = jnp.zeros((SUBLANES, D_RNN), F32)

    tm = _tile_rows(seq, 512)
    tm_ffn = _tile_rows(seq, 1024)

    for l in range(depth):
        last = l == depth - 1
        moe = l % 2 == 1
        j = l // 2
        mod = mods[l:l + 1]
        gm, gf, gg = g_mix[l:l + 1], g_ffn[l:l + 1], g_grp[l:l + 1]

        xr_c, gr_c, q_c, kv_c = in_proj(xc, mod, gm, w_in_b[l], None, None, row=1, tm=n_ctx)
        xr_l, gr_l, q_l, kv_l = in_proj(xl, mod, gm, w_in_b[l], cos, sin, row=0, tm=tm)

        rnn_args = (conv_w[l], conv_b[l:l + 1], wg_gate[l], b_rg[l], lam[l])
        hf_c, hb_c, h_ctx_end = rnn_bidir(xr_c, *rnn_args, zeros_h0, tm=n_ctx)
        hf_l, hb_l, _ = rnn_bidir(xr_l, *rnn_args, h_ctx_end, tm=tm)

        o_l = attention(q_l, kv_l, kv_c, sink[l], band=True)

        router = (wr_hi[j], wr_lo[j]) if moe else None
        weights = tuple(w[j] for w in exp_w) if moe else tuple(w[j:j + 1] for w in ffn_w)
        tf = 512

        g_fin = g_final.reshape(1, D_MODEL) if last else None
        if moe:
            xl, h2, meta_i, meta_f, cnt = merge(xl, hf_l, hb_l, gr_l, o_l, mod, gg, w_out_b[l], gf, router,
                                                row=0, tm=tm, moe="route")
            n_pad = 2 * seq + N_EXPERTS * MOE_TILE
            pos1, pos2, tile_expert, n_tiles, zero_start, zero_len = _route_tables(cnt, meta_i, n_pad)
            xs = moe_dispatch(pos1, pos2, zero_start, zero_len, h2, n_pad, tm=tm)
            ys = grouped_ffn(tile_expert, n_tiles, xs, *weights)
            xl = moe_combine(pos1, pos2, xl, meta_f, mod, g_fin, ys, row=0, tm=tm)
        else:
            xl, h2 = merge(xl, hf_l, hb_l, gr_l, o_l, mod, gg, w_out_b[l], gf, None, row=0, tm=tm, moe=None)
            xl = ffn(h2, xl, mod, None, g_fin, *weights, row=0, tm=tm_ffn, tf=tf)

        if not last:
            o_c = attention(q_c, kv_c, kv_c, sink[l], band=False)
            res = merge(xc, hf_c, hb_c, gr_c, o_c, mod, gg, w_out_b[l], gf, router, row=1, tm=n_ctx,
                        moe="comb" if moe else None)
            xc, h2c = res[0], res[1]
            comb_c = res[2] if moe else None
            xc = ffn(h2c, xc, mod, comb_c, None, *weights, row=1, tm=n_ctx, tf=tf)

    return xl[None]
```

```python
import functools

import jax
import jax.numpy as jnp
from jax import lax
from jax.experimental import pallas as pl
from jax.experimental.pallas import tpu as pltpu

F32 = jnp.float32
BF16 = jnp.bfloat16

D_MODEL = 1024
D_RNN = 512
D_ATTN = 512
D_KV = 128
HEAD_DIM = 64
N_HEADS = 8
N_KV_HEADS = 2
HEADS_PER_KV = N_HEADS // N_KV_HEADS
D_IN = 2 * D_RNN + D_ATTN + 2 * D_KV
RNN_BLOCK = 64
RNN_CHUNK = 256
CONV_W = 4
LRU_C = 8.0
WINDOW = 128
BLOCK_Q = 128
GRID_W = 64
ROPE_BASE = 10000.0
N_EXPERTS = 8
EPS = 1e-6
NEG_INF = -1e30

LANES = 128
SUBLANES = 8
VMEM_LIMIT = 56 * 1024 * 1024

SH_M, SC_M, GT_M, SH_F, SC_F, GT_F = (k * D_MODEL for k in range(6))


def _params(*sem):
    return pltpu.CompilerParams(dimension_semantics=sem, vmem_limit_bytes=VMEM_LIMIT)


def _rms(x):
    return x * lax.rsqrt(jnp.mean(x * x, axis=-1, keepdims=True) + EPS)


def _adaln_kernel(c_ref, w_ref, b_ref, o_ref):
    c = c_ref[...]
    s = c * jax.nn.sigmoid(c)
    w = w_ref[0]
    s_hi, w_hi = s.astype(BF16), w.astype(BF16)
    s_lo = (s - s_hi.astype(F32)).astype(BF16)
    w_lo = (w - w_hi.astype(F32)).astype(BF16)
    dot = lambda a, b: jnp.dot(a, b, preferred_element_type=F32)
    o_ref[0] = dot(s_hi, w_hi) + (dot(s_hi, w_lo) + dot(s_lo, w_hi)) + b_ref[0]


def adaln_all(cond8, w_mod, b_mod):
    depth = w_mod.shape[0]
    nc = 1536
    return pl.pallas_call(
        _adaln_kernel,
        grid=(depth, 6 * D_MODEL // nc),
        in_specs=[
            pl.BlockSpec((SUBLANES, D_MODEL), lambda l, j: (0, 0)),
            pl.BlockSpec((1, D_MODEL, nc), lambda l, j: (l, 0, j)),
            pl.BlockSpec((1, 1, nc), lambda l, j: (l, 0, j)),
        ],
        out_specs=pl.BlockSpec((1, SUBLANES, nc), lambda l, j: (l, 0, j)),
        out_shape=jax.ShapeDtypeStruct((depth, SUBLANES, 6 * D_MODEL), F32),
        compiler_params=_params("arbitrary", "arbitrary"),
        name="adaln",
    )(cond8, w_mod, b_mod.reshape(depth, 1, 6 * D_MODEL))


ROPE_ROWS = SUBLANES * GRID_W


def _rope_kernel(f_ref, c_ref, s_ref):
    i = pl.program_id(0)
    lane = lax.broadcasted_iota(jnp.int32, (GRID_W, LANES), 1)
    col_axis = ((lane >> 5) & 1) == 1
    sign = jnp.where(((lane >> 4) & 1) == 0, -1.0, 1.0).astype(F32)
    freq = f_ref[...]
    rowpos = (i * SUBLANES + lax.broadcasted_iota(jnp.int32, (SUBLANES, LANES), 0)).astype(F32)
    colpos = lax.broadcasted_iota(jnp.int32, (GRID_W, LANES), 0).astype(F32)
    ang_r = rowpos * freq
    ang_c = colpos * freq
    cr, sr = jnp.cos(ang_r), jnp.sin(ang_r)
    cc, sc = jnp.cos(ang_c), jnp.sin(ang_c) * sign
    for g in range(SUBLANES):
        rows = slice(g * GRID_W, (g + 1) * GRID_W)
        c_ref[rows, :] = jnp.where(col_axis, cc, jnp.broadcast_to(cr[g:g + 1, :], (GRID_W, LANES)))
        s_ref[rows, :] = jnp.where(col_axis, sc, jnp.broadcast_to(sr[g:g + 1, :], (GRID_W, LANES)) * sign)


def rope_tables(seq):
    axis_dim = HEAD_DIM // 2
    freqs = ROPE_BASE ** (-jnp.arange(0, axis_dim, 2, dtype=F32) / axis_dim)
    freq_lane = jnp.tile(freqs, LANES // freqs.shape[0]).reshape(1, LANES)
    return pl.pallas_call(
        _rope_kernel,
        grid=(seq // ROPE_ROWS,),
        in_specs=[pl.BlockSpec((1, LANES), lambda i: (0, 0))],
        out_specs=[pl.BlockSpec((ROPE_ROWS, LANES), lambda i: (i, 0))] * 2,
        out_shape=[jax.ShapeDtypeStruct((seq, LANES), F32)] * 2,
        compiler_params=_params("arbitrary"),
        name="rope_tables",
    )(freq_lane)


def _rope_chunk(x, cos, sin):
    lane = lax.broadcasted_iota(jnp.int32, x.shape, 1)
    first_half = ((lane >> 4) & 1) == 0
    partner = jnp.where(first_half, pltpu.roll(x, LANES - 16, axis=1), pltpu.roll(x, 16, axis=1))
    return x * cos + partner * sin


def _in_proj_kernel(*refs, row, rope):
    if rope:
        x_ref, mod_ref, g_ref, w_ref, cos_ref, sin_ref, xr_ref, gr_ref, q_ref, kv_ref = refs
    else:
        x_ref, mod_ref, g_ref, w_ref, xr_ref, gr_ref, q_ref, kv_ref = refs
    x = x_ref[...]
    shift = mod_ref[0, row:row + 1, SH_M:SH_M + D_MODEL]
    scale = mod_ref[0, row:row + 1, SC_M:SC_M + D_MODEL]
    h = (_rms(x) * g_ref[...]) * (1.0 + scale) + shift
    p = jnp.dot(h.astype(BF16), w_ref[0], preferred_element_type=F32)
    xr_ref[...] = p[:, 0:D_RNN]
    gr_ref[...] = p[:, D_RNN:2 * D_RNN].astype(gr_ref.dtype)
    q0 = 2 * D_RNN
    if rope:
        cos, sin = cos_ref[...], sin_ref[...]
    for c in range(D_ATTN // LANES):
        qc = p[:, q0 + c * LANES:q0 + (c + 1) * LANES]
        if rope:
            qc = _rope_chunk(qc, cos, sin)
        q_ref[:, c * LANES:(c + 1) * LANES] = (qc * (HEAD_DIM ** -0.5)).astype(BF16)
    k = p[:, q0 + D_ATTN:q0 + D_ATTN + D_KV]
    v = p[:, q0 + D_ATTN + D_KV:q0 + D_ATTN + 2 * D_KV]
    if rope:
        k = _rope_chunk(k, cos, sin)
    kv_ref[:, 0:LANES] = k.astype(BF16)
    kv_ref[:, LANES:2 * LANES] = pltpu.roll(k, HEAD_DIM, axis=1).astype(BF16)
    kv_ref[:, 2 * LANES:3 * LANES] = v.astype(BF16)
    kv_ref[:, 3 * LANES:4 * LANES] = pltpu.roll(v, HEAD_DIM, axis=1).astype(BF16)


def in_proj(x, mod, g_mix, w_in, cos, sin, *, layer, row, tm):
    n = x.shape[0]
    rope = cos is not None
    tile = lambda w: pl.BlockSpec((tm, w), lambda i: (i, 0))
    in_specs = [
        tile(D_MODEL),
        pl.BlockSpec((1, SUBLANES, 6 * D_MODEL), lambda i: (0, 0, 0)),
        pl.BlockSpec((1, D_MODEL), lambda i: (0, 0)),
        pl.BlockSpec((1, D_MODEL, D_IN), lambda i: (layer, 0, 0)),
    ]
    args = [x, mod, g_mix, w_in]
    if rope:
        in_specs += [tile(LANES), tile(LANES)]
        args += [cos, sin]
    return pl.pallas_call(
        functools.partial(_in_proj_kernel, row=row, rope=rope),
        grid=(n // tm,),
        in_specs=in_specs,
        out_specs=[tile(D_RNN), tile(D_RNN), tile(D_ATTN), tile(4 * LANES)],
        out_shape=[
            jax.ShapeDtypeStruct((n, D_RNN), F32),
            jax.ShapeDtypeStruct((n, D_RNN), BF16),
            jax.ShapeDtypeStruct((n, D_ATTN), BF16),
            jax.ShapeDtypeStruct((n, 4 * LANES), BF16),
        ],
        compiler_params=_params("arbitrary"),
        name="in_proj",
    )(*args)


def _rnn_kernel(pf_ref, mf_ref, nf_ref, pb_ref, mb_ref, nb_ref, cw_ref, cb_ref, wg_ref, bg_ref,
                lam_ref, h0_ref, hf_ref, hb_ref, hl_ref, ext_ref, a_ref, b_ref, cf_ref, cbk_ref,
                *, tm, nt):
    i = pl.program_id(0)
    ngroups = tm // SUBLANES

    @pl.when(i == 0)
    def _():
        cf_ref[...] = jnp.broadcast_to(h0_ref[0:1, :], (SUBLANES, D_RNN))
        cbk_ref[...] = jnp.broadcast_to(h0_ref[1:2, :], (SUBLANES, D_RNN))

    softplus_neg_lam = jax.nn.softplus(-lam_ref[...])
    sub = lax.broadcasted_iota(jnp.int32, (tm, RNN_CHUNK), 0) & (SUBLANES - 1)

    def run(d, prev_ref, main_ref, next_ref, at_start, at_end, out_ref, carry_ref):
        ext_ref[0:SUBLANES, :] = jnp.where(at_start, 0.0, prev_ref[...])
        ext_ref[SUBLANES:tm + SUBLANES, :] = main_ref[...]
        ext_ref[tm + SUBLANES:tm + 2 * SUBLANES, :] = jnp.where(at_end, 0.0, next_ref[...])
        u = cb_ref[...]
        for k in range(CONV_W):
            start = SUBLANES - CONV_W // 2 + k
            u = u + ext_ref[start:start + tm, :] * cw_ref[k:k + 1, :]
        ub = u.astype(BF16)
        for c in range(D_RNN // RNN_CHUNK):
            ch = slice(c * RNN_CHUNK, (c + 1) * RNN_CHUNK)
            g = jnp.dot(ub[:, ch], wg_ref[0, d, c], preferred_element_type=F32)
            r = jax.nn.sigmoid(g[:, :RNN_CHUNK] + bg_ref[d, 0:1, ch])
            ig = jax.nn.sigmoid(g[:, RNN_CHUNK:] + bg_ref[d, 1:2, ch])
            log_a = (-LRU_C * r) * softplus_neg_lam[d:d + 1, ch]
            a = jnp.exp(log_a)
            mult = jnp.sqrt(jnp.maximum(1.0 - a * a, 0.0))
            b = mult * (ig * u[:, ch])
            for dd in (1, 2, 4):
                if d == 0:
                    a_sh, b_sh, ok = pltpu.roll(a, dd, axis=0), pltpu.roll(b, dd, axis=0), sub >= dd
                else:
                    a_sh, b_sh = pltpu.roll(a, tm - dd, axis=0), pltpu.roll(b, tm - dd, axis=0)
                    ok = sub < SUBLANES - dd
                b = a * jnp.where(ok, b_sh, 0.0) + b
                a = a * jnp.where(ok, a_sh, 1.0)
            a_ref[:, ch] = a
            b_ref[:, ch] = b

        def body(g, carry):
            gg = g if d == 0 else ngroups - 1 - g
            off = pl.multiple_of(gg * SUBLANES, SUBLANES)
            h = b_ref[pl.ds(off, SUBLANES), :] + a_ref[pl.ds(off, SUBLANES), :] * carry
            out_ref[pl.ds(off, SUBLANES), :] = h
            edge = h[SUBLANES - 1:SUBLANES, :] if d == 0 else h[0:1, :]
            return jnp.broadcast_to(edge, (SUBLANES, D_RNN))

        carry_ref[...] = lax.fori_loop(0, ngroups, body, carry_ref[...], unroll=8)

    run(0, pf_ref, mf_ref, nf_ref, i == 0, i == nt - 1, hf_ref, cf_ref)
    run(1, pb_ref, mb_ref, nb_ref, i == nt - 1, i == 0, hb_ref, cbk_ref)

    @pl.when(i == nt - 1)
    def _():
        rows = lax.broadcasted_iota(jnp.int32, (SUBLANES, D_RNN), 0)
        hl_ref[...] = jnp.where(rows == 0, cf_ref[...], jnp.where(rows == 1, cbk_ref[...], 0.0))


def rnn_bidir(xr, conv_w, conv_b, wg, bg, lam, h0, *, layer, tm):
    n = xr.shape[0]
    nt = n // tm
    per = tm // SUBLANES
    last8 = n // SUBLANES - 1
    fwd = lambda i: i
    bwd = lambda i: nt - 1 - i
    main = lambda t: pl.BlockSpec((tm, D_RNN), lambda i: (t(i), 0))
    prev = lambda t: pl.BlockSpec((SUBLANES, D_RNN), lambda i: (jnp.maximum(t(i) * per - 1, 0), 0))
    nxt = lambda t: pl.BlockSpec((SUBLANES, D_RNN), lambda i: (jnp.minimum((t(i) + 1) * per, last8), 0))
    whole = lambda a: pl.BlockSpec(a.shape, lambda i: (0,) * a.ndim)
    return pl.pallas_call(
        functools.partial(_rnn_kernel, tm=tm, nt=nt),
        grid=(nt,),
        in_specs=[prev(fwd), main(fwd), nxt(fwd), prev(bwd), main(bwd), nxt(bwd),
                  whole(conv_w), whole(conv_b),
                  pl.BlockSpec((1,) + wg.shape[1:], lambda i: (layer, 0, 0, 0, 0)),
                  whole(bg), whole(lam), whole(h0)],
        out_specs=[main(fwd), main(bwd), pl.BlockSpec((SUBLANES, D_RNN), lambda i: (0, 0))],
        out_shape=[
            jax.ShapeDtypeStruct((n, D_RNN), F32),
            jax.ShapeDtypeStruct((n, D_RNN), F32),
            jax.ShapeDtypeStruct((SUBLANES, D_RNN), F32),
        ],
        scratch_shapes=[
            pltpu.VMEM((tm + 2 * SUBLANES, D_RNN), F32),
            pltpu.VMEM((tm, D_RNN), F32),
            pltpu.VMEM((tm, D_RNN), F32),
            pltpu.VMEM((SUBLANES, D_RNN), F32),
            pltpu.VMEM((SUBLANES, D_RNN), F32),
        ],
        compiler_params=_params("arbitrary"),
        name="rnn_bidir",
    )(xr, xr, xr, xr, xr, xr, conv_w, conv_b, wg, bg, lam, h0)


def _attn_kernel(*refs, nb, band, n_ctx):
    if band:
        sink_ref, q_ref, kvx_ref, kvp_ref, kvc_ref, kvn_ref, o_ref = refs
        kv = jnp.concatenate([kvx_ref[...], kvp_ref[...], kvc_ref[...], kvn_ref[...]], axis=0)
    else:
        sink_ref, q_ref, kvx_ref, o_ref = refs
        kv = kvx_ref[...]
    i = pl.program_id(0)
    nk = kv.shape[0]
    lane = lax.broadcasted_iota(jnp.int32, (nk, LANES), 1)
    low = lane < HEAD_DIM
    zero = jnp.zeros((), BF16)
    k, ks, v, vs = (kv[:, j * LANES:(j + 1) * LANES] for j in range(4))
    k_low = (jnp.where(low, k, zero), jnp.where(low, ks, zero))
    k_high = (jnp.where(low, zero, ks), jnp.where(low, zero, k))
    v_low = (jnp.where(low, v, zero), jnp.where(low, vs, zero))
    v_high = (jnp.where(low, zero, vs), jnp.where(low, zero, v))

    rows2 = 2 * BLOCK_Q
    r = lax.broadcasted_iota(jnp.int32, (rows2, nk), 0) & (BLOCK_Q - 1)
    if band:
        jb = lax.broadcasted_iota(jnp.int32, (rows2, nk), 1) - n_ctx
        valid = (jb < 0) | ((jb >= r) & (jb <= r + 2 * WINDOW)
                            & ((jb >= BLOCK_Q) | (i > 0)) & ((jb < 2 * BLOCK_Q) | (i < nb - 1)))
    upper = lax.broadcasted_iota(jnp.int32, (rows2, 1), 0) >= BLOCK_Q

    def probs(q2, kmat, sink_a, sink_b):
        s = lax.dot_general(q2, kmat, (((1,), (1,)), ((), ())), preferred_element_type=F32)
        if band:
            s = jnp.where(valid, s, NEG_INF)
        sink = jnp.where(upper, sink_b, sink_a)
        m = jnp.maximum(jnp.max(s, axis=-1, keepdims=True), sink)
        p = jnp.exp(s - m)
        den = jnp.sum(p, axis=-1, keepdims=True) + jnp.exp(sink - m)
        return p.astype(BF16), 1.0 / den

    for hk in range(N_KV_HEADS):
        c0, c1 = 2 * hk, 2 * hk + 1
        q2 = jnp.concatenate([q_ref[:, c0 * LANES:(c0 + 1) * LANES],
                              q_ref[:, c1 * LANES:(c1 + 1) * LANES]], axis=0)
        h0 = hk * HEADS_PER_KV
        p_low, inv_low = probs(q2, k_low[hk], sink_ref[h0], sink_ref[h0 + 2])
        p_high, inv_high = probs(q2, k_high[hk], sink_ref[h0 + 1], sink_ref[h0 + 3])
        o2 = (jnp.dot(p_low, v_low[hk], preferred_element_type=F32) * inv_low
              + jnp.dot(p_high, v_high[hk], preferred_element_type=F32) * inv_high)
        o_ref[:, c0 * LANES:(c0 + 1) * LANES] = o2[:BLOCK_Q].astype(o_ref.dtype)
        o_ref[:, c1 * LANES:(c1 + 1) * LANES] = o2[BLOCK_Q:].astype(o_ref.dtype)


def attention(q, kv, kv_ctx, sink, *, band):
    n = q.shape[0]
    nb = n // BLOCK_Q
    n_ctx = kv_ctx.shape[0]
    in_specs = [
        pl.BlockSpec(memory_space=pltpu.SMEM),
        pl.BlockSpec((BLOCK_Q, D_ATTN), lambda i: (i, 0)),
        pl.BlockSpec((n_ctx, 4 * LANES), lambda i: (0, 0)),
    ]
    args = [sink, q, kv_ctx]
    if band:
        blk = lambda f: pl.BlockSpec((BLOCK_Q, 4 * LANES), lambda i: (f(i), 0))
        in_specs += [blk(lambda i: jnp.maximum(i - 1, 0)), blk(lambda i: i),
                     blk(lambda i: jnp.minimum(i + 1, nb - 1))]
        args += [kv, kv, kv]
    return pl.pallas_call(
        functools.partial(_attn_kernel, nb=nb, band=band, n_ctx=n_ctx),
        grid=(nb,),
        in_specs=in_specs,
        out_specs=pl.BlockSpec((BLOCK_Q, D_ATTN), lambda i: (i, 0)),
        out_shape=jax.ShapeDtypeStruct((n, D_ATTN), BF16),
        compiler_params=_params("arbitrary"),
        name="attention",
    )(*args)


def _merge_kernel(*refs, row, moe):
    if moe == "comb":
        (x_ref, hf_ref, hb_ref, gr_ref, o_ref, mod_ref, gg_ref, wo_ref, gf_ref, wrh_ref, wrl_ref,
         xo_ref, h2_ref, comb_ref) = refs
    elif moe == "route":
        (x_ref, hf_ref, hb_ref, gr_ref, o_ref, mod_ref, gg_ref, wo_ref, gf_ref, wrh_ref, wrl_ref,
         xo_ref, h2_ref, mi_ref, mf_ref, cnt_ref, run_ref) = refs
    else:
        x_ref, hf_ref, hb_ref, gr_ref, o_ref, mod_ref, gg_ref, wo_ref, gf_ref, xo_ref, h2_ref = refs
    mod = lambda off: mod_ref[0, row:row + 1, off:off + D_MODEL]
    y_rnn = (hf_ref[...] + hb_ref[...]) * jax.nn.gelu(gr_ref[...].astype(F32))
    y = jnp.concatenate([_rms(y_rnn) * gg_ref[:, :D_RNN],
                         _rms(o_ref[...].astype(F32)) * gg_ref[:, D_RNN:]], axis=1)
    out = jnp.dot(y.astype(BF16), wo_ref[0], preferred_element_type=F32)
    x = x_ref[...] + mod(GT_M) * out
    xo_ref[...] = x
    h2 = (_rms(x) * gf_ref[...]) * (1.0 + mod(SC_F)) + mod(SH_F)
    h2_ref[...] = h2.astype(h2_ref.dtype)
    if moe:
        hi = h2.astype(BF16)
        lo = (h2 - hi.astype(F32)).astype(BF16)
        dot = lambda a, b: jnp.dot(a, b, preferred_element_type=F32)
        logits = dot(hi, wrh_ref[0]) + (dot(hi, wrl_ref[0]) + dot(lo, wrh_ref[0]))
        lane = lax.broadcasted_iota(jnp.int32, logits.shape, 1)
        logits = jnp.where(lane < N_EXPERTS, logits, -jnp.inf)
        v1 = jnp.max(logits, axis=-1, keepdims=True)
        i1 = jnp.min(jnp.where(logits == v1, lane, LANES), axis=-1, keepdims=True)
        rest = jnp.where(lane == i1, -jnp.inf, logits)
        v2 = jnp.max(rest, axis=-1, keepdims=True)
        i2 = jnp.min(jnp.where(rest == v2, lane, LANES), axis=-1, keepdims=True)
        e2 = jnp.exp(v2 - v1)
        w1 = 1.0 / (1.0 + e2)
        w2 = e2 / (1.0 + e2)
        if moe == "comb":
            comb_ref[...] = jnp.where(lane == i1, w1, 0.0) + jnp.where(lane == i2, w2, 0.0)
        else:
            tm = logits.shape[0]

            @pl.when(pl.program_id(0) == 0)
            def _():
                run_ref[...] = jnp.zeros_like(run_ref)

            chosen = (lane == i1) | (lane == i2)
            before = (lax.broadcasted_iota(jnp.int32, (tm, tm), 0)
                      > lax.broadcasted_iota(jnp.int32, (tm, tm), 1))
            prefix = dot(jnp.where(before, 1.0, 0.0).astype(BF16), jnp.where(chosen, 1.0, 0.0).astype(BF16))
            rank = prefix + run_ref[...]
            r1 = jnp.sum(jnp.where(lane == i1, rank, 0.0), axis=-1, keepdims=True).astype(jnp.int32)
            r2 = jnp.sum(jnp.where(lane == i2, rank, 0.0), axis=-1, keepdims=True).astype(jnp.int32)
            run_ref[...] += jnp.sum(jnp.where(chosen, 1.0, 0.0), axis=0, keepdims=True)
            meta_i = jnp.where(lane == 0, i1, jnp.where(lane == 1, i2,
                               jnp.where(lane == 2, r1, jnp.where(lane == 3, r2, 0))))
            mi_ref[...] = meta_i[:, :META_W]
            mf_ref[...] = jnp.where(lane == 0, w1, jnp.where(lane == 1, w2, 0.0))[:, :META_W]
            cnt_ref[...] = jnp.broadcast_to(run_ref[...], cnt_ref.shape).astype(jnp.int32)


META_W = 8


def merge(x, hf, hb, gr, o, mod, g_grp, w_out, g_ffn, router, *, layer, moe_layer, row, tm, moe):
    n = x.shape[0]
    tile = lambda w: pl.BlockSpec((tm, w), lambda i: (i, 0))
    whole = lambda a: pl.BlockSpec(a.shape, lambda i: (0,) * a.ndim)
    in_specs = [tile(D_MODEL), tile(D_RNN), tile(D_RNN), tile(D_RNN), tile(D_ATTN),
                whole(mod), whole(g_grp),
                pl.BlockSpec((1, D_MODEL, D_MODEL), lambda i: (layer, 0, 0)), whole(g_ffn)]
    args = [x, hf, hb, gr, o, mod, g_grp, w_out, g_ffn]
    out_specs = [tile(D_MODEL), tile(D_MODEL)]
    h2_dtype = F32 if moe == "route" else BF16
    out_shape = [jax.ShapeDtypeStruct((n, D_MODEL), F32), jax.ShapeDtypeStruct((n, D_MODEL), h2_dtype)]
    scratch = []
    if moe:
        in_specs += [pl.BlockSpec((1, D_MODEL, LANES), lambda i: (moe_layer, 0, 0))] * 2
        args += list(router)
    if moe == "comb":
        out_specs.append(tile(LANES))
        out_shape.append(jax.ShapeDtypeStruct((n, LANES), F32))
    elif moe == "route":
        out_specs += [tile(META_W), tile(META_W), pl.BlockSpec((SUBLANES, LANES), lambda i: (0, 0))]
        out_shape += [jax.ShapeDtypeStruct((n, META_W), jnp.int32), jax.ShapeDtypeStruct((n, META_W), F32),
                      jax.ShapeDtypeStruct((SUBLANES, LANES), jnp.int32)]
        scratch = [pltpu.VMEM((1, LANES), F32)]
    return pl.pallas_call(
        functools.partial(_merge_kernel, row=row, moe=moe),
        grid=(n // tm,),
        in_specs=in_specs,
        out_specs=out_specs,
        out_shape=out_shape,
        scratch_shapes=scratch,
        compiler_params=_params("arbitrary"),
        name="merge",
    )(*args)


def _ffn_kernel(*refs, row, n_exp, nf, moe, final):
    refs = list(refs)
    h_ref, x_ref, mod_ref = refs[:3]
    pos = 3
    comb_ref = None
    if moe:
        comb_ref = refs[pos]
        pos += 1
    gfin_ref = None
    if final:
        gfin_ref = refs[pos]
        pos += 1
    wg_ref, wu_ref, wd_ref, o_ref, acc_ref = refs[pos:pos + 5]
    tot_ref = refs[pos + 5] if moe else acc_ref
    e = pl.program_id(1)
    f = pl.program_id(2)

    @pl.when(f == 0)
    def _():
        acc_ref[...] = jnp.zeros_like(acc_ref)

    if moe:
        @pl.when((e == 0) & (f == 0))
        def _():
            tot_ref[...] = jnp.zeros_like(tot_ref)

    h = h_ref[...]
    a = jnp.dot(h, wg_ref[0], preferred_element_type=F32)
    b = jnp.dot(h, wu_ref[0], preferred_element_type=F32)
    act = (a * jax.nn.sigmoid(a)) * b
    acc_ref[...] += jnp.dot(act.astype(BF16), wd_ref[0], preferred_element_type=F32)

    if moe:
        @pl.when(f == nf - 1)
        def _():
            comb = comb_ref[...]
            lane = lax.broadcasted_iota(jnp.int32, comb.shape, 1)
            ce = jnp.sum(jnp.where(lane == e, comb, 0.0), axis=-1, keepdims=True)
            tot_ref[...] += ce * acc_ref[...]

    @pl.when((e == n_exp - 1) & (f == nf - 1))
    def _():
        x = x_ref[...] + mod_ref[0, row:row + 1, GT_F:GT_F + D_MODEL] * tot_ref[...]
        if final:
            x = _rms(x) * gfin_ref[...]
        o_ref[...] = x


def ffn(h2, x, mod, comb, g_final, wg, wu, wd, *, first, n_exp, row, tm, tf):
    n = x.shape[0]
    d_ff = wg.shape[2]
    nf = d_ff // tf
    moe = comb is not None
    final = g_final is not None
    tile = lambda w: pl.BlockSpec((tm, w), lambda t, e, f: (t, 0))
    in_specs = [tile(D_MODEL), tile(D_MODEL), pl.BlockSpec(mod.shape, lambda t, e, f: (0, 0, 0))]
    args = [h2, x, mod]
    if moe:
        in_specs.append(tile(LANES))
        args.append(comb)
    if final:
        in_specs.append(pl.BlockSpec((1, D_MODEL), lambda t, e, f: (0, 0)))
        args.append(g_final)
    in_specs += [
        pl.BlockSpec((1, D_MODEL, tf), lambda t, e, f: (first + e, 0, f)),
        pl.BlockSpec((1, D_MODEL, tf), lambda t, e, f: (first + e, 0, f)),
        pl.BlockSpec((1, tf, D_MODEL), lambda t, e, f: (first + e, f, 0)),
    ]
    args += [wg, wu, wd]
    scratch = [pltpu.VMEM((tm, D_MODEL), F32)]
    if moe:
        scratch.append(pltpu.VMEM((tm, D_MODEL), F32))
    return pl.pallas_call(
        functools.partial(_ffn_kernel, row=row, n_exp=n_exp, nf=nf, moe=moe, final=final),
        grid=(n // tm, n_exp, nf),
        in_specs=in_specs,
        out_specs=tile(D_MODEL),
        out_shape=jax.ShapeDtypeStruct((n, D_MODEL), F32),
        scratch_shapes=scratch,
        compiler_params=_params("arbitrary", "arbitrary", "arbitrary"),
        name="ffn",
    )(*args)


MOE_TILE = 512


def _route_tables(cnt, meta_i, n_pad):
    i32 = jnp.int32
    counts = cnt[0, :N_EXPERTS]
    tiles = (counts + MOE_TILE - 1) // MOE_TILE
    cum = jnp.cumsum(tiles)
    starts = (cum - tiles) * MOE_TILE
    experts = jnp.arange(N_EXPERTS, dtype=i32)
    start_of = lambda e: jnp.sum(jnp.where(e[:, None] == experts[None, :], starts[None, :], 0), axis=1)
    pos1 = (meta_i[:, 2] + start_of(meta_i[:, 0])).astype(i32)
    pos2 = (meta_i[:, 3] + start_of(meta_i[:, 1])).astype(i32)
    n_tiles = cum[-1]
    t = jnp.arange(n_pad // MOE_TILE, dtype=i32)
    te = jnp.sum((t[:, None] >= cum[None, :]).astype(i32), axis=1)
    used = t < n_tiles
    tile_expert = jnp.where(used, te, jnp.max(jnp.where(used, te, 0))).astype(i32)
    zero_start = jnp.concatenate([starts + counts, (n_tiles * MOE_TILE)[None]]).astype(i32)
    zero_len = jnp.concatenate([tiles * MOE_TILE - counts, (n_pad - n_tiles * MOE_TILE)[None]]).astype(i32)
    return pos1, pos2, tile_expert, n_tiles.reshape(1).astype(i32), zero_start, zero_len


def _dispatch_kernel(pos1_ref, pos2_ref, zs_ref, zl_ref, h_ref, x_hbm, zero_ref, sem, *, tm):
    i = pl.program_id(0)

    def row_copy(src, r, p, s):
        return pltpu.make_async_copy(src.at[pl.ds(r, 1)], x_hbm.at[pl.ds(p, 1)], sem.at[s])

    @pl.when(i == 0)
    def _():
        zero_ref[...] = jnp.zeros_like(zero_ref)
        for k in range(N_EXPERTS + 1):
            def issue(j, c):
                row_copy(zero_ref, 0, zs_ref[k] + j, 2).start()
                return c
            lax.fori_loop(0, zl_ref[k], issue, 0)
        for k in range(N_EXPERTS + 1):
            def drain(j, c):
                row_copy(zero_ref, 0, zs_ref[k] + j, 2).wait()
                return c
            lax.fori_loop(0, zl_ref[k], drain, 0)

    base = i * tm

    def issue(g, c):
        r0 = pl.multiple_of(g * SUBLANES, SUBLANES)
        for s in range(SUBLANES):
            row_copy(h_ref, r0 + s, pos1_ref[base + r0 + s], 0).start()
            row_copy(h_ref, r0 + s, pos2_ref[base + r0 + s], 1).start(priority=1)
        return c

    lax.fori_loop(0, tm // SUBLANES, issue, 0)

    def drain(g, c):
        r0 = pl.multiple_of(g * SUBLANES, SUBLANES)
        for s in range(SUBLANES):
            row_copy(h_ref, r0 + s, pos1_ref[base + r0 + s], 0).wait()
            row_copy(h_ref, r0 + s, pos2_ref[base + r0 + s], 1).wait()
        return c

    lax.fori_loop(0, tm // SUBLANES, drain, 0)


def moe_dispatch(pos1, pos2, zero_start, zero_len, h2, n_pad, *, tm):
    n = h2.shape[0]
    return pl.pallas_call(
        functools.partial(_dispatch_kernel, tm=tm),
        grid_spec=pltpu.PrefetchScalarGridSpec(
            num_scalar_prefetch=4,
            grid=(n // tm,),
            in_specs=[pl.BlockSpec((tm, D_MODEL), lambda i, *_: (i, 0))],
            out_specs=pl.BlockSpec(memory_space=pl.ANY),
            scratch_shapes=[pltpu.VMEM((SUBLANES, D_MODEL), F32), pltpu.SemaphoreType.DMA((3,))],
        ),
        out_shape=jax.ShapeDtypeStruct((n_pad, D_MODEL), F32),
        compiler_params=_params("arbitrary"),
        name="moe_dispatch",
    )(pos1, pos2, zero_start, zero_len, h2)


def _grouped_ffn_kernel(te_ref, nt_ref, x_ref, wg_ref, wu_ref, wd_ref, y_ref):
    t = pl.program_id(0)

    @pl.when(t < nt_ref[0])
    def _():
        x = x_ref[...].astype(BF16)
        a = jnp.dot(x, wg_ref[0], preferred_element_type=F32)
        b = jnp.dot(x, wu_ref[0], preferred_element_type=F32)
        act = (a * jax.nn.sigmoid(a)) * b
        y_ref[...] = jnp.dot(act.astype(BF16), wd_ref[0], preferred_element_type=F32)

    @pl.when(t >= nt_ref[0])
    def _():
        y_ref[...] = jnp.zeros_like(y_ref)


def grouped_ffn(tile_expert, n_tiles, xs, wg, wu, wd, *, first):
    n_pad = xs.shape[0]
    d_ff = wg.shape[2]
    rows = pl.BlockSpec((MOE_TILE, D_MODEL), lambda t, te, nt: (t, 0))
    return pl.pallas_call(
        _grouped_ffn_kernel,
        grid_spec=pltpu.PrefetchScalarGridSpec(
            num_scalar_prefetch=2,
            grid=(n_pad // MOE_TILE,),
            in_specs=[rows,
                      pl.BlockSpec((1, D_MODEL, d_ff), lambda t, te, nt: (first + te[t], 0, 0)),
                      pl.BlockSpec((1, D_MODEL, d_ff), lambda t, te, nt: (first + te[t], 0, 0)),
                      pl.BlockSpec((1, d_ff, D_MODEL), lambda t, te, nt: (first + te[t], 0, 0))],
            out_specs=rows,
        ),
        out_shape=jax.ShapeDtypeStruct((n_pad, D_MODEL), F32),
        compiler_params=_params("arbitrary"),
        name="grouped_ffn",
    )(tile_expert, n_tiles, xs, wg, wu, wd)


def _combine_kernel(*refs, row, tm, final):
    if final:
        pos1_ref, pos2_ref, x_ref, mf_ref, mod_ref, gfin_ref, y_hbm, o_ref, b1_ref, b2_ref, sem = refs
    else:
        pos1_ref, pos2_ref, x_ref, mf_ref, mod_ref, y_hbm, o_ref, b1_ref, b2_ref, sem = refs
    base = pl.program_id(0) * tm

    def row_copy(p, dst, r, s):
        return pltpu.make_async_copy(y_hbm.at[pl.ds(p, 1)], dst.at[pl.ds(r, 1)], sem.at[s])

    def issue(g, c):
        r0 = pl.multiple_of(g * SUBLANES, SUBLANES)
        for s in range(SUBLANES):
            row_copy(pos1_ref[base + r0 + s], b1_ref, r0 + s, 0).start()
            row_copy(pos2_ref[base + r0 + s], b2_ref, r0 + s, 1).start(priority=1)
        return c

    lax.fori_loop(0, tm // SUBLANES, issue, 0)

    def drain(g, c):
        r0 = pl.multiple_of(g * SUBLANES, SUBLANES)
        for s in range(SUBLANES):
            row_copy(pos1_ref[base + r0 + s], b1_ref, r0 + s, 0).wait()
            row_copy(pos2_ref[base + r0 + s], b2_ref, r0 + s, 1).wait()
        return c

    lax.fori_loop(0, tm // SUBLANES, drain, 0)

    tot = mf_ref[:, 0:1] * b1_ref[...] + mf_ref[:, 1:2] * b2_ref[...]
    x = x_ref[...] + mod_ref[0, row:row + 1, GT_F:GT_F + D_MODEL] * tot
    if final:
        x = _rms(x) * gfin_ref[...]
    o_ref[...] = x


def moe_combine(pos1, pos2, x, meta_f, mod, g_final, ys, *, row, tm):
    n = x.shape[0]
    final = g_final is not None
    tile = lambda w: pl.BlockSpec((tm, w), lambda i, *_: (i, 0))
    in_specs = [tile(D_MODEL), tile(META_W), pl.BlockSpec(mod.shape, lambda i, *_: (0, 0, 0))]
    args = [x, meta_f, mod]
    if final:
        in_specs.append(pl.BlockSpec((1, D_MODEL), lambda i, *_: (0, 0)))
        args.append(g_final)
    in_specs.append(pl.BlockSpec(memory_space=pl.ANY))
    args.append(ys)
    return pl.pallas_call(
        functools.partial(_combine_kernel, row=row, tm=tm, final=final),
        grid_spec=pltpu.PrefetchScalarGridSpec(
            num_scalar_prefetch=2,
            grid=(n // tm,),
            in_specs=in_specs,
            out_specs=tile(D_MODEL),
            scratch_shapes=[pltpu.VMEM((tm, D_MODEL), F32), pltpu.VMEM((tm, D_MODEL), F32),
                            pltpu.SemaphoreType.DMA((2,))],
        ),
        out_shape=jax.ShapeDtypeStruct((n, D_MODEL), F32),
        compiler_params=_params("arbitrary"),
        name="moe_combine",
    )(pos1, pos2, *args)


def _gate_weights(w_rg):
    depth = w_rg.shape[0]
    per = RNN_CHUNK // RNN_BLOCK
    nch = D_RNN // RNN_CHUNK
    wr = w_rg.astype(BF16).reshape(depth, 2, 2, nch, per, RNN_BLOCK, 1, RNN_BLOCK)
    on_diag = jnp.eye(per, dtype=bool).reshape(per, 1, per, 1)
    bd = jnp.where(on_diag, wr, jnp.zeros((), BF16))
    bd = bd.reshape(depth, 2, 2, nch, RNN_CHUNK, RNN_CHUNK)
    return jnp.concatenate([bd[:, :, 0], bd[:, :, 1]], axis=-1)


def _tile_rows(n, want):
    return want if n % want == 0 else n


def kernel(x, c, ctx, c_ctx, w_mod, b_mod, g_mix, g_ffn, w_in, conv_w, conv_b, w_rg, b_rg, lam,
           sink, g_grp, w_out, w_ffn_gate, w_ffn_up, w_ffn_down, w_router, w_exp_gate, w_exp_up,
           w_exp_down, g_final):
    assert x.shape[0] == 1 and ctx.shape[0] == 1
    depth = w_mod.shape[0]
    seq, n_ctx = x.shape[1], ctx.shape[1]
    xl, xc = x[0], ctx[0]

    cond8 = jnp.zeros((SUBLANES, D_MODEL), F32).at[0].set(c[0]).at[1].set(c_ctx)
    mods = adaln_all(cond8, w_mod, b_mod)
    cos, sin = rope_tables(seq)

    w_in_b = w_in.astype(BF16)
    w_out_b = w_out.astype(BF16)
    wg_gate = _gate_weights(w_rg)
    ffn_w = (w_ffn_gate.astype(BF16), w_ffn_up.astype(BF16), w_ffn_down.astype(BF16))
    exp_w = tuple(w.astype(BF16).reshape((-1,) + w.shape[2:]) for w in (w_exp_gate, w_exp_up, w_exp_down))
    wr_pad = jnp.pad(w_router, ((0, 0), (0, 0), (0, LANES - N_EXPERTS)))
    wr_hi = wr_pad.astype(BF16)
    wr_lo = (wr_pad - wr_hi.astype(F32)).astype(BF16)
    zeros_h0 = jnp.zeros((SUBLANES, D_RNN), F32)

    tm = _tile_rows(seq, 512)
    tm_ffn = _tile_rows(seq, 1024)

    for l in range(depth):
        last = l == depth - 1
        moe = l % 2 == 1
        j = l // 2
        mod = mods[l:l + 1]
        gm, gf, gg = g_mix[l:l + 1], g_ffn[l:l + 1], g_grp[l:l + 1]

        xr_c, gr_c, q_c, kv_c = in_proj(xc, mod, gm, w_in_b, None, None, layer=l, row=1, tm=n_ctx)
        xr_l, gr_l, q_l, kv_l = in_proj(xl, mod, gm, w_in_b, cos, sin, layer=l, row=0, tm=tm)

        rnn_args = (conv_w[l], conv_b[l:l + 1], wg_gate, b_rg[l], lam[l])
        hf_c, hb_c, h_ctx_end = rnn_bidir(xr_c, *rnn_args, zeros_h0, layer=l, tm=n_ctx)
        hf_l, hb_l, _ = rnn_bidir(xr_l, *rnn_args, h_ctx_end, layer=l, tm=tm)

        o_l = attention(q_l, kv_l, kv_c, sink[l], band=True)

        router = (wr_hi, wr_lo) if moe else None
        weights = exp_w if moe else ffn_w
        first = j * N_EXPERTS if moe else j
        tf = 512
        merge_l = functools.partial(merge, layer=l, moe_layer=j)

        g_fin = g_final.reshape(1, D_MODEL) if last else None
        if moe:
            xl, h2, meta_i, meta_f, cnt = merge_l(xl, hf_l, hb_l, gr_l, o_l, mod, gg, w_out_b, gf, router,
                                                  row=0, tm=tm, moe="route")
            n_pad = 2 * seq + N_EXPERTS * MOE_TILE
            pos1, pos2, tile_expert, n_tiles, zero_start, zero_len = _route_tables(cnt, meta_i, n_pad)
            xs = moe_dispatch(pos1, pos2, zero_start, zero_len, h2, n_pad, tm=tm)
            ys = grouped_ffn(tile_expert, n_tiles, xs, *weights, first=first)
            xl = moe_combine(pos1, pos2, xl, meta_f, mod, g_fin, ys, row=0, tm=tm)
        else:
            xl, h2 = merge_l(xl, hf_l, hb_l, gr_l, o_l, mod, gg, w_out_b, gf, None, row=0, tm=tm, moe=None)
            xl = ffn(h2, xl, mod, None, g_fin, *weights, first=first, n_exp=1, row=0, tm=tm_ffn, tf=tf)

        if not last:
            o_c = attention(q_c, kv_c, kv_c, sink[l], band=False)
            res = merge_l(xc, hf_c, hb_c, gr_c, o_c, mod, gg, w_out_b, gf, router, row=1, tm=n_ctx,
                          moe="comb" if moe else None)
            xc, h2c = res[0], res[1]
            comb_c = res[2] if moe else None
            xc = ffn(h2c, xc, mod, comb_c, None, *weights, first=first, n_exp=N_EXPERTS if moe else 1,
                     row=1, tm=n_ctx, tf=tf)

    return xl[None]
```

```python
import functools

import jax
import jax.numpy as jnp
from jax import lax
from jax.experimental import pallas as pl
from jax.experimental.pallas import tpu as pltpu

F32 = jnp.float32
BF16 = jnp.bfloat16

D_MODEL = 1024
D_RNN = 512
D_ATTN = 512
D_KV = 128
HEAD_DIM = 64
N_HEADS = 8
N_KV_HEADS = 2
HEADS_PER_KV = N_HEADS // N_KV_HEADS
D_IN = 2 * D_RNN + D_ATTN + 2 * D_KV
RNN_BLOCK = 64
RNN_CHUNK = 256
CONV_W = 4
LRU_C = 8.0
WINDOW = 128
BLOCK_Q = 128
GRID_W = 64
ROPE_BASE = 10000.0
N_EXPERTS = 8
EPS = 1e-6
NEG_INF = -1e30

LANES = 128
SUBLANES = 8
VMEM_LIMIT = 56 * 1024 * 1024

SH_M, SC_M, GT_M, SH_F, SC_F, GT_F = (k * D_MODEL for k in range(6))


def _params(*sem):
    return pltpu.CompilerParams(dimension_semantics=sem, vmem_limit_bytes=VMEM_LIMIT)


def _rms(x):
    return x * lax.rsqrt(jnp.mean(x * x, axis=-1, keepdims=True) + EPS)


def _adaln_kernel(c_ref, w_ref, b_ref, o_ref):
    c = c_ref[...]
    s = c * jax.nn.sigmoid(c)
    w = w_ref[0]
    s_hi, w_hi = s.astype(BF16), w.astype(BF16)
    s_lo = (s - s_hi.astype(F32)).astype(BF16)
    w_lo = (w - w_hi.astype(F32)).astype(BF16)
    dot = lambda a, b: jnp.dot(a, b, preferred_element_type=F32)
    o_ref[0] = dot(s_hi, w_hi) + (dot(s_hi, w_lo) + dot(s_lo, w_hi)) + b_ref[0]


def adaln_all(cond8, w_mod, b_mod):
    depth = w_mod.shape[0]
    nc = 1536
    return pl.pallas_call(
        _adaln_kernel,
        grid=(depth, 6 * D_MODEL // nc),
        in_specs=[
            pl.BlockSpec((SUBLANES, D_MODEL), lambda l, j: (0, 0)),
            pl.BlockSpec((1, D_MODEL, nc), lambda l, j: (l, 0, j)),
            pl.BlockSpec((1, 1, nc), lambda l, j: (l, 0, j)),
        ],
        out_specs=pl.BlockSpec((1, SUBLANES, nc), lambda l, j: (l, 0, j)),
        out_shape=jax.ShapeDtypeStruct((depth, SUBLANES, 6 * D_MODEL), F32),
        compiler_params=_params("arbitrary", "arbitrary"),
        name="adaln",
    )(cond8, w_mod, b_mod.reshape(depth, 1, 6 * D_MODEL))


ROPE_ROWS = SUBLANES * GRID_W


def _rope_kernel(f_ref, c_ref, s_ref):
    i = pl.program_id(0)
    lane = lax.broadcasted_iota(jnp.int32, (GRID_W, LANES), 1)
    col_axis = ((lane >> 5) & 1) == 1
    sign = jnp.where(((lane >> 4) & 1) == 0, -1.0, 1.0).astype(F32)
    freq = f_ref[...]
    rowpos = (i * SUBLANES + lax.broadcasted_iota(jnp.int32, (SUBLANES, LANES), 0)).astype(F32)
    colpos = lax.broadcasted_iota(jnp.int32, (GRID_W, LANES), 0).astype(F32)
    ang_r = rowpos * freq
    ang_c = colpos * freq
    cr, sr = jnp.cos(ang_r), jnp.sin(ang_r)
    cc, sc = jnp.cos(ang_c), jnp.sin(ang_c) * sign
    for g in range(SUBLANES):
        rows = slice(g * GRID_W, (g + 1) * GRID_W)
        c_ref[rows, :] = jnp.where(col_axis, cc, jnp.broadcast_to(cr[g:g + 1, :], (GRID_W, LANES)))
        s_ref[rows, :] = jnp.where(col_axis, sc, jnp.broadcast_to(sr[g:g + 1, :], (GRID_W, LANES)) * sign)


def rope_tables(seq):
    axis_dim = HEAD_DIM // 2
    freqs = ROPE_BASE ** (-jnp.arange(0, axis_dim, 2, dtype=F32) / axis_dim)
    freq_lane = jnp.tile(freqs, LANES // freqs.shape[0]).reshape(1, LANES)
    return pl.pallas_call(
        _rope_kernel,
        grid=(seq // ROPE_ROWS,),
        in_specs=[pl.BlockSpec((1, LANES), lambda i: (0, 0))],
        out_specs=[pl.BlockSpec((ROPE_ROWS, LANES), lambda i: (i, 0))] * 2,
        out_shape=[jax.ShapeDtypeStruct((seq, LANES), F32)] * 2,
        compiler_params=_params("arbitrary"),
        name="rope_tables",
    )(freq_lane)


def _rope_chunk(x, cos, sin):
    lane = lax.broadcasted_iota(jnp.int32, x.shape, 1)
    first_half = ((lane >> 4) & 1) == 0
    partner = jnp.where(first_half, pltpu.roll(x, LANES - 16, axis=1), pltpu.roll(x, 16, axis=1))
    return x * cos + partner * sin


def _in_proj_kernel(*refs, row, rope):
    if rope:
        x_ref, mod_ref, g_ref, w_ref, cos_ref, sin_ref, xr_ref, gr_ref, q_ref, kv_ref = refs
    else:
        x_ref, mod_ref, g_ref, w_ref, xr_ref, gr_ref, q_ref, kv_ref = refs
    x = x_ref[...]
    shift = mod_ref[0, row:row + 1, SH_M:SH_M + D_MODEL]
    scale = mod_ref[0, row:row + 1, SC_M:SC_M + D_MODEL]
    h = (_rms(x) * g_ref[...]) * (1.0 + scale) + shift
    p = jnp.dot(h.astype(BF16), w_ref[0], preferred_element_type=F32)
    xr_ref[...] = p[:, 0:D_RNN]
    gr_ref[...] = p[:, D_RNN:2 * D_RNN].astype(gr_ref.dtype)
    q0 = 2 * D_RNN
    if rope:
        cos, sin = cos_ref[...], sin_ref[...]
    for c in range(D_ATTN // LANES):
        qc = p[:, q0 + c * LANES:q0 + (c + 1) * LANES]
        if rope:
            qc = _rope_chunk(qc, cos, sin)
        q_ref[:, c * LANES:(c + 1) * LANES] = (qc * (HEAD_DIM ** -0.5)).astype(BF16)
    k = p[:, q0 + D_ATTN:q0 + D_ATTN + D_KV]
    v = p[:, q0 + D_ATTN + D_KV:q0 + D_ATTN + 2 * D_KV]
    if rope:
        k = _rope_chunk(k, cos, sin)
    kv_ref[:, 0:LANES] = k.astype(BF16)
    kv_ref[:, LANES:2 * LANES] = pltpu.roll(k, HEAD_DIM, axis=1).astype(BF16)
    kv_ref[:, 2 * LANES:3 * LANES] = v.astype(BF16)
    kv_ref[:, 3 * LANES:4 * LANES] = pltpu.roll(v, HEAD_DIM, axis=1).astype(BF16)


def in_proj(x, mod, g_mix, w_in, cos, sin, *, layer, row, tm):
    n = x.shape[0]
    rope = cos is not None
    tile = lambda w: pl.BlockSpec((tm, w), lambda i: (i, 0))
    in_specs = [
        tile(D_MODEL),
        pl.BlockSpec((1, SUBLANES, 6 * D_MODEL), lambda i: (0, 0, 0)),
        pl.BlockSpec((1, D_MODEL), lambda i: (0, 0)),
        pl.BlockSpec((1, D_MODEL, D_IN), lambda i: (layer, 0, 0)),
    ]
    args = [x, mod, g_mix, w_in]
    if rope:
        in_specs += [tile(LANES), tile(LANES)]
        args += [cos, sin]
    return pl.pallas_call(
        functools.partial(_in_proj_kernel, row=row, rope=rope),
        grid=(n // tm,),
        in_specs=in_specs,
        out_specs=[tile(D_RNN), tile(D_RNN), tile(D_ATTN), tile(4 * LANES)],
        out_shape=[
            jax.ShapeDtypeStruct((n, D_RNN), F32),
            jax.ShapeDtypeStruct((n, D_RNN), BF16),
            jax.ShapeDtypeStruct((n, D_ATTN), BF16),
            jax.ShapeDtypeStruct((n, 4 * LANES), BF16),
        ],
        compiler_params=_params("arbitrary"),
        name="in_proj",
    )(*args)


RNN_SLABS = D_RNN // LANES
SEG_PAD = 4


def _rnn_kernel(pf_ref, mf_ref, nf_ref, pb_ref, mb_ref, nb_ref, cw_ref, cb_ref, wg_ref, bg_ref,
                lam_ref, h0_ref, hf_ref, hb_ref, hl_ref, ext_ref, xs_ref, u_ref, a_ref, b_ref, hs_ref,
                cf_ref, cbk_ref, *, tm, nt):
    i = pl.program_id(0)
    seg = tm // SUBLANES
    pitch = seg + SEG_PAD
    slab = lambda c: slice(c * LANES, (c + 1) * LANES)

    @pl.when(i == 0)
    def _():
        cf_ref[...] = jnp.broadcast_to(h0_ref[0:1, :], (SUBLANES, D_RNN))
        cbk_ref[...] = jnp.broadcast_to(h0_ref[1:2, :], (SUBLANES, D_RNN))

    softplus_neg_lam = jax.nn.softplus(-lam_ref[...])
    seg_id = lax.broadcasted_iota(jnp.int32, (SUBLANES, D_RNN), 0)

    def run(d, prev_ref, main_ref, next_ref, at_start, at_end, out_ref, carry_ref):
        ext_ref[0:SUBLANES, :] = jnp.where(at_start, 0.0, prev_ref[...])
        ext_ref[SUBLANES:tm + SUBLANES, :] = main_ref[...]
        ext_ref[tm + SUBLANES:tm + 2 * SUBLANES, :] = jnp.where(at_end, 0.0, next_ref[...])
        halo = SUBLANES - CONV_W // 2
        for j in range(SUBLANES):
            for c in range(RNN_SLABS):
                xs_ref[c, j * pitch:j * pitch + seg + SUBLANES, :] = (
                    ext_ref[j * seg + halo:j * seg + halo + seg + SUBLANES, slab(c)])
        for g in range(seg):
            for c in range(RNN_SLABS):
                acc = cb_ref[:, slab(c)]
                for k in range(CONV_W):
                    acc = acc + xs_ref[c, pl.ds(g + k, SUBLANES, stride=pitch), :] * cw_ref[k:k + 1, slab(c)]
                u_ref[g * SUBLANES:(g + 1) * SUBLANES, slab(c)] = acc
        u = u_ref[...]
        ub = u.astype(BF16)
        for c in range(D_RNN // RNN_CHUNK):
            ch = slice(c * RNN_CHUNK, (c + 1) * RNN_CHUNK)
            g = jnp.dot(ub[:, ch], wg_ref[0, d, c], preferred_element_type=F32)
            r = jax.nn.sigmoid(g[:, :RNN_CHUNK] + bg_ref[d, 0:1, ch])
            ig = jax.nn.sigmoid(g[:, RNN_CHUNK:] + bg_ref[d, 1:2, ch])
            log_a = (-LRU_C * r) * softplus_neg_lam[d:d + 1, ch]
            a = jnp.exp(log_a)
            mult = jnp.sqrt(jnp.maximum(1.0 - a * a, 0.0))
            a_ref[:, ch] = a
            b_ref[:, ch] = mult * (ig * u[:, ch])

        def scan(n, hp):
            g = n if d == 0 else seg - 1 - n
            off = pl.multiple_of(g * SUBLANES, SUBLANES)
            a_g = a_ref[pl.ds(off, SUBLANES), :]
            h = a_g * hp[0] + b_ref[pl.ds(off, SUBLANES), :]
            p = a_g * hp[1]
            b_ref[pl.ds(off, SUBLANES), :] = h
            a_ref[pl.ds(off, SUBLANES), :] = p
            return h, p

        zeros = jnp.zeros((SUBLANES, D_RNN), F32)
        h_end, p_end = lax.fori_loop(0, seg, scan, (zeros, zeros + 1.0), unroll=8)

        carry = carry_ref[...]
        enter = zeros
        for n in range(SUBLANES):
            j = n if d == 0 else SUBLANES - 1 - n
            enter = jnp.where(seg_id == j, carry, enter)
            carry = jnp.broadcast_to(h_end[j:j + 1, :] + p_end[j:j + 1, :] * carry[0:1, :], (SUBLANES, D_RNN))
        carry_ref[...] = carry

        for g in range(seg):
            rows = slice(g * SUBLANES, (g + 1) * SUBLANES)
            h = b_ref[rows, :] + a_ref[rows, :] * enter
            for c in range(RNN_SLABS):
                hs_ref[c, pl.ds(g, SUBLANES, stride=pitch), :] = h[:, slab(c)]
        for j in range(SUBLANES):
            for c in range(RNN_SLABS):
                out_ref[j * seg:(j + 1) * seg, slab(c)] = hs_ref[c, j * pitch:j * pitch + seg, :]

    run(0, pf_ref, mf_ref, nf_ref, i == 0, i == nt - 1, hf_ref, cf_ref)
    run(1, pb_ref, mb_ref, nb_ref, i == nt - 1, i == 0, hb_ref, cbk_ref)

    @pl.when(i == nt - 1)
    def _():
        rows = lax.broadcasted_iota(jnp.int32, (SUBLANES, D_RNN), 0)
        hl_ref[...] = jnp.where(rows == 0, cf_ref[...], jnp.where(rows == 1, cbk_ref[...], 0.0))


def rnn_bidir(xr, conv_w, conv_b, wg, bg, lam, h0, *, layer, tm):
    n = xr.shape[0]
    nt = n // tm
    per = tm // SUBLANES
    last8 = n // SUBLANES - 1
    fwd = lambda i: i
    bwd = lambda i: nt - 1 - i
    main = lambda t: pl.BlockSpec((tm, D_RNN), lambda i: (t(i), 0))
    prev = lambda t: pl.BlockSpec((SUBLANES, D_RNN), lambda i: (jnp.maximum(t(i) * per - 1, 0), 0))
    nxt = lambda t: pl.BlockSpec((SUBLANES, D_RNN), lambda i: (jnp.minimum((t(i) + 1) * per, last8), 0))
    whole = lambda a: pl.BlockSpec(a.shape, lambda i: (0,) * a.ndim)
    return pl.pallas_call(
        functools.partial(_rnn_kernel, tm=tm, nt=nt),
        grid=(nt,),
        in_specs=[prev(fwd), main(fwd), nxt(fwd), prev(bwd), main(bwd), nxt(bwd),
                  whole(conv_w), whole(conv_b),
                  pl.BlockSpec((1,) + wg.shape[1:], lambda i: (layer, 0, 0, 0, 0)),
                  whole(bg), whole(lam), whole(h0)],
        out_specs=[main(fwd), main(bwd), pl.BlockSpec((SUBLANES, D_RNN), lambda i: (0, 0))],
        out_shape=[
            jax.ShapeDtypeStruct((n, D_RNN), F32),
            jax.ShapeDtypeStruct((n, D_RNN), F32),
            jax.ShapeDtypeStruct((SUBLANES, D_RNN), F32),
        ],
        scratch_shapes=[
            pltpu.VMEM((tm + 2 * SUBLANES, D_RNN), F32),
            pltpu.VMEM((RNN_SLABS, tm + SUBLANES * SEG_PAD + SUBLANES, LANES), F32),
            pltpu.VMEM((tm, D_RNN), F32),
            pltpu.VMEM((tm, D_RNN), F32),
            pltpu.VMEM((tm, D_RNN), F32),
            pltpu.VMEM((RNN_SLABS, tm + SUBLANES * SEG_PAD, LANES), F32),
            pltpu.VMEM((SUBLANES, D_RNN), F32),
            pltpu.VMEM((SUBLANES, D_RNN), F32),
        ],
        compiler_params=_params("arbitrary"),
        name="rnn_bidir",
    )(xr, xr, xr, xr, xr, xr, conv_w, conv_b, wg, bg, lam, h0)


def _attn_kernel(*refs, nb, band, n_ctx):
    if band:
        sink_ref, q_ref, kvx_ref, kvp_ref, kvc_ref, kvn_ref, o_ref = refs
        kv = jnp.concatenate([kvx_ref[...], kvp_ref[...], kvc_ref[...], kvn_ref[...]], axis=0)
    else:
        sink_ref, q_ref, kvx_ref, o_ref = refs
        kv = kvx_ref[...]
    i = pl.program_id(0)
    nk = kv.shape[0]
    lane = lax.broadcasted_iota(jnp.int32, (nk, LANES), 1)
    low = lane < HEAD_DIM
    zero = jnp.zeros((), BF16)
    k, ks, v, vs = (kv[:, j * LANES:(j + 1) * LANES] for j in range(4))
    k_low = (jnp.where(low, k, zero), jnp.where(low, ks, zero))
    k_high = (jnp.where(low, zero, ks), jnp.where(low, zero, k))
    v_low = (jnp.where(low, v, zero), jnp.where(low, vs, zero))
    v_high = (jnp.where(low, zero, vs), jnp.where(low, zero, v))

    rows2 = 2 * BLOCK_Q
    r = lax.broadcasted_iota(jnp.int32, (rows2, nk), 0) & (BLOCK_Q - 1)
    if band:
        jb = lax.broadcasted_iota(jnp.int32, (rows2, nk), 1) - n_ctx
        valid = (jb < 0) | ((jb >= r) & (jb <= r + 2 * WINDOW)
                            & ((jb >= BLOCK_Q) | (i > 0)) & ((jb < 2 * BLOCK_Q) | (i < nb - 1)))
    upper = lax.broadcasted_iota(jnp.int32, (rows2, 1), 0) >= BLOCK_Q

    def probs(q2, kmat, sink_a, sink_b):
        s = lax.dot_general(q2, kmat, (((1,), (1,)), ((), ())), preferred_element_type=F32)
        if band:
            s = jnp.where(valid, s, NEG_INF)
        sink = jnp.where(upper, sink_b, sink_a)
        m = jnp.maximum(jnp.max(s, axis=-1, keepdims=True), sink)
        p = jnp.exp(s - m)
        den = jnp.sum(p, axis=-1, keepdims=True) + jnp.exp(sink - m)
        return p.astype(BF16), 1.0 / den

    for hk in range(N_KV_HEADS):
        c0, c1 = 2 * hk, 2 * hk + 1
        q2 = jnp.concatenate([q_ref[:, c0 * LANES:(c0 + 1) * LANES],
                              q_ref[:, c1 * LANES:(c1 + 1) * LANES]], axis=0)
        h0 = hk * HEADS_PER_KV
        p_low, inv_low = probs(q2, k_low[hk], sink_ref[h0], sink_ref[h0 + 2])
        p_high, inv_high = probs(q2, k_high[hk], sink_ref[h0 + 1], sink_ref[h0 + 3])
        o2 = (jnp.dot(p_low, v_low[hk], preferred_element_type=F32) * inv_low
              + jnp.dot(p_high, v_high[hk], preferred_element_type=F32) * inv_high)
        o_ref[:, c0 * LANES:(c0 + 1) * LANES] = o2[:BLOCK_Q].astype(o_ref.dtype)
        o_ref[:, c1 * LANES:(c1 + 1) * LANES] = o2[BLOCK_Q:].astype(o_ref.dtype)


def attention(q, kv, kv_ctx, sink, *, band):
    n = q.shape[0]
    nb = n // BLOCK_Q
    n_ctx = kv_ctx.shape[0]
    in_specs = [
        pl.BlockSpec(memory_space=pltpu.SMEM),
        pl.BlockSpec((BLOCK_Q, D_ATTN), lambda i: (i, 0)),
        pl.BlockSpec((n_ctx, 4 * LANES), lambda i: (0, 0)),
    ]
    args = [sink, q, kv_ctx]
    if band:
        blk = lambda f: pl.BlockSpec((BLOCK_Q, 4 * LANES), lambda i: (f(i), 0))
        in_specs += [blk(lambda i: jnp.maximum(i - 1, 0)), blk(lambda i: i),
                     blk(lambda i: jnp.minimum(i + 1, nb - 1))]
        args += [kv, kv, kv]
    return pl.pallas_call(
        functools.partial(_attn_kernel, nb=nb, band=band, n_ctx=n_ctx),
        grid=(nb,),
        in_specs=in_specs,
        out_specs=pl.BlockSpec((BLOCK_Q, D_ATTN), lambda i: (i, 0)),
        out_shape=jax.ShapeDtypeStruct((n, D_ATTN), BF16),
        compiler_params=_params("arbitrary"),
        name="attention",
    )(*args)


def _merge_kernel(*refs, row, moe):
    if moe == "comb":
        (x_ref, hf_ref, hb_ref, gr_ref, o_ref, mod_ref, gg_ref, wo_ref, gf_ref, wrh_ref, wrl_ref,
         xo_ref, h2_ref, comb_ref) = refs
    elif moe == "route":
        (x_ref, hf_ref, hb_ref, gr_ref, o_ref, mod_ref, gg_ref, wo_ref, gf_ref, wrh_ref, wrl_ref,
         xo_ref, h2_ref, mi_ref, mf_ref, cnt_ref, run_ref) = refs
    else:
        x_ref, hf_ref, hb_ref, gr_ref, o_ref, mod_ref, gg_ref, wo_ref, gf_ref, xo_ref, h2_ref = refs
    mod = lambda off: mod_ref[0, row:row + 1, off:off + D_MODEL]
    y_rnn = (hf_ref[...] + hb_ref[...]) * jax.nn.gelu(gr_ref[...].astype(F32))
    y = jnp.concatenate([_rms(y_rnn) * gg_ref[:, :D_RNN],
                         _rms(o_ref[...].astype(F32)) * gg_ref[:, D_RNN:]], axis=1)
    out = jnp.dot(y.astype(BF16), wo_ref[0], preferred_element_type=F32)
    x = x_ref[...] + mod(GT_M) * out
    xo_ref[...] = x
    h2 = (_rms(x) * gf_ref[...]) * (1.0 + mod(SC_F)) + mod(SH_F)
    h2_ref[...] = h2.astype(h2_ref.dtype)
    if moe:
        hi = h2.astype(BF16)
        lo = (h2 - hi.astype(F32)).astype(BF16)
        dot = lambda a, b: jnp.dot(a, b, preferred_element_type=F32)
        logits = dot(hi, wrh_ref[0]) + (dot(hi, wrl_ref[0]) + dot(lo, wrh_ref[0]))
        lane = lax.broadcasted_iota(jnp.int32, logits.shape, 1)
        logits = jnp.where(lane < N_EXPERTS, logits, -jnp.inf)
        v1 = jnp.max(logits, axis=-1, keepdims=True)
        i1 = jnp.min(jnp.where(logits == v1, lane, LANES), axis=-1, keepdims=True)
        rest = jnp.where(lane == i1, -jnp.inf, logits)
        v2 = jnp.max(rest, axis=-1, keepdims=True)
        i2 = jnp.min(jnp.where(rest == v2, lane, LANES), axis=-1, keepdims=True)
        e2 = jnp.exp(v2 - v1)
        w1 = 1.0 / (1.0 + e2)
        w2 = e2 / (1.0 + e2)
        if moe == "comb":
            comb_ref[...] = jnp.where(lane == i1, w1, 0.0) + jnp.where(lane == i2, w2, 0.0)
        else:
            tm = logits.shape[0]

            @pl.when(pl.program_id(0) == 0)
            def _():
                run_ref[...] = jnp.zeros_like(run_ref)

            chosen = (lane == i1) | (lane == i2)
            before = (lax.broadcasted_iota(jnp.int32, (tm, tm), 0)
                      > lax.broadcasted_iota(jnp.int32, (tm, tm), 1))
            prefix = dot(jnp.where(before, 1.0, 0.0).astype(BF16), jnp.where(chosen, 1.0, 0.0).astype(BF16))
            rank = prefix + run_ref[...]
            r1 = jnp.sum(jnp.where(lane == i1, rank, 0.0), axis=-1, keepdims=True).astype(jnp.int32)
            r2 = jnp.sum(jnp.where(lane == i2, rank, 0.0), axis=-1, keepdims=True).astype(jnp.int32)
            run_ref[...] += jnp.sum(jnp.where(chosen, 1.0, 0.0), axis=0, keepdims=True)
            meta_i = jnp.where(lane == 0, i1, jnp.where(lane == 1, i2,
                               jnp.where(lane == 2, r1, jnp.where(lane == 3, r2, 0))))
            mi_ref[...] = meta_i[:, :META_W]
            mf_ref[...] = jnp.where(lane == 0, w1, jnp.where(lane == 1, w2, 0.0))[:, :META_W]
            cnt_ref[...] = jnp.broadcast_to(run_ref[...], cnt_ref.shape).astype(jnp.int32)


META_W = 8


def merge(x, hf, hb, gr, o, mod, g_grp, w_out, g_ffn, router, *, layer, moe_layer, row, tm, moe):
    n = x.shape[0]
    tile = lambda w: pl.BlockSpec((tm, w), lambda i: (i, 0))
    whole = lambda a: pl.BlockSpec(a.shape, lambda i: (0,) * a.ndim)
    in_specs = [tile(D_MODEL), tile(D_RNN), tile(D_RNN), tile(D_RNN), tile(D_ATTN),
                whole(mod), whole(g_grp),
                pl.BlockSpec((1, D_MODEL, D_MODEL), lambda i: (layer, 0, 0)), whole(g_ffn)]
    args = [x, hf, hb, gr, o, mod, g_grp, w_out, g_ffn]
    out_specs = [tile(D_MODEL), tile(D_MODEL)]
    h2_dtype = F32 if moe == "route" else BF16
    out_shape = [jax.ShapeDtypeStruct((n, D_MODEL), F32), jax.ShapeDtypeStruct((n, D_MODEL), h2_dtype)]
    scratch = []
    if moe:
        in_specs += [pl.BlockSpec((1, D_MODEL, LANES), lambda i: (moe_layer, 0, 0))] * 2
        args += list(router)
    if moe == "comb":
        out_specs.append(tile(LANES))
        out_shape.append(jax.ShapeDtypeStruct((n, LANES), F32))
    elif moe == "route":
        out_specs += [tile(META_W), tile(META_W), pl.BlockSpec((SUBLANES, LANES), lambda i: (0, 0))]
        out_shape += [jax.ShapeDtypeStruct((n, META_W), jnp.int32), jax.ShapeDtypeStruct((n, META_W), F32),
                      jax.ShapeDtypeStruct((SUBLANES, LANES), jnp.int32)]
        scratch = [pltpu.VMEM((1, LANES), F32)]
    return pl.pallas_call(
        functools.partial(_merge_kernel, row=row, moe=moe),
        grid=(n // tm,),
        in_specs=in_specs,
        out_specs=out_specs,
        out_shape=out_shape,
        scratch_shapes=scratch,
        compiler_params=_params("arbitrary"),
        name="merge",
    )(*args)


def _ffn_kernel(*refs, row, n_exp, nf, moe, final):
    refs = list(refs)
    h_ref, x_ref, mod_ref = refs[:3]
    pos = 3
    comb_ref = None
    if moe:
        comb_ref = refs[pos]
        pos += 1
    gfin_ref = None
    if final:
        gfin_ref = refs[pos]
        pos += 1
    wg_ref, wu_ref, wd_ref, o_ref, acc_ref = refs[pos:pos + 5]
    tot_ref = refs[pos + 5] if moe else acc_ref
    e = pl.program_id(1)
    f = pl.program_id(2)

    @pl.when(f == 0)
    def _():
        acc_ref[...] = jnp.zeros_like(acc_ref)

    if moe:
        @pl.when((e == 0) & (f == 0))
        def _():
            tot_ref[...] = jnp.zeros_like(tot_ref)

    h = h_ref[...]
    a = jnp.dot(h, wg_ref[0], preferred_element_type=F32)
    b = jnp.dot(h, wu_ref[0], preferred_element_type=F32)
    act = (a * jax.nn.sigmoid(a)) * b
    acc_ref[...] += jnp.dot(act.astype(BF16), wd_ref[0], preferred_element_type=F32)

    if moe:
        @pl.when(f == nf - 1)
        def _():
            comb = comb_ref[...]
            lane = lax.broadcasted_iota(jnp.int32, comb.shape, 1)
            ce = jnp.sum(jnp.where(lane == e, comb, 0.0), axis=-1, keepdims=True)
            tot_ref[...] += ce * acc_ref[...]

    @pl.when((e == n_exp - 1) & (f == nf - 1))
    def _():
        x = x_ref[...] + mod_ref[0, row:row + 1, GT_F:GT_F + D_MODEL] * tot_ref[...]
        if final:
            x = _rms(x) * gfin_ref[...]
        o_ref[...] = x


def ffn(h2, x, mod, comb, g_final, wg, wu, wd, *, first, n_exp, row, tm, tf):
    n = x.shape[0]
    d_ff = wg.shape[2]
    nf = d_ff // tf
    moe = comb is not None
    final = g_final is not None
    tile = lambda w: pl.BlockSpec((tm, w), lambda t, e, f: (t, 0))
    in_specs = [tile(D_MODEL), tile(D_MODEL), pl.BlockSpec(mod.shape, lambda t, e, f: (0, 0, 0))]
    args = [h2, x, mod]
    if moe:
        in_specs.append(tile(LANES))
        args.append(comb)
    if final:
        in_specs.append(pl.BlockSpec((1, D_MODEL), lambda t, e, f: (0, 0)))
        args.append(g_final)
    in_specs += [
        pl.BlockSpec((1, D_MODEL, tf), lambda t, e, f: (first + e, 0, f)),
        pl.BlockSpec((1, D_MODEL, tf), lambda t, e, f: (first + e, 0, f)),
        pl.BlockSpec((1, tf, D_MODEL), lambda t, e, f: (first + e, f, 0)),
    ]
    args += [wg, wu, wd]
    scratch = [pltpu.VMEM((tm, D_MODEL), F32)]
    if moe:
        scratch.append(pltpu.VMEM((tm, D_MODEL), F32))
    return pl.pallas_call(
        functools.partial(_ffn_kernel, row=row, n_exp=n_exp, nf=nf, moe=moe, final=final),
        grid=(n // tm, n_exp, nf),
        in_specs=in_specs,
        out_specs=tile(D_MODEL),
        out_shape=jax.ShapeDtypeStruct((n, D_MODEL), F32),
        scratch_shapes=scratch,
        compiler_params=_params("arbitrary", "arbitrary", "arbitrary"),
        name="ffn",
    )(*args)


MOE_TILE = 512


def _route_tables(cnt, meta_i, n_pad):
    i32 = jnp.int32
    counts = cnt[0, :N_EXPERTS]
    tiles = (counts + MOE_TILE - 1) // MOE_TILE
    cum = jnp.cumsum(tiles)
    starts = (cum - tiles) * MOE_TILE
    experts = jnp.arange(N_EXPERTS, dtype=i32)
    start_of = lambda e: jnp.sum(jnp.where(e[:, None] == experts[None, :], starts[None, :], 0), axis=1)
    pos1 = (meta_i[:, 2] + start_of(meta_i[:, 0])).astype(i32)
    pos2 = (meta_i[:, 3] + start_of(meta_i[:, 1])).astype(i32)
    n_tiles = cum[-1]
    t = jnp.arange(n_pad // MOE_TILE, dtype=i32)
    te = jnp.sum((t[:, None] >= cum[None, :]).astype(i32), axis=1)
    used = t < n_tiles
    tile_expert = jnp.where(used, te, jnp.max(jnp.where(used, te, 0))).astype(i32)
    zero_start = jnp.concatenate([starts + counts, (n_tiles * MOE_TILE)[None]]).astype(i32)
    zero_len = jnp.concatenate([tiles * MOE_TILE - counts, (n_pad - n_tiles * MOE_TILE)[None]]).astype(i32)
    return pos1, pos2, tile_expert, n_tiles.reshape(1).astype(i32), zero_start, zero_len


def _dispatch_kernel(pos1_ref, pos2_ref, zs_ref, zl_ref, h_ref, x_hbm, zero_ref, sem, *, tm):
    i = pl.program_id(0)

    def row_copy(src, r, p, s):
        return pltpu.make_async_copy(src.at[pl.ds(r, 1)], x_hbm.at[pl.ds(p, 1)], sem.at[s])

    @pl.when(i == 0)
    def _():
        zero_ref[...] = jnp.zeros_like(zero_ref)
        for k in range(N_EXPERTS + 1):
            def issue(j, c):
                row_copy(zero_ref, 0, zs_ref[k] + j, 2).start()
                return c
            lax.fori_loop(0, zl_ref[k], issue, 0)
        for k in range(N_EXPERTS + 1):
            def drain(j, c):
                row_copy(zero_ref, 0, zs_ref[k] + j, 2).wait()
                return c
            lax.fori_loop(0, zl_ref[k], drain, 0)

    base = i * tm

    def issue(g, c):
        r0 = pl.multiple_of(g * SUBLANES, SUBLANES)
        for s in range(SUBLANES):
            row_copy(h_ref, r0 + s, pos1_ref[base + r0 + s], 0).start()
            row_copy(h_ref, r0 + s, pos2_ref[base + r0 + s], 1).start(priority=1)
        return c

    lax.fori_loop(0, tm // SUBLANES, issue, 0)

    def drain(g, c):
        r0 = pl.multiple_of(g * SUBLANES, SUBLANES)
        for s in range(SUBLANES):
            row_copy(h_ref, r0 + s, pos1_ref[base + r0 + s], 0).wait()
            row_copy(h_ref, r0 + s, pos2_ref[base + r0 + s], 1).wait()
        return c

    lax.fori_loop(0, tm // SUBLANES, drain, 0)


def moe_dispatch(pos1, pos2, zero_start, zero_len, h2, n_pad, *, tm):
    n = h2.shape[0]
    return pl.pallas_call(
        functools.partial(_dispatch_kernel, tm=tm),
        grid_spec=pltpu.PrefetchScalarGridSpec(
            num_scalar_prefetch=4,
            grid=(n // tm,),
            in_specs=[pl.BlockSpec((tm, D_MODEL), lambda i, *_: (i, 0))],
            out_specs=pl.BlockSpec(memory_space=pl.ANY),
            scratch_shapes=[pltpu.VMEM((SUBLANES, D_MODEL), F32), pltpu.SemaphoreType.DMA((3,))],
        ),
        out_shape=jax.ShapeDtypeStruct((n_pad, D_MODEL), F32),
        compiler_params=_params("arbitrary"),
        name="moe_dispatch",
    )(pos1, pos2, zero_start, zero_len, h2)


def _grouped_ffn_kernel(te_ref, nt_ref, x_ref, wg_ref, wu_ref, wd_ref, y_ref):
    t = pl.program_id(0)

    @pl.when(t < nt_ref[0])
    def _():
        x = x_ref[...].astype(BF16)
        a = jnp.dot(x, wg_ref[0], preferred_element_type=F32)
        b = jnp.dot(x, wu_ref[0], preferred_element_type=F32)
        act = (a * jax.nn.sigmoid(a)) * b
        y_ref[...] = jnp.dot(act.astype(BF16), wd_ref[0], preferred_element_type=F32)

    @pl.when(t >= nt_ref[0])
    def _():
        y_ref[...] = jnp.zeros_like(y_ref)


def grouped_ffn(tile_expert, n_tiles, xs, wg, wu, wd, *, first):
    n_pad = xs.shape[0]
    d_ff = wg.shape[2]
    rows = pl.BlockSpec((MOE_TILE, D_MODEL), lambda t, te, nt: (t, 0))
    return pl.pallas_call(
        _grouped_ffn_kernel,
        grid_spec=pltpu.PrefetchScalarGridSpec(
            num_scalar_prefetch=2,
            grid=(n_pad // MOE_TILE,),
            in_specs=[rows,
                      pl.BlockSpec((1, D_MODEL, d_ff), lambda t, te, nt: (first + te[t], 0, 0)),
                      pl.BlockSpec((1, D_MODEL, d_ff), lambda t, te, nt: (first + te[t], 0, 0)),
                      pl.BlockSpec((1, d_ff, D_MODEL), lambda t, te, nt: (first + te[t], 0, 0))],
            out_specs=rows,
        ),
        out_shape=jax.ShapeDtypeStruct((n_pad, D_MODEL), F32),
        compiler_params=_params("arbitrary"),
        name="grouped_ffn",
    )(tile_expert, n_tiles, xs, wg, wu, wd)


def _combine_kernel(*refs, row, tm, final):
    if final:
        pos1_ref, pos2_ref, x_ref, mf_ref, mod_ref, gfin_ref, y_hbm, o_ref, b1_ref, b2_ref, sem = refs
    else:
        pos1_ref, pos2_ref, x_ref, mf_ref, mod_ref, y_hbm, o_ref, b1_ref, b2_ref, sem = refs
    base = pl.program_id(0) * tm

    def row_copy(p, dst, r, s):
        return pltpu.make_async_copy(y_hbm.at[pl.ds(p, 1)], dst.at[pl.ds(r, 1)], sem.at[s])

    def issue(g, c):
        r0 = pl.multiple_of(g * SUBLANES, SUBLANES)
        for s in range(SUBLANES):
            row_copy(pos1_ref[base + r0 + s], b1_ref, r0 + s, 0).start()
            row_copy(pos2_ref[base + r0 + s], b2_ref, r0 + s, 1).start(priority=1)
        return c

    lax.fori_loop(0, tm // SUBLANES, issue, 0)

    def drain(g, c):
        r0 = pl.multiple_of(g * SUBLANES, SUBLANES)
        for s in range(SUBLANES):
            row_copy(pos1_ref[base + r0 + s], b1_ref, r0 + s, 0).wait()
            row_copy(pos2_ref[base + r0 + s], b2_ref, r0 + s, 1).wait()
        return c

    lax.fori_loop(0, tm // SUBLANES, drain, 0)

    tot = mf_ref[:, 0:1] * b1_ref[...] + mf_ref[:, 1:2] * b2_ref[...]
    x = x_ref[...] + mod_ref[0, row:row + 1, GT_F:GT_F + D_MODEL] * tot
    if final:
        x = _rms(x) * gfin_ref[...]
    o_ref[...] = x


def moe_combine(pos1, pos2, x, meta_f, mod, g_final, ys, *, row, tm):
    n = x.shape[0]
    final = g_final is not None
    tile = lambda w: pl.BlockSpec((tm, w), lambda i, *_: (i, 0))
    in_specs = [tile(D_MODEL), tile(META_W), pl.BlockSpec(mod.shape, lambda i, *_: (0, 0, 0))]
    args = [x, meta_f, mod]
    if final:
        in_specs.append(pl.BlockSpec((1, D_MODEL), lambda i, *_: (0, 0)))
        args.append(g_final)
    in_specs.append(pl.BlockSpec(memory_space=pl.ANY))
    args.append(ys)
    return pl.pallas_call(
        functools.partial(_combine_kernel, row=row, tm=tm, final=final),
        grid_spec=pltpu.PrefetchScalarGridSpec(
            num_scalar_prefetch=2,
            grid=(n // tm,),
            in_specs=in_specs,
            out_specs=tile(D_MODEL),
            scratch_shapes=[pltpu.VMEM((tm, D_MODEL), F32), pltpu.VMEM((tm, D_MODEL), F32),
                            pltpu.SemaphoreType.DMA((2,))],
        ),
        out_shape=jax.ShapeDtypeStruct((n, D_MODEL), F32),
        compiler_params=_params("arbitrary"),
        name="moe_combine",
    )(pos1, pos2, *args)


def _gate_weights(w_rg):
    depth = w_rg.shape[0]
    per = RNN_CHUNK // RNN_BLOCK
    nch = D_RNN // RNN_CHUNK
    wr = w_rg.astype(BF16).reshape(depth, 2, 2, nch, per, RNN_BLOCK, 1, RNN_BLOCK)
    on_diag = jnp.eye(per, dtype=bool).reshape(per, 1, per, 1)
    bd = jnp.where(on_diag, wr, jnp.zeros((), BF16))
    bd = bd.reshape(depth, 2, 2, nch, RNN_CHUNK, RNN_CHUNK)
    return jnp.concatenate([bd[:, :, 0], bd[:, :, 1]], axis=-1)


def _tile_rows(n, want):
    return want if n % want == 0 else n


def kernel(x, c, ctx, c_ctx, w_mod, b_mod, g_mix, g_ffn, w_in, conv_w, conv_b, w_rg, b_rg, lam,
           sink, g_grp, w_out, w_ffn_gate, w_ffn_up, w_ffn_down, w_router, w_exp_gate, w_exp_up,
           w_exp_down, g_final):
    assert x.shape[0] == 1 and ctx.shape[0] == 1
    depth = w_mod.shape[0]
    seq, n_ctx = x.shape[1], ctx.shape[1]
    xl, xc = x[0], ctx[0]

    cond8 = jnp.zeros((SUBLANES, D_MODEL), F32).at[0].set(c[0]).at[1].set(c_ctx)
    mods = adaln_all(cond8, w_mod, b_mod)
    cos, sin = rope_tables(seq)

    w_in_b = w_in.astype(BF16)
    w_out_b = w_out.astype(BF16)
    wg_gate = _gate_weights(w_rg)
    ffn_w = (w_ffn_gate.astype(BF16), w_ffn_up.astype(BF16), w_ffn_down.astype(BF16))
    exp_w = tuple(w.astype(BF16).reshape((-1,) + w.shape[2:]) for w in (w_exp_gate, w_exp_up, w_exp_down))
    wr_pad = jnp.pad(w_router, ((0, 0), (0, 0), (0, LANES - N_EXPERTS)))
    wr_hi = wr_pad.astype(BF16)
    wr_lo = (wr_pad - wr_hi.astype(F32)).astype(BF16)
    zeros_h0 = jnp.zeros((SUBLANES, D_RNN), F32)

    tm = _tile_rows(seq, 512)
    tm_ffn = _tile_rows(seq, 1024)

    for l in range(depth):
        last = l == depth - 1
        moe = l % 2 == 1
        j = l // 2
        mod = mods[l:l + 1]
        gm, gf, gg = g_mix[l:l + 1], g_ffn[l:l + 1], g_grp[l:l + 1]

        xr_c, gr_c, q_c, kv_c = in_proj(xc, mod, gm, w_in_b, None, None, layer=l, row=1, tm=n_ctx)
        xr_l, gr_l, q_l, kv_l = in_proj(xl, mod, gm, w_in_b, cos, sin, layer=l, row=0, tm=tm)

        rnn_args = (conv_w[l], conv_b[l:l + 1], wg_gate, b_rg[l], lam[l])
        hf_c, hb_c, h_ctx_end = rnn_bidir(xr_c, *rnn_args, zeros_h0, layer=l, tm=n_ctx)
        hf_l, hb_l, _ = rnn_bidir(xr_l, *rnn_args, h_ctx_end, layer=l, tm=tm)

        o_l = attention(q_l, kv_l, kv_c, sink[l], band=True)

        router = (wr_hi, wr_lo) if moe else None
        weights = exp_w if moe else ffn_w
        first = j * N_EXPERTS if moe else j
        tf = 512
        merge_l = functools.partial(merge, layer=l, moe_layer=j)

        g_fin = g_final.reshape(1, D_MODEL) if last else None
        if moe:
            xl, h2, meta_i, meta_f, cnt = merge_l(xl, hf_l, hb_l, gr_l, o_l, mod, gg, w_out_b, gf, router,
                                                  row=0, tm=tm, moe="route")
            n_pad = 2 * seq + N_EXPERTS * MOE_TILE
            pos1, pos2, tile_expert, n_tiles, zero_start, zero_len = _route_tables(cnt, meta_i, n_pad)
            xs = moe_dispatch(pos1, pos2, zero_start, zero_len, h2, n_pad, tm=tm)
            ys = grouped_ffn(tile_expert, n_tiles, xs, *weights, first=first)
            xl = moe_combine(pos1, pos2, xl, meta_f, mod, g_fin, ys, row=0, tm=tm)
        else:
            xl, h2 = merge_l(xl, hf_l, hb_l, gr_l, o_l, mod, gg, w_out_b, gf, None, row=0, tm=tm, moe=None)
            xl = ffn(h2, xl, mod, None, g_fin, *weights, first=first, n_exp=1, row=0, tm=tm_ffn, tf=tf)

        if not last:
            o_c = attention(q_c, kv_c, kv_c, sink[l], band=False)
            res = merge_l(xc, hf_c, hb_c, gr_c, o_c, mod, gg, w_out_b, gf, router, row=1, tm=n_ctx,
                          moe="comb" if moe else None)
            xc, h2c = res[0], res[1]
            comb_c = res[2] if moe else None
            xc = ffn(h2c, xc, mod, comb_c, None, *weights, first=first, n_exp=N_EXPERTS if moe else 1,
                     row=1, tm=n_ctx, tf=tf)

    return xl[None]
```

```python
import functools

import jax
import jax.numpy as jnp
from jax import lax
from jax.experimental import pallas as pl
from jax.experimental.pallas import tpu as pltpu

F32 = jnp.float32
BF16 = jnp.bfloat16

D_MODEL = 1024
D_RNN = 512
D_ATTN = 512
D_KV = 128
HEAD_DIM = 64
N_HEADS = 8
N_KV_HEADS = 2
HEADS_PER_KV = N_HEADS // N_KV_HEADS
D_IN = 2 * D_RNN + D_ATTN + 2 * D_KV
RNN_BLOCK = 64
RNN_CHUNK = 256
CONV_W = 4
LRU_C = 8.0
WINDOW = 128
BLOCK_Q = 128
GRID_W = 64
ROPE_BASE = 10000.0
N_EXPERTS = 8
EPS = 1e-6
NEG_INF = -1e30

LANES = 128
SUBLANES = 8
VMEM_LIMIT = 56 * 1024 * 1024

SH_M, SC_M, GT_M, SH_F, SC_F, GT_F = (k * D_MODEL for k in range(6))


def _params(*sem):
    return pltpu.CompilerParams(dimension_semantics=sem, vmem_limit_bytes=VMEM_LIMIT)


def _rms(x):
    return x * lax.rsqrt(jnp.mean(x * x, axis=-1, keepdims=True) + EPS)


def _adaln_kernel(c_ref, w_ref, b_ref, o_ref):
    c = c_ref[...]
    s = c * jax.nn.sigmoid(c)
    w = w_ref[0]
    s_hi, w_hi = s.astype(BF16), w.astype(BF16)
    s_lo = (s - s_hi.astype(F32)).astype(BF16)
    w_lo = (w - w_hi.astype(F32)).astype(BF16)
    dot = lambda a, b: jnp.dot(a, b, preferred_element_type=F32)
    o_ref[0] = dot(s_hi, w_hi) + (dot(s_hi, w_lo) + dot(s_lo, w_hi)) + b_ref[0]


def adaln_all(cond8, w_mod, b_mod):
    depth = w_mod.shape[0]
    nc = 1536
    return pl.pallas_call(
        _adaln_kernel,
        grid=(depth, 6 * D_MODEL // nc),
        in_specs=[
            pl.BlockSpec((SUBLANES, D_MODEL), lambda l, j: (0, 0)),
            pl.BlockSpec((1, D_MODEL, nc), lambda l, j: (l, 0, j)),
            pl.BlockSpec((1, 1, nc), lambda l, j: (l, 0, j)),
        ],
        out_specs=pl.BlockSpec((1, SUBLANES, nc), lambda l, j: (l, 0, j)),
        out_shape=jax.ShapeDtypeStruct((depth, SUBLANES, 6 * D_MODEL), F32),
        compiler_params=_params("arbitrary", "arbitrary"),
        name="adaln",
    )(cond8, w_mod, b_mod.reshape(depth, 1, 6 * D_MODEL))


ROPE_ROWS = SUBLANES * GRID_W


def _rope_kernel(f_ref, c_ref, s_ref):
    i = pl.program_id(0)
    lane = lax.broadcasted_iota(jnp.int32, (GRID_W, LANES), 1)
    col_axis = ((lane >> 5) & 1) == 1
    sign = jnp.where(((lane >> 4) & 1) == 0, -1.0, 1.0).astype(F32)
    freq = f_ref[...]
    rowpos = (i * SUBLANES + lax.broadcasted_iota(jnp.int32, (SUBLANES, LANES), 0)).astype(F32)
    colpos = lax.broadcasted_iota(jnp.int32, (GRID_W, LANES), 0).astype(F32)
    ang_r = rowpos * freq
    ang_c = colpos * freq
    cr, sr = jnp.cos(ang_r), jnp.sin(ang_r)
    cc, sc = jnp.cos(ang_c), jnp.sin(ang_c) * sign
    for g in range(SUBLANES):
        rows = slice(g * GRID_W, (g + 1) * GRID_W)
        c_ref[rows, :] = jnp.where(col_axis, cc, jnp.broadcast_to(cr[g:g + 1, :], (GRID_W, LANES)))
        s_ref[rows, :] = jnp.where(col_axis, sc, jnp.broadcast_to(sr[g:g + 1, :], (GRID_W, LANES)) * sign)


def rope_tables(seq):
    axis_dim = HEAD_DIM // 2
    freqs = ROPE_BASE ** (-jnp.arange(0, axis_dim, 2, dtype=F32) / axis_dim)
    freq_lane = jnp.tile(freqs, LANES // freqs.shape[0]).reshape(1, LANES)
    return pl.pallas_call(
        _rope_kernel,
        grid=(seq // ROPE_ROWS,),
        in_specs=[pl.BlockSpec((1, LANES), lambda i: (0, 0))],
        out_specs=[pl.BlockSpec((ROPE_ROWS, LANES), lambda i: (i, 0))] * 2,
        out_shape=[jax.ShapeDtypeStruct((seq, LANES), F32)] * 2,
        compiler_params=_params("arbitrary"),
        name="rope_tables",
    )(freq_lane)


def _rope_chunk(x, cos, sin):
    lane = lax.broadcasted_iota(jnp.int32, x.shape, 1)
    first_half = ((lane >> 4) & 1) == 0
    partner = jnp.where(first_half, pltpu.roll(x, LANES - 16, axis=1), pltpu.roll(x, 16, axis=1))
    return x * cos + partner * sin


def _in_proj_kernel(*refs, row, rope):
    if rope:
        x_ref, mod_ref, g_ref, w_ref, cos_ref, sin_ref, xr_ref, gr_ref, q_ref, kv_ref = refs
    else:
        x_ref, mod_ref, g_ref, w_ref, xr_ref, gr_ref, q_ref, kv_ref = refs
    x = x_ref[...]
    shift = mod_ref[0, row:row + 1, SH_M:SH_M + D_MODEL]
    scale = mod_ref[0, row:row + 1, SC_M:SC_M + D_MODEL]
    h = (_rms(x) * g_ref[...]) * (1.0 + scale) + shift
    p = jnp.dot(h.astype(BF16), w_ref[0], preferred_element_type=F32)
    xr_ref[...] = p[:, 0:D_RNN]
    gr_ref[...] = p[:, D_RNN:2 * D_RNN].astype(gr_ref.dtype)
    q0 = 2 * D_RNN
    if rope:
        cos, sin = cos_ref[...], sin_ref[...]
    for c in range(D_ATTN // LANES):
        qc = p[:, q0 + c * LANES:q0 + (c + 1) * LANES]
        if rope:
            qc = _rope_chunk(qc, cos, sin)
        q_ref[:, c * LANES:(c + 1) * LANES] = (qc * (HEAD_DIM ** -0.5)).astype(BF16)
    k = p[:, q0 + D_ATTN:q0 + D_ATTN + D_KV]
    v = p[:, q0 + D_ATTN + D_KV:q0 + D_ATTN + 2 * D_KV]
    if rope:
        k = _rope_chunk(k, cos, sin)
    kv_ref[:, 0:LANES] = k.astype(BF16)
    kv_ref[:, LANES:2 * LANES] = pltpu.roll(k, HEAD_DIM, axis=1).astype(BF16)
    kv_ref[:, 2 * LANES:3 * LANES] = v.astype(BF16)
    kv_ref[:, 3 * LANES:4 * LANES] = pltpu.roll(v, HEAD_DIM, axis=1).astype(BF16)


def in_proj(x, mod, g_mix, w_in, cos, sin, *, layer, row, tm):
    n = x.shape[0]
    rope = cos is not None
    tile = lambda w: pl.BlockSpec((tm, w), lambda i: (i, 0))
    in_specs = [
        tile(D_MODEL),
        pl.BlockSpec((1, SUBLANES, 6 * D_MODEL), lambda i: (0, 0, 0)),
        pl.BlockSpec((1, D_MODEL), lambda i: (0, 0)),
        pl.BlockSpec((1, D_MODEL, D_IN), lambda i: (layer, 0, 0)),
    ]
    args = [x, mod, g_mix, w_in]
    if rope:
        in_specs += [tile(LANES), tile(LANES)]
        args += [cos, sin]
    return pl.pallas_call(
        functools.partial(_in_proj_kernel, row=row, rope=rope),
        grid=(n // tm,),
        in_specs=in_specs,
        out_specs=[tile(D_RNN), tile(D_RNN), tile(D_ATTN), tile(4 * LANES)],
        out_shape=[
            jax.ShapeDtypeStruct((n, D_RNN), F32),
            jax.ShapeDtypeStruct((n, D_RNN), BF16),
            jax.ShapeDtypeStruct((n, D_ATTN), BF16),
            jax.ShapeDtypeStruct((n, 4 * LANES), BF16),
        ],
        compiler_params=_params("arbitrary"),
        name="in_proj",
    )(*args)


RNN_SLABS = D_RNN // LANES
SEG_PAD = 4


def _rnn_kernel(pf_ref, mf_ref, nf_ref, pb_ref, mb_ref, nb_ref, cw_ref, cb_ref, wg_ref, bg_ref,
                lam_ref, h0_ref, hf_ref, hb_ref, hl_ref, ext_ref, xs_ref, u_ref, a_ref, b_ref, hs_ref,
                cf_ref, cbk_ref, *, tm, nt):
    i = pl.program_id(0)
    seg = tm // SUBLANES
    pitch = seg + SEG_PAD
    slab = lambda c: slice(c * LANES, (c + 1) * LANES)

    @pl.when(i == 0)
    def _():
        cf_ref[...] = jnp.broadcast_to(h0_ref[0:1, :], (SUBLANES, D_RNN))
        cbk_ref[...] = jnp.broadcast_to(h0_ref[1:2, :], (SUBLANES, D_RNN))

    softplus_neg_lam = jax.nn.softplus(-lam_ref[...])
    seg_id = lax.broadcasted_iota(jnp.int32, (SUBLANES, D_RNN), 0)

    def run(d, prev_ref, main_ref, next_ref, at_start, at_end, out_ref, carry_ref):
        ext_ref[0:SUBLANES, :] = jnp.where(at_start, 0.0, prev_ref[...])
        ext_ref[SUBLANES:tm + SUBLANES, :] = main_ref[...]
        ext_ref[tm + SUBLANES:tm + 2 * SUBLANES, :] = jnp.where(at_end, 0.0, next_ref[...])
        halo = SUBLANES - CONV_W // 2
        for j in range(SUBLANES):
            for c in range(RNN_SLABS):
                xs_ref[c, j * pitch:j * pitch + seg + SUBLANES, :] = (
                    ext_ref[j * seg + halo:j * seg + halo + seg + SUBLANES, slab(c)])
        for g in range(seg):
            for c in range(RNN_SLABS):
                acc = cb_ref[:, slab(c)]
                for k in range(CONV_W):
                    acc = acc + xs_ref[c, pl.ds(g + k, SUBLANES, stride=pitch), :] * cw_ref[k:k + 1, slab(c)]
                u_ref[g * SUBLANES:(g + 1) * SUBLANES, slab(c)] = acc
        u = u_ref[...]
        ub = u.astype(BF16)
        for c in range(D_RNN // RNN_CHUNK):
            ch = slice(c * RNN_CHUNK, (c + 1) * RNN_CHUNK)
            g = jnp.dot(ub[:, ch], wg_ref[0, d, c], preferred_element_type=F32)
            r = jax.nn.sigmoid(g[:, :RNN_CHUNK] + bg_ref[d, 0:1, ch])
            ig = jax.nn.sigmoid(g[:, RNN_CHUNK:] + bg_ref[d, 1:2, ch])
            log_a = (-LRU_C * r) * softplus_neg_lam[d:d + 1, ch]
            a = jnp.exp(log_a)
            mult = jnp.sqrt(jnp.maximum(1.0 - a * a, 0.0))
            a_ref[:, ch] = a
            b_ref[:, ch] = mult * (ig * u[:, ch])

        def scan(n, hp):
            g = n if d == 0 else seg - 1 - n
            off = pl.multiple_of(g * SUBLANES, SUBLANES)
            a_g = a_ref[pl.ds(off, SUBLANES), :]
            h = a_g * hp[0] + b_ref[pl.ds(off, SUBLANES), :]
            p = a_g * hp[1]
            b_ref[pl.ds(off, SUBLANES), :] = h
            a_ref[pl.ds(off, SUBLANES), :] = p
            return h, p

        zeros = jnp.zeros((SUBLANES, D_RNN), F32)
        h_end, p_end = lax.fori_loop(0, seg, scan, (zeros, zeros + 1.0), unroll=8)

        carry = carry_ref[...]
        enter = zeros
        for n in range(SUBLANES):
            j = n if d == 0 else SUBLANES - 1 - n
            enter = jnp.where(seg_id == j, carry, enter)
            carry = jnp.broadcast_to(h_end[j:j + 1, :] + p_end[j:j + 1, :] * carry[0:1, :], (SUBLANES, D_RNN))
        carry_ref[...] = carry

        for g in range(seg):
            rows = slice(g * SUBLANES, (g + 1) * SUBLANES)
            h = b_ref[rows, :] + a_ref[rows, :] * enter
            for c in range(RNN_SLABS):
                hs_ref[c, pl.ds(g, SUBLANES, stride=pitch), :] = h[:, slab(c)]
        for j in range(SUBLANES):
            for c in range(RNN_SLABS):
                out_ref[j * seg:(j + 1) * seg, slab(c)] = (
                    hs_ref[c, j * pitch:j * pitch + seg, :].astype(out_ref.dtype))

    run(0, pf_ref, mf_ref, nf_ref, i == 0, i == nt - 1, hf_ref, cf_ref)
    run(1, pb_ref, mb_ref, nb_ref, i == nt - 1, i == 0, hb_ref, cbk_ref)

    @pl.when(i == nt - 1)
    def _():
        rows = lax.broadcasted_iota(jnp.int32, (SUBLANES, D_RNN), 0)
        hl_ref[...] = jnp.where(rows == 0, cf_ref[...], jnp.where(rows == 1, cbk_ref[...], 0.0))


def rnn_bidir(xr, conv_w, conv_b, wg, bg, lam, h0, *, layer, tm):
    n = xr.shape[0]
    nt = n // tm
    per = tm // SUBLANES
    last8 = n // SUBLANES - 1
    fwd = lambda i: i
    bwd = lambda i: nt - 1 - i
    main = lambda t: pl.BlockSpec((tm, D_RNN), lambda i: (t(i), 0))
    prev = lambda t: pl.BlockSpec((SUBLANES, D_RNN), lambda i: (jnp.maximum(t(i) * per - 1, 0), 0))
    nxt = lambda t: pl.BlockSpec((SUBLANES, D_RNN), lambda i: (jnp.minimum((t(i) + 1) * per, last8), 0))
    whole = lambda a: pl.BlockSpec(a.shape, lambda i: (0,) * a.ndim)
    return pl.pallas_call(
        functools.partial(_rnn_kernel, tm=tm, nt=nt),
        grid=(nt,),
        in_specs=[prev(fwd), main(fwd), nxt(fwd), prev(bwd), main(bwd), nxt(bwd),
                  whole(conv_w), whole(conv_b),
                  pl.BlockSpec((1,) + wg.shape[1:], lambda i: (layer, 0, 0, 0, 0)),
                  whole(bg), whole(lam), whole(h0)],
        out_specs=[main(fwd), main(bwd), pl.BlockSpec((SUBLANES, D_RNN), lambda i: (0, 0))],
        out_shape=[
            jax.ShapeDtypeStruct((n, D_RNN), BF16),
            jax.ShapeDtypeStruct((n, D_RNN), BF16),
            jax.ShapeDtypeStruct((SUBLANES, D_RNN), F32),
        ],
        scratch_shapes=[
            pltpu.VMEM((tm + 2 * SUBLANES, D_RNN), F32),
            pltpu.VMEM((RNN_SLABS, tm + SUBLANES * SEG_PAD + SUBLANES, LANES), F32),
            pltpu.VMEM((tm, D_RNN), F32),
            pltpu.VMEM((tm, D_RNN), F32),
            pltpu.VMEM((tm, D_RNN), F32),
            pltpu.VMEM((RNN_SLABS, tm + SUBLANES * SEG_PAD, LANES), F32),
            pltpu.VMEM((SUBLANES, D_RNN), F32),
            pltpu.VMEM((SUBLANES, D_RNN), F32),
        ],
        compiler_params=_params("arbitrary"),
        name="rnn_bidir",
    )(xr, xr, xr, xr, xr, xr, conv_w, conv_b, wg, bg, lam, h0)


def _attn_kernel(*refs, nb, band, n_ctx):
    if band:
        sink_ref, q_ref, kvx_ref, kvp_ref, kvc_ref, kvn_ref, o_ref = refs
        kv = jnp.concatenate([kvx_ref[...], kvp_ref[...], kvc_ref[...], kvn_ref[...]], axis=0)
    else:
        sink_ref, q_ref, kvx_ref, o_ref = refs
        kv = kvx_ref[...]
    i = pl.program_id(0)
    nk = kv.shape[0]
    lane = lax.broadcasted_iota(jnp.int32, (nk, LANES), 1)
    low = lane < HEAD_DIM
    zero = jnp.zeros((), BF16)
    k, ks, v, vs = (kv[:, j * LANES:(j + 1) * LANES] for j in range(4))
    k_low = (jnp.where(low, k, zero), jnp.where(low, ks, zero))
    k_high = (jnp.where(low, zero, ks), jnp.where(low, zero, k))
    v_low = (jnp.where(low, v, zero), jnp.where(low, vs, zero))
    v_high = (jnp.where(low, zero, vs), jnp.where(low, zero, v))

    rows2 = 2 * BLOCK_Q
    r = lax.broadcasted_iota(jnp.int32, (rows2, nk), 0) & (BLOCK_Q - 1)
    if band:
        jb = lax.broadcasted_iota(jnp.int32, (rows2, nk), 1) - n_ctx
        valid = (jb < 0) | ((jb >= r) & (jb <= r + 2 * WINDOW)
                            & ((jb >= BLOCK_Q) | (i > 0)) & ((jb < 2 * BLOCK_Q) | (i < nb - 1)))
    upper = lax.broadcasted_iota(jnp.int32, (rows2, 1), 0) >= BLOCK_Q

    def probs(q2, kmat, sink_a, sink_b):
        s = lax.dot_general(q2, kmat, (((1,), (1,)), ((), ())), preferred_element_type=F32)
        if band:
            s = jnp.where(valid, s, NEG_INF)
        sink = jnp.where(upper, sink_b, sink_a)
        m = jnp.maximum(jnp.max(s, axis=-1, keepdims=True), sink)
        p = jnp.exp(s - m)
        den = jnp.sum(p, axis=-1, keepdims=True) + jnp.exp(sink - m)
        return p.astype(BF16), 1.0 / den

    for hk in range(N_KV_HEADS):
        c0, c1 = 2 * hk, 2 * hk + 1
        q2 = jnp.concatenate([q_ref[:, c0 * LANES:(c0 + 1) * LANES],
                              q_ref[:, c1 * LANES:(c1 + 1) * LANES]], axis=0)
        h0 = hk * HEADS_PER_KV
        p_low, inv_low = probs(q2, k_low[hk], sink_ref[h0], sink_ref[h0 + 2])
        p_high, inv_high = probs(q2, k_high[hk], sink_ref[h0 + 1], sink_ref[h0 + 3])
        o2 = (jnp.dot(p_low, v_low[hk], preferred_element_type=F32) * inv_low
              + jnp.dot(p_high, v_high[hk], preferred_element_type=F32) * inv_high)
        o_ref[:, c0 * LANES:(c0 + 1) * LANES] = o2[:BLOCK_Q].astype(o_ref.dtype)
        o_ref[:, c1 * LANES:(c1 + 1) * LANES] = o2[BLOCK_Q:].astype(o_ref.dtype)


def attention(q, kv, kv_ctx, sink, *, band):
    n = q.shape[0]
    nb = n // BLOCK_Q
    n_ctx = kv_ctx.shape[0]
    in_specs = [
        pl.BlockSpec(memory_space=pltpu.SMEM),
        pl.BlockSpec((BLOCK_Q, D_ATTN), lambda i: (i, 0)),
        pl.BlockSpec((n_ctx, 4 * LANES), lambda i: (0, 0)),
    ]
    args = [sink, q, kv_ctx]
    if band:
        blk = lambda f: pl.BlockSpec((BLOCK_Q, 4 * LANES), lambda i: (f(i), 0))
        in_specs += [blk(lambda i: jnp.maximum(i - 1, 0)), blk(lambda i: i),
                     blk(lambda i: jnp.minimum(i + 1, nb - 1))]
        args += [kv, kv, kv]
    return pl.pallas_call(
        functools.partial(_attn_kernel, nb=nb, band=band, n_ctx=n_ctx),
        grid=(nb,),
        in_specs=in_specs,
        out_specs=pl.BlockSpec((BLOCK_Q, D_ATTN), lambda i: (i, 0)),
        out_shape=jax.ShapeDtypeStruct((n, D_ATTN), BF16),
        compiler_params=_params("arbitrary"),
        name="attention",
    )(*args)


def _merge_kernel(*refs, row, moe):
    if moe == "comb":
        (x_ref, hf_ref, hb_ref, gr_ref, o_ref, mod_ref, gg_ref, wo_ref, gf_ref, wrh_ref, wrl_ref,
         xo_ref, h2_ref, comb_ref) = refs
    elif moe == "route":
        (x_ref, hf_ref, hb_ref, gr_ref, o_ref, mod_ref, gg_ref, wo_ref, gf_ref, wrh_ref, wrl_ref,
         xo_ref, h2_ref, mi_ref, mf_ref, cnt_ref) = refs
    else:
        x_ref, hf_ref, hb_ref, gr_ref, o_ref, mod_ref, gg_ref, wo_ref, gf_ref, xo_ref, h2_ref = refs
    mod = lambda off: mod_ref[0, row:row + 1, off:off + D_MODEL]
    y_rnn = (hf_ref[...].astype(F32) + hb_ref[...].astype(F32)) * jax.nn.gelu(gr_ref[...].astype(F32))
    y = jnp.concatenate([_rms(y_rnn) * gg_ref[:, :D_RNN],
                         _rms(o_ref[...].astype(F32)) * gg_ref[:, D_RNN:]], axis=1)
    out = jnp.dot(y.astype(BF16), wo_ref[0], preferred_element_type=F32)
    x = x_ref[...] + mod(GT_M) * out
    xo_ref[...] = x
    h2 = (_rms(x) * gf_ref[...]) * (1.0 + mod(SC_F)) + mod(SH_F)
    h2_ref[...] = h2.astype(h2_ref.dtype)
    if moe:
        hi = h2.astype(BF16)
        lo = (h2 - hi.astype(F32)).astype(BF16)
        dot = lambda a, b: jnp.dot(a, b, preferred_element_type=F32)
        logits = dot(hi, wrh_ref[0]) + (dot(hi, wrl_ref[0]) + dot(lo, wrh_ref[0]))
        lane = lax.broadcasted_iota(jnp.int32, logits.shape, 1)
        logits = jnp.where(lane < N_EXPERTS, logits, -jnp.inf)
        v1 = jnp.max(logits, axis=-1, keepdims=True)
        i1 = jnp.min(jnp.where(logits == v1, lane, LANES), axis=-1, keepdims=True)
        rest = jnp.where(lane == i1, -jnp.inf, logits)
        v2 = jnp.max(rest, axis=-1, keepdims=True)
        i2 = jnp.min(jnp.where(rest == v2, lane, LANES), axis=-1, keepdims=True)
        e2 = jnp.exp(v2 - v1)
        w1 = 1.0 / (1.0 + e2)
        w2 = e2 / (1.0 + e2)
        if moe == "comb":
            comb_ref[...] = jnp.where(lane == i1, w1, 0.0) + jnp.where(lane == i2, w2, 0.0)
        else:
            tm = logits.shape[0]
            chosen = (lane == i1) | (lane == i2)
            before = (lax.broadcasted_iota(jnp.int32, (tm, tm), 0)
                      > lax.broadcasted_iota(jnp.int32, (tm, tm), 1))
            rank = dot(jnp.where(before, 1.0, 0.0).astype(BF16), jnp.where(chosen, 1.0, 0.0).astype(BF16))
            r1 = jnp.sum(jnp.where(lane == i1, rank, 0.0), axis=-1, keepdims=True).astype(jnp.int32)
            r2 = jnp.sum(jnp.where(lane == i2, rank, 0.0), axis=-1, keepdims=True).astype(jnp.int32)
            meta_i = jnp.where(lane == 0, i1, jnp.where(lane == 1, i2,
                               jnp.where(lane == 2, r1, jnp.where(lane == 3, r2, 0))))
            mi_ref[...] = meta_i[:, :META_W]
            mf_ref[...] = jnp.where(lane == 0, w1, jnp.where(lane == 1, w2, 0.0))[:, :META_W]
            counts = jnp.sum(jnp.where(chosen, 1.0, 0.0), axis=0, keepdims=True)
            cnt_ref[0] = jnp.broadcast_to(counts, cnt_ref.shape[1:]).astype(jnp.int32)


META_W = 8


def merge(x, hf, hb, gr, o, mod, g_grp, w_out, g_ffn, router, *, layer, moe_layer, row, tm, moe):
    n = x.shape[0]
    tile = lambda w: pl.BlockSpec((tm, w), lambda i: (i, 0))
    whole = lambda a: pl.BlockSpec(a.shape, lambda i: (0,) * a.ndim)
    in_specs = [tile(D_MODEL), tile(D_RNN), tile(D_RNN), tile(D_RNN), tile(D_ATTN),
                whole(mod), whole(g_grp),
                pl.BlockSpec((1, D_MODEL, D_MODEL), lambda i: (layer, 0, 0)), whole(g_ffn)]
    args = [x, hf, hb, gr, o, mod, g_grp, w_out, g_ffn]
    out_specs = [tile(D_MODEL), tile(D_MODEL)]
    out_shape = [jax.ShapeDtypeStruct((n, D_MODEL), F32), jax.ShapeDtypeStruct((n, D_MODEL), BF16)]
    if moe:
        in_specs += [pl.BlockSpec((1, D_MODEL, LANES), lambda i: (moe_layer, 0, 0))] * 2
        args += list(router)
    if moe == "comb":
        out_specs.append(tile(LANES))
        out_shape.append(jax.ShapeDtypeStruct((n, LANES), F32))
    elif moe == "route":
        out_specs += [tile(META_W), tile(META_W), pl.BlockSpec((1, SUBLANES, LANES), lambda i: (i, 0, 0))]
        out_shape += [jax.ShapeDtypeStruct((n, META_W), jnp.int32), jax.ShapeDtypeStruct((n, META_W), F32),
                      jax.ShapeDtypeStruct((n // tm, SUBLANES, LANES), jnp.int32)]
    return pl.pallas_call(
        functools.partial(_merge_kernel, row=row, moe=moe),
        grid=(n // tm,),
        in_specs=in_specs,
        out_specs=out_specs,
        out_shape=out_shape,
        compiler_params=_params("arbitrary"),
        name="merge",
    )(*args)


def _ffn_kernel(*refs, row, n_exp, nf, moe, final):
    refs = list(refs)
    h_ref, x_ref, mod_ref = refs[:3]
    pos = 3
    comb_ref = None
    if moe:
        comb_ref = refs[pos]
        pos += 1
    gfin_ref = None
    if final:
        gfin_ref = refs[pos]
        pos += 1
    wg_ref, wu_ref, wd_ref, o_ref, acc_ref = refs[pos:pos + 5]
    tot_ref = refs[pos + 5] if moe else acc_ref
    e = pl.program_id(1)
    f = pl.program_id(2)

    @pl.when(f == 0)
    def _():
        acc_ref[...] = jnp.zeros_like(acc_ref)

    if moe:
        @pl.when((e == 0) & (f == 0))
        def _():
            tot_ref[...] = jnp.zeros_like(tot_ref)

    h = h_ref[...]
    a = jnp.dot(h, wg_ref[0], preferred_element_type=F32)
    b = jnp.dot(h, wu_ref[0], preferred_element_type=F32)
    act = (a * jax.nn.sigmoid(a)) * b
    acc_ref[...] += jnp.dot(act.astype(BF16), wd_ref[0], preferred_element_type=F32)

    if moe:
        @pl.when(f == nf - 1)
        def _():
            comb = comb_ref[...]
            lane = lax.broadcasted_iota(jnp.int32, comb.shape, 1)
            ce = jnp.sum(jnp.where(lane == e, comb, 0.0), axis=-1, keepdims=True)
            tot_ref[...] += ce * acc_ref[...]

    @pl.when((e == n_exp - 1) & (f == nf - 1))
    def _():
        x = x_ref[...] + mod_ref[0, row:row + 1, GT_F:GT_F + D_MODEL] * tot_ref[...]
        if final:
            x = _rms(x) * gfin_ref[...]
        o_ref[...] = x


def ffn(h2, x, mod, comb, g_final, wg, wu, wd, *, first, n_exp, row, tm, tf):
    n = x.shape[0]
    d_ff = wg.shape[2]
    nf = d_ff // tf
    moe = comb is not None
    final = g_final is not None
    tile = lambda w: pl.BlockSpec((tm, w), lambda t, e, f: (t, 0))
    in_specs = [tile(D_MODEL), tile(D_MODEL), pl.BlockSpec(mod.shape, lambda t, e, f: (0, 0, 0))]
    args = [h2, x, mod]
    if moe:
        in_specs.append(tile(LANES))
        args.append(comb)
    if final:
        in_specs.append(pl.BlockSpec((1, D_MODEL), lambda t, e, f: (0, 0)))
        args.append(g_final)
    in_specs += [
        pl.BlockSpec((1, D_MODEL, tf), lambda t, e, f: (first + e, 0, f)),
        pl.BlockSpec((1, D_MODEL, tf), lambda t, e, f: (first + e, 0, f)),
        pl.BlockSpec((1, tf, D_MODEL), lambda t, e, f: (first + e, f, 0)),
    ]
    args += [wg, wu, wd]
    scratch = [pltpu.VMEM((tm, D_MODEL), F32)]
    if moe:
        scratch.append(pltpu.VMEM((tm, D_MODEL), F32))
    return pl.pallas_call(
        functools.partial(_ffn_kernel, row=row, n_exp=n_exp, nf=nf, moe=moe, final=final),
        grid=(n // tm, n_exp, nf),
        in_specs=in_specs,
        out_specs=tile(D_MODEL),
        out_shape=jax.ShapeDtypeStruct((n, D_MODEL), F32),
        scratch_shapes=scratch,
        compiler_params=_params("arbitrary", "arbitrary", "arbitrary"),
        name="ffn",
    )(*args)


MOE_TILE = 512
MOE_TOKENS = 512
SEG_ALIGN = 16
PERM_ROWS = 2 * MOE_TOKENS + N_EXPERTS * SEG_ALIGN
CHUNKS = (64, 32, 16)


def _block_tables(cnt_tile, meta_i, tm):
    i32 = jnp.int32
    n_tiles = cnt_tile.shape[0]
    n = meta_i.shape[0]
    cnt = cnt_tile[:, 0, :N_EXPERTS]
    c16 = (cnt + SEG_ALIGN - 1) // SEG_ALIGN * SEG_ALIGN
    seg_off = jnp.cumsum(c16, axis=1) - c16
    run = jnp.cumsum(c16, axis=0) - c16
    total = jnp.sum(c16, axis=0)
    tiles = (total + MOE_TILE - 1) // MOE_TILE
    cum = jnp.cumsum(tiles)
    start = (cum - tiles) * MOE_TILE
    dst = start[None, :] + run
    experts = jnp.arange(N_EXPERTS, dtype=i32)
    off_tok = jnp.repeat(seg_off, tm, axis=0)
    pick = lambda e: jnp.sum(jnp.where(e[:, None] == experts[None, :], off_tok, 0), axis=1)
    d1 = (meta_i[:, 2] + pick(meta_i[:, 0])).astype(i32)
    d2 = (meta_i[:, 3] + pick(meta_i[:, 1])).astype(i32)
    d_rows = jnp.zeros((n_tiles, SUBLANES, tm), i32).at[:, 0].set(d1.reshape(n_tiles, tm))
    d_rows = d_rows.at[:, 1].set(d2.reshape(n_tiles, tm))
    d_cols = jnp.zeros((n, META_W), i32).at[:, 0].set(d1).at[:, 1].set(d2)
    n_pad = _moe_rows(n)
    n_used = cum[-1]
    t = jnp.arange(n_pad // MOE_TILE, dtype=i32)
    te = jnp.sum((t[:, None] >= cum[None, :]).astype(i32), axis=1)
    used = t < n_used
    tile_expert = jnp.where(used, te, jnp.max(jnp.where(used, te, 0))).astype(i32)
    flat = lambda a: a.reshape(-1).astype(i32)
    return dict(seg_off=flat(seg_off), c16=flat(c16), dst=flat(dst), zero_start=(start + total).astype(i32),
                zero_len=(tiles * MOE_TILE - total).astype(i32), n_used=n_used.reshape(1).astype(i32),
                tile_expert=tile_expert, d_rows=d_rows, d_cols=d_cols)


def _moe_rows(n):
    worst = 2 * n + (n // MOE_TOKENS) * N_EXPERTS * SEG_ALIGN + N_EXPERTS * MOE_TILE
    return (worst + MOE_TILE - 1) // MOE_TILE * MOE_TILE


def _segment_copies(src, dst, src_off, dst_off, rows, sems, act):
    big, mid, small = CHUNKS
    n_big = lax.shift_right_logical(rows, big.bit_length() - 1)

    def copy(size, k_src, k_dst, sem):
        return pltpu.make_async_copy(src.at[pl.ds(pl.multiple_of(k_src, SEG_ALIGN), size)],
                                     dst.at[pl.ds(pl.multiple_of(k_dst, SEG_ALIGN), size)], sem)

    def body(k, c):
        act(copy(big, src_off + k * big, dst_off + k * big, sems.at[0]))
        return c

    lax.fori_loop(0, n_big, body, 0)
    done = n_big * big

    @pl.when((rows & mid) != 0)
    def _():
        act(copy(mid, src_off + done, dst_off + done, sems.at[1]))

    @pl.when((rows & small) != 0)
    def _():
        act(copy(small, src_off + done + (rows & mid), dst_off + done + (rows & mid), sems.at[2]))


ZERO_SIZES = (256, 128, 64, 32, 16)


def _zero_fill(zero_ref, dst, start, rows, sems, act):
    done = 0 * rows
    for s, size in enumerate(ZERO_SIZES):
        @pl.when((rows & size) != 0)
        def _(s=s, size=size, done=done):
            act(pltpu.make_async_copy(zero_ref.at[pl.ds(0, size)],
                                      dst.at[pl.ds(pl.multiple_of(start + done, SEG_ALIGN), size)], sems.at[s]))
        done = done + (rows & size)


def _start(copy):
    copy.start()


def _wait(copy):
    copy.wait()


def _dispatch_kernel(so_ref, c16_ref, dst_ref, zs_ref, zl_ref, nu_ref, h_ref, d_ref, x_hbm,
                      z_ref, zero_ref, sems, zsems, *, tm, n_tiles, n_pad):
    i = pl.program_id(0)
    d1 = d_ref[0, 0:1, :]
    d2 = d_ref[0, 1:2, :]
    r = lax.broadcasted_iota(jnp.int32, (PERM_ROWS, tm), 0)
    perm = jnp.where((r == d1) | (r == d2), 1.0, 0.0).astype(BF16)
    z_ref[...] = jnp.dot(perm, h_ref[...], preferred_element_type=F32).astype(BF16)
    for act in (_start, _wait):
        for e in range(N_EXPERTS):
            k = i * N_EXPERTS + e
            _segment_copies(z_ref, x_hbm, so_ref[k], dst_ref[k], c16_ref[k], sems, act)

    @pl.when(i == n_tiles - 1)
    def _():
        zero_ref[...] = jnp.zeros_like(zero_ref)
        for act in (_start, _wait):
            for e in range(N_EXPERTS):
                _zero_fill(zero_ref, x_hbm, zs_ref[e], zl_ref[e], zsems, act)

            def tail(t, c):
                act(pltpu.make_async_copy(zero_ref, x_hbm.at[pl.ds(pl.multiple_of(t * MOE_TILE, MOE_TILE),
                                                                   MOE_TILE)], zsems.at[len(ZERO_SIZES)]))
                return c

            lax.fori_loop(nu_ref[0], n_pad // MOE_TILE, tail, 0)


def moe_dispatch(tab, h2, *, tm):
    n = h2.shape[0]
    n_tiles = n // tm
    n_pad = _moe_rows(n)
    return pl.pallas_call(
        functools.partial(_dispatch_kernel, tm=tm, n_tiles=n_tiles, n_pad=n_pad),
        grid_spec=pltpu.PrefetchScalarGridSpec(
            num_scalar_prefetch=6,
            grid=(n_tiles,),
            in_specs=[pl.BlockSpec((tm, D_MODEL), lambda i, *_: (i, 0)),
                      pl.BlockSpec((1, SUBLANES, tm), lambda i, *_: (i, 0, 0))],
            out_specs=pl.BlockSpec(memory_space=pl.ANY),
            scratch_shapes=[pltpu.VMEM((PERM_ROWS, D_MODEL), BF16), pltpu.VMEM((MOE_TILE, D_MODEL), BF16),
                            pltpu.SemaphoreType.DMA((len(CHUNKS),)),
                            pltpu.SemaphoreType.DMA((len(ZERO_SIZES) + 1,))],
        ),
        out_shape=jax.ShapeDtypeStruct((n_pad, D_MODEL), BF16),
        compiler_params=_params("arbitrary"),
        name="moe_dispatch",
    )(tab["seg_off"], tab["c16"], tab["dst"], tab["zero_start"], tab["zero_len"], tab["n_used"],
      h2, tab["d_rows"])


def _grouped_ffn_kernel(te_ref, nt_ref, x_ref, wg_ref, wu_ref, wd_ref, y_ref):
    t = pl.program_id(0)

    @pl.when(t < nt_ref[0])
    def _():
        x = x_ref[...]
        a = jnp.dot(x, wg_ref[0], preferred_element_type=F32)
        b = jnp.dot(x, wu_ref[0], preferred_element_type=F32)
        act = (a * jax.nn.sigmoid(a)) * b
        y_ref[...] = jnp.dot(act.astype(BF16), wd_ref[0], preferred_element_type=F32).astype(y_ref.dtype)

    @pl.when(t >= nt_ref[0])
    def _():
        y_ref[...] = jnp.zeros_like(y_ref)


def grouped_ffn(tile_expert, n_used, xs, wg, wu, wd, *, first):
    n_pad = xs.shape[0]
    d_ff = wg.shape[2]
    rows = pl.BlockSpec((MOE_TILE, D_MODEL), lambda t, te, nt: (t, 0))
    return pl.pallas_call(
        _grouped_ffn_kernel,
        grid_spec=pltpu.PrefetchScalarGridSpec(
            num_scalar_prefetch=2,
            grid=(n_pad // MOE_TILE,),
            in_specs=[rows,
                      pl.BlockSpec((1, D_MODEL, d_ff), lambda t, te, nt: (first + te[t], 0, 0)),
                      pl.BlockSpec((1, D_MODEL, d_ff), lambda t, te, nt: (first + te[t], 0, 0)),
                      pl.BlockSpec((1, d_ff, D_MODEL), lambda t, te, nt: (first + te[t], 0, 0))],
            out_specs=rows,
        ),
        out_shape=jax.ShapeDtypeStruct((n_pad, D_MODEL), BF16),
        compiler_params=_params("arbitrary"),
        name="grouped_ffn",
    )(tile_expert, n_used, xs, wg, wu, wd)


def _combine_kernel(*refs, row, tm, final):
    if final:
        so_ref, c16_ref, dst_ref, x_ref, dc_ref, mf_ref, mod_ref, gfin_ref, y_hbm, o_ref, yt_ref, sems = refs
    else:
        so_ref, c16_ref, dst_ref, x_ref, dc_ref, mf_ref, mod_ref, y_hbm, o_ref, yt_ref, sems = refs
    i = pl.program_id(0)

    @pl.when(i == 0)
    def _():
        yt_ref[...] = jnp.zeros_like(yt_ref)

    for act in (_start, _wait):
        for e in range(N_EXPERTS):
            k = i * N_EXPERTS + e
            _segment_copies(y_hbm, yt_ref, dst_ref[k], so_ref[k], c16_ref[k], sems, act)

    yt = yt_ref[...]
    r = lax.broadcasted_iota(jnp.int32, (tm, PERM_ROWS), 1)
    pick = lambda col: jnp.dot(jnp.where(r == dc_ref[:, col:col + 1], 1.0, 0.0).astype(BF16), yt,
                               preferred_element_type=F32)
    tot = mf_ref[:, 0:1] * pick(0) + mf_ref[:, 1:2] * pick(1)
    x = x_ref[...] + mod_ref[0, row:row + 1, GT_F:GT_F + D_MODEL] * tot
    if final:
        x = _rms(x) * gfin_ref[...]
    o_ref[...] = x


def moe_combine(tab, x, meta_f, mod, g_final, ys, *, row, tm):
    n = x.shape[0]
    final = g_final is not None
    tile = lambda w: pl.BlockSpec((tm, w), lambda i, *_: (i, 0))
    in_specs = [tile(D_MODEL), tile(META_W), tile(META_W), pl.BlockSpec(mod.shape, lambda i, *_: (0, 0, 0))]
    args = [x, tab["d_cols"], meta_f, mod]
    if final:
        in_specs.append(pl.BlockSpec((1, D_MODEL), lambda i, *_: (0, 0)))
        args.append(g_final)
    in_specs.append(pl.BlockSpec(memory_space=pl.ANY))
    args.append(ys)
    return pl.pallas_call(
        functools.partial(_combine_kernel, row=row, tm=tm, final=final),
        grid_spec=pltpu.PrefetchScalarGridSpec(
            num_scalar_prefetch=3,
            grid=(n // tm,),
            in_specs=in_specs,
            out_specs=tile(D_MODEL),
            scratch_shapes=[pltpu.VMEM((PERM_ROWS, D_MODEL), BF16), pltpu.SemaphoreType.DMA((len(CHUNKS),))],
        ),
        out_shape=jax.ShapeDtypeStruct((n, D_MODEL), F32),
        compiler_params=_params("arbitrary"),
        name="moe_combine",
    )(tab["seg_off"], tab["c16"], tab["dst"], *args)


def _gate_weights(w_rg):
    depth = w_rg.shape[0]
    per = RNN_CHUNK // RNN_BLOCK
    nch = D_RNN // RNN_CHUNK
    wr = w_rg.astype(BF16).reshape(depth, 2, 2, nch, per, RNN_BLOCK, 1, RNN_BLOCK)
    on_diag = jnp.eye(per, dtype=bool).reshape(per, 1, per, 1)
    bd = jnp.where(on_diag, wr, jnp.zeros((), BF16))
    bd = bd.reshape(depth, 2, 2, nch, RNN_CHUNK, RNN_CHUNK)
    return jnp.concatenate([bd[:, :, 0], bd[:, :, 1]], axis=-1)


def _tile_rows(n, want):
    return want if n % want == 0 else n


def kernel(x, c, ctx, c_ctx, w_mod, b_mod, g_mix, g_ffn, w_in, conv_w, conv_b, w_rg, b_rg, lam,
           sink, g_grp, w_out, w_ffn_gate, w_ffn_up, w_ffn_down, w_router, w_exp_gate, w_exp_up,
           w_exp_down, g_final):
    assert x.shape[0] == 1 and ctx.shape[0] == 1
    depth = w_mod.shape[0]
    seq, n_ctx = x.shape[1], ctx.shape[1]
    xl, xc = x[0], ctx[0]

    cond8 = jnp.zeros((SUBLANES, D_MODEL), F32).at[0].set(c[0]).at[1].set(c_ctx)
    mods = adaln_all(cond8, w_mod, b_mod)
    cos, sin = rope_tables(seq)

    w_in_b = w_in.astype(BF16)
    w_out_b = w_out.astype(BF16)
    wg_gate = _gate_weights(w_rg)
    ffn_w = (w_ffn_gate.astype(BF16), w_ffn_up.astype(BF16), w_ffn_down.astype(BF16))
    exp_w = tuple(w.astype(BF16).reshape((-1,) + w.shape[2:]) for w in (w_exp_gate, w_exp_up, w_exp_down))
    wr_pad = jnp.pad(w_router, ((0, 0), (0, 0), (0, LANES - N_EXPERTS)))
    wr_hi = wr_pad.astype(BF16)
    wr_lo = (wr_pad - wr_hi.astype(F32)).astype(BF16)
    zeros_h0 = jnp.zeros((SUBLANES, D_RNN), F32)

    tm = _tile_rows(seq, 512)
    tm_ffn = _tile_rows(seq, 1024)

    for l in range(depth):
        last = l == depth - 1
        moe = l % 2 == 1
        j = l // 2
        mod = mods[l:l + 1]
        gm, gf, gg = g_mix[l:l + 1], g_ffn[l:l + 1], g_grp[l:l + 1]

        xr_c, gr_c, q_c, kv_c = in_proj(xc, mod, gm, w_in_b, None, None, layer=l, row=1, tm=n_ctx)
        xr_l, gr_l, q_l, kv_l = in_proj(xl, mod, gm, w_in_b, cos, sin, layer=l, row=0, tm=tm)

        rnn_args = (conv_w[l], conv_b[l:l + 1], wg_gate, b_rg[l], lam[l])
        hf_c, hb_c, h_ctx_end = rnn_bidir(xr_c, *rnn_args, zeros_h0, layer=l, tm=n_ctx)
        hf_l, hb_l, _ = rnn_bidir(xr_l, *rnn_args, h_ctx_end, layer=l, tm=tm)

        o_l = attention(q_l, kv_l, kv_c, sink[l], band=True)

        router = (wr_hi, wr_lo) if moe else None
        weights = exp_w if moe else ffn_w
        first = j * N_EXPERTS if moe else j
        tf = 512
        merge_l = functools.partial(merge, layer=l, moe_layer=j)

        g_fin = g_final.reshape(1, D_MODEL) if last else None
        if moe:
            xl, h2, meta_i, meta_f, cnt = merge_l(xl, hf_l, hb_l, gr_l, o_l, mod, gg, w_out_b, gf, router,
                                                  row=0, tm=tm, moe="route")
            assert tm == MOE_TOKENS
            tab = _block_tables(cnt, meta_i, tm)
            xs = moe_dispatch(tab, h2, tm=tm)
            ys = grouped_ffn(tab["tile_expert"], tab["n_used"], xs, *weights, first=first)
            xl = moe_combine(tab, xl, meta_f, mod, g_fin, ys, row=0, tm=tm)
        else:
            xl, h2 = merge_l(xl, hf_l, hb_l, gr_l, o_l, mod, gg, w_out_b, gf, None, row=0, tm=tm, moe=None)
            xl = ffn(h2, xl, mod, None, g_fin, *weights, first=first, n_exp=1, row=0, tm=tm_ffn, tf=tf)

        if not last:
            o_c = attention(q_c, kv_c, kv_c, sink[l], band=False)
            res = merge_l(xc, hf_c, hb_c, gr_c, o_c, mod, gg, w_out_b, gf, router, row=1, tm=n_ctx,
                          moe="comb" if moe else None)
            xc, h2c = res[0], res[1]
            comb_c = res[2] if moe else None
            xc = ffn(h2c, xc, mod, comb_c, None, *weights, first=first, n_exp=N_EXPERTS if moe else 1,
                     row=1, tm=n_ctx, tf=tf)

    return xl[None]
```

```python
import functools

import jax
import jax.numpy as jnp
from jax import lax
from jax.experimental import pallas as pl
from jax.experimental.pallas import tpu as pltpu

F32 = jnp.float32
BF16 = jnp.bfloat16

D_MODEL = 1024
D_RNN = 512
D_ATTN = 512
D_KV = 128
HEAD_DIM = 64
N_HEADS = 8
N_KV_HEADS = 2
HEADS_PER_KV = N_HEADS // N_KV_HEADS
D_IN = 2 * D_RNN + D_ATTN + 2 * D_KV
RNN_BLOCK = 64
RNN_CHUNK = 256
CONV_W = 4
LRU_C = 8.0
WINDOW = 128
BLOCK_Q = 128
GRID_W = 64
ROPE_BASE = 10000.0
N_EXPERTS = 8
EPS = 1e-6
NEG_INF = -1e30

LANES = 128
SUBLANES = 8
VMEM_LIMIT = 56 * 1024 * 1024

SH_M, SC_M, GT_M, SH_F, SC_F, GT_F = (k * D_MODEL for k in range(6))


def _params(*sem):
    return pltpu.CompilerParams(dimension_semantics=sem, vmem_limit_bytes=VMEM_LIMIT)


def _rms(x):
    return x * lax.rsqrt(jnp.mean(x * x, axis=-1, keepdims=True) + EPS)


def _adaln_kernel(c_ref, w_ref, b_ref, o_ref):
    c = c_ref[...]
    s = c * jax.nn.sigmoid(c)
    w = w_ref[0]
    s_hi, w_hi = s.astype(BF16), w.astype(BF16)
    s_lo = (s - s_hi.astype(F32)).astype(BF16)
    w_lo = (w - w_hi.astype(F32)).astype(BF16)
    dot = lambda a, b: jnp.dot(a, b, preferred_element_type=F32)
    o_ref[0] = dot(s_hi, w_hi) + (dot(s_hi, w_lo) + dot(s_lo, w_hi)) + b_ref[0]


def adaln_all(cond8, w_mod, b_mod):
    depth = w_mod.shape[0]
    nc = 1536
    return pl.pallas_call(
        _adaln_kernel,
        grid=(depth, 6 * D_MODEL // nc),
        in_specs=[
            pl.BlockSpec((SUBLANES, D_MODEL), lambda l, j: (0, 0)),
            pl.BlockSpec((1, D_MODEL, nc), lambda l, j: (l, 0, j)),
            pl.BlockSpec((1, 1, nc), lambda l, j: (l, 0, j)),
        ],
        out_specs=pl.BlockSpec((1, SUBLANES, nc), lambda l, j: (l, 0, j)),
        out_shape=jax.ShapeDtypeStruct((depth, SUBLANES, 6 * D_MODEL), F32),
        compiler_params=_params("arbitrary", "arbitrary"),
        name="adaln",
    )(cond8, w_mod, b_mod.reshape(depth, 1, 6 * D_MODEL))


ROPE_ROWS = SUBLANES * GRID_W


def _rope_kernel(f_ref, c_ref, s_ref):
    i = pl.program_id(0)
    lane = lax.broadcasted_iota(jnp.int32, (GRID_W, LANES), 1)
    col_axis = ((lane >> 5) & 1) == 1
    sign = jnp.where(((lane >> 4) & 1) == 0, -1.0, 1.0).astype(F32)
    freq = f_ref[...]
    rowpos = (i * SUBLANES + lax.broadcasted_iota(jnp.int32, (SUBLANES, LANES), 0)).astype(F32)
    colpos = lax.broadcasted_iota(jnp.int32, (GRID_W, LANES), 0).astype(F32)
    ang_r = rowpos * freq
    ang_c = colpos * freq
    cr, sr = jnp.cos(ang_r), jnp.sin(ang_r)
    cc, sc = jnp.cos(ang_c), jnp.sin(ang_c) * sign
    for g in range(SUBLANES):
        rows = slice(g * GRID_W, (g + 1) * GRID_W)
        c_ref[rows, :] = jnp.where(col_axis, cc, jnp.broadcast_to(cr[g:g + 1, :], (GRID_W, LANES)))
        s_ref[rows, :] = jnp.where(col_axis, sc, jnp.broadcast_to(sr[g:g + 1, :], (GRID_W, LANES)) * sign)


def rope_tables(seq):
    axis_dim = HEAD_DIM // 2
    freqs = ROPE_BASE ** (-jnp.arange(0, axis_dim, 2, dtype=F32) / axis_dim)
    freq_lane = jnp.tile(freqs, LANES // freqs.shape[0]).reshape(1, LANES)
    return pl.pallas_call(
        _rope_kernel,
        grid=(seq // ROPE_ROWS,),
        in_specs=[pl.BlockSpec((1, LANES), lambda i: (0, 0))],
        out_specs=[pl.BlockSpec((ROPE_ROWS, LANES), lambda i: (i, 0))] * 2,
        out_shape=[jax.ShapeDtypeStruct((seq, LANES), F32)] * 2,
        compiler_params=_params("arbitrary"),
        name="rope_tables",
    )(freq_lane)


def _rope_chunk(x, cos, sin):
    lane = lax.broadcasted_iota(jnp.int32, x.shape, 1)
    first_half = ((lane >> 4) & 1) == 0
    partner = jnp.where(first_half, pltpu.roll(x, LANES - 16, axis=1), pltpu.roll(x, 16, axis=1))
    return x * cos + partner * sin


def _in_proj_kernel(*refs, row, rope):
    if rope:
        x_ref, mod_ref, g_ref, w_ref, cos_ref, sin_ref, xr_ref, gr_ref, q_ref, kv_ref = refs
    else:
        x_ref, mod_ref, g_ref, w_ref, xr_ref, gr_ref, q_ref, kv_ref = refs
    x = x_ref[...]
    shift = mod_ref[0, row:row + 1, SH_M:SH_M + D_MODEL]
    scale = mod_ref[0, row:row + 1, SC_M:SC_M + D_MODEL]
    h = (_rms(x) * g_ref[...]) * (1.0 + scale) + shift
    p = jnp.dot(h.astype(BF16), w_ref[0], preferred_element_type=F32)
    xr_ref[...] = p[:, 0:D_RNN]
    gr_ref[...] = p[:, D_RNN:2 * D_RNN].astype(gr_ref.dtype)
    q0 = 2 * D_RNN
    if rope:
        cos, sin = cos_ref[...], sin_ref[...]
    for c in range(D_ATTN // LANES):
        qc = p[:, q0 + c * LANES:q0 + (c + 1) * LANES]
        if rope:
            qc = _rope_chunk(qc, cos, sin)
        q_ref[:, c * LANES:(c + 1) * LANES] = (qc * (HEAD_DIM ** -0.5)).astype(BF16)
    k = p[:, q0 + D_ATTN:q0 + D_ATTN + D_KV]
    v = p[:, q0 + D_ATTN + D_KV:q0 + D_ATTN + 2 * D_KV]
    if rope:
        k = _rope_chunk(k, cos, sin)
    kv_ref[:, 0:LANES] = k.astype(BF16)
    kv_ref[:, LANES:2 * LANES] = pltpu.roll(k, HEAD_DIM, axis=1).astype(BF16)
    kv_ref[:, 2 * LANES:3 * LANES] = v.astype(BF16)
    kv_ref[:, 3 * LANES:4 * LANES] = pltpu.roll(v, HEAD_DIM, axis=1).astype(BF16)


def in_proj(x, mod, g_mix, w_in, cos, sin, *, layer, row, tm):
    n = x.shape[0]
    rope = cos is not None
    tile = lambda w: pl.BlockSpec((tm, w), lambda i: (i, 0))
    in_specs = [
        tile(D_MODEL),
        pl.BlockSpec((1, SUBLANES, 6 * D_MODEL), lambda i: (0, 0, 0)),
        pl.BlockSpec((1, D_MODEL), lambda i: (0, 0)),
        pl.BlockSpec((1, D_MODEL, D_IN), lambda i: (layer, 0, 0)),
    ]
    args = [x, mod, g_mix, w_in]
    if rope:
        in_specs += [tile(LANES), tile(LANES)]
        args += [cos, sin]
    return pl.pallas_call(
        functools.partial(_in_proj_kernel, row=row, rope=rope),
        grid=(n // tm,),
        in_specs=in_specs,
        out_specs=[tile(D_RNN), tile(D_RNN), tile(D_ATTN), tile(4 * LANES)],
        out_shape=[
            jax.ShapeDtypeStruct((n, D_RNN), F32),
            jax.ShapeDtypeStruct((n, D_RNN), BF16),
            jax.ShapeDtypeStruct((n, D_ATTN), BF16),
            jax.ShapeDtypeStruct((n, 4 * LANES), BF16),
        ],
        compiler_params=_params("arbitrary"),
        name="in_proj",
    )(*args)


RNN_SLABS = D_RNN // LANES
SEG_PAD = 4


def _rnn_kernel(pf_ref, mf_ref, nf_ref, pb_ref, mb_ref, nb_ref, cw_ref, cb_ref, wg_ref, bg_ref,
                lam_ref, h0_ref, hf_ref, hb_ref, hl_ref, ext_ref, xs_ref, u_ref, a_ref, b_ref, hs_ref,
                cf_ref, cbk_ref, *, tm, nt):
    i = pl.program_id(0)
    seg = tm // SUBLANES
    pitch = seg + SEG_PAD
    slab = lambda c: slice(c * LANES, (c + 1) * LANES)

    @pl.when(i == 0)
    def _():
        cf_ref[...] = jnp.broadcast_to(h0_ref[0:1, :], (SUBLANES, D_RNN))
        cbk_ref[...] = jnp.broadcast_to(h0_ref[1:2, :], (SUBLANES, D_RNN))

    softplus_neg_lam = jax.nn.softplus(-lam_ref[...])
    seg_id = lax.broadcasted_iota(jnp.int32, (SUBLANES, D_RNN), 0)

    def run(d, prev_ref, main_ref, next_ref, at_start, at_end, out_ref, carry_ref):
        ext_ref[0:SUBLANES, :] = jnp.where(at_start, 0.0, prev_ref[...])
        ext_ref[SUBLANES:tm + SUBLANES, :] = main_ref[...]
        ext_ref[tm + SUBLANES:tm + 2 * SUBLANES, :] = jnp.where(at_end, 0.0, next_ref[...])
        halo = SUBLANES - CONV_W // 2
        for j in range(SUBLANES):
            for c in range(RNN_SLABS):
                xs_ref[c, j * pitch:j * pitch + seg + SUBLANES, :] = (
                    ext_ref[j * seg + halo:j * seg + halo + seg + SUBLANES, slab(c)])
        for g in range(seg):
            for c in range(RNN_SLABS):
                acc = cb_ref[:, slab(c)]
                for k in range(CONV_W):
                    acc = acc + xs_ref[c, pl.ds(g + k, SUBLANES, stride=pitch), :] * cw_ref[k:k + 1, slab(c)]
                u_ref[g * SUBLANES:(g + 1) * SUBLANES, slab(c)] = acc
        u = u_ref[...]
        ub = u.astype(BF16)
        for c in range(D_RNN // RNN_CHUNK):
            ch = slice(c * RNN_CHUNK, (c + 1) * RNN_CHUNK)
            g = jnp.dot(ub[:, ch], wg_ref[0, d, c], preferred_element_type=F32)
            r = jax.nn.sigmoid(g[:, :RNN_CHUNK] + bg_ref[d, 0:1, ch])
            ig = jax.nn.sigmoid(g[:, RNN_CHUNK:] + bg_ref[d, 1:2, ch])
            log_a = (-LRU_C * r) * softplus_neg_lam[d:d + 1, ch]
            a = jnp.exp(log_a)
            mult = jnp.sqrt(jnp.maximum(1.0 - a * a, 0.0))
            a_ref[:, ch] = a
            b_ref[:, ch] = mult * (ig * u[:, ch])

        def scan(n, hp):
            g = n if d == 0 else seg - 1 - n
            off = pl.multiple_of(g * SUBLANES, SUBLANES)
            a_g = a_ref[pl.ds(off, SUBLANES), :]
            h = a_g * hp[0] + b_ref[pl.ds(off, SUBLANES), :]
            p = a_g * hp[1]
            b_ref[pl.ds(off, SUBLANES), :] = h
            a_ref[pl.ds(off, SUBLANES), :] = p
            return h, p

        zeros = jnp.zeros((SUBLANES, D_RNN), F32)
        h_end, p_end = lax.fori_loop(0, seg, scan, (zeros, zeros + 1.0), unroll=8)

        carry = carry_ref[...]
        enter = zeros
        for n in range(SUBLANES):
            j = n if d == 0 else SUBLANES - 1 - n
            enter = jnp.where(seg_id == j, carry, enter)
            carry = jnp.broadcast_to(h_end[j:j + 1, :] + p_end[j:j + 1, :] * carry[0:1, :], (SUBLANES, D_RNN))
        carry_ref[...] = carry

        for g in range(seg):
            rows = slice(g * SUBLANES, (g + 1) * SUBLANES)
            h = b_ref[rows, :] + a_ref[rows, :] * enter
            for c in range(RNN_SLABS):
                hs_ref[c, pl.ds(g, SUBLANES, stride=pitch), :] = h[:, slab(c)]
        for j in range(SUBLANES):
            for c in range(RNN_SLABS):
                out_ref[j * seg:(j + 1) * seg, slab(c)] = (
                    hs_ref[c, j * pitch:j * pitch + seg, :].astype(out_ref.dtype))

    run(0, pf_ref, mf_ref, nf_ref, i == 0, i == nt - 1, hf_ref, cf_ref)
    run(1, pb_ref, mb_ref, nb_ref, i == nt - 1, i == 0, hb_ref, cbk_ref)

    @pl.when(i == nt - 1)
    def _():
        rows = lax.broadcasted_iota(jnp.int32, (SUBLANES, D_RNN), 0)
        hl_ref[...] = jnp.where(rows == 0, cf_ref[...], jnp.where(rows == 1, cbk_ref[...], 0.0))


def rnn_bidir(xr, conv_w, conv_b, wg, bg, lam, h0, *, layer, tm):
    n = xr.shape[0]
    nt = n // tm
    per = tm // SUBLANES
    last8 = n // SUBLANES - 1
    fwd = lambda i: i
    bwd = lambda i: nt - 1 - i
    main = lambda t: pl.BlockSpec((tm, D_RNN), lambda i: (t(i), 0))
    prev = lambda t: pl.BlockSpec((SUBLANES, D_RNN), lambda i: (jnp.maximum(t(i) * per - 1, 0), 0))
    nxt = lambda t: pl.BlockSpec((SUBLANES, D_RNN), lambda i: (jnp.minimum((t(i) + 1) * per, last8), 0))
    whole = lambda a: pl.BlockSpec(a.shape, lambda i: (0,) * a.ndim)
    return pl.pallas_call(
        functools.partial(_rnn_kernel, tm=tm, nt=nt),
        grid=(nt,),
        in_specs=[prev(fwd), main(fwd), nxt(fwd), prev(bwd), main(bwd), nxt(bwd),
                  whole(conv_w), whole(conv_b),
                  pl.BlockSpec((1,) + wg.shape[1:], lambda i: (layer, 0, 0, 0, 0)),
                  whole(bg), whole(lam), whole(h0)],
        out_specs=[main(fwd), main(bwd), pl.BlockSpec((SUBLANES, D_RNN), lambda i: (0, 0))],
        out_shape=[
            jax.ShapeDtypeStruct((n, D_RNN), BF16),
            jax.ShapeDtypeStruct((n, D_RNN), BF16),
            jax.ShapeDtypeStruct((SUBLANES, D_RNN), F32),
        ],
        scratch_shapes=[
            pltpu.VMEM((tm + 2 * SUBLANES, D_RNN), F32),
            pltpu.VMEM((RNN_SLABS, tm + SUBLANES * SEG_PAD + SUBLANES, LANES), F32),
            pltpu.VMEM((tm, D_RNN), F32),
            pltpu.VMEM((tm, D_RNN), F32),
            pltpu.VMEM((tm, D_RNN), F32),
            pltpu.VMEM((RNN_SLABS, tm + SUBLANES * SEG_PAD, LANES), F32),
            pltpu.VMEM((SUBLANES, D_RNN), F32),
            pltpu.VMEM((SUBLANES, D_RNN), F32),
        ],
        compiler_params=_params("arbitrary"),
        name="rnn_bidir",
    )(xr, xr, xr, xr, xr, xr, conv_w, conv_b, wg, bg, lam, h0)


def _attn_kernel(*refs, nb, band, n_ctx):
    if band:
        sink_ref, q_ref, kvx_ref, kvp_ref, kvc_ref, kvn_ref, o_ref = refs
        kv = jnp.concatenate([kvx_ref[...], kvp_ref[...], kvc_ref[...], kvn_ref[...]], axis=0)
    else:
        sink_ref, q_ref, kvx_ref, o_ref = refs
        kv = kvx_ref[...]
    i = pl.program_id(0)
    nk = kv.shape[0]
    lane = lax.broadcasted_iota(jnp.int32, (nk, LANES), 1)
    low = lane < HEAD_DIM
    zero = jnp.zeros((), BF16)
    k, ks, v, vs = (kv[:, j * LANES:(j + 1) * LANES] for j in range(4))
    k_low = (jnp.where(low, k, zero), jnp.where(low, ks, zero))
    k_high = (jnp.where(low, zero, ks), jnp.where(low, zero, k))
    v_low = (jnp.where(low, v, zero), jnp.where(low, vs, zero))
    v_high = (jnp.where(low, zero, vs), jnp.where(low, zero, v))

    rows2 = 2 * BLOCK_Q
    r = lax.broadcasted_iota(jnp.int32, (rows2, nk), 0) & (BLOCK_Q - 1)
    if band:
        jb = lax.broadcasted_iota(jnp.int32, (rows2, nk), 1) - n_ctx
        valid = (jb < 0) | ((jb >= r) & (jb <= r + 2 * WINDOW)
                            & ((jb >= BLOCK_Q) | (i > 0)) & ((jb < 2 * BLOCK_Q) | (i < nb - 1)))
    upper = lax.broadcasted_iota(jnp.int32, (rows2, 1), 0) >= BLOCK_Q

    def probs(q2, kmat, sink_a, sink_b):
        s = lax.dot_general(q2, kmat, (((1,), (1,)), ((), ())), preferred_element_type=F32)
        if band:
            s = jnp.where(valid, s, NEG_INF)
        sink = jnp.where(upper, sink_b, sink_a)
        m = jnp.maximum(jnp.max(s, axis=-1, keepdims=True), sink)
        p = jnp.exp(s - m)
        den = jnp.sum(p, axis=-1, keepdims=True) + jnp.exp(sink - m)
        return p.astype(BF16), 1.0 / den

    for hk in range(N_KV_HEADS):
        c0, c1 = 2 * hk, 2 * hk + 1
        q2 = jnp.concatenate([q_ref[:, c0 * LANES:(c0 + 1) * LANES],
                              q_ref[:, c1 * LANES:(c1 + 1) * LANES]], axis=0)
        h0 = hk * HEADS_PER_KV
        p_low, inv_low = probs(q2, k_low[hk], sink_ref[h0], sink_ref[h0 + 2])
        p_high, inv_high = probs(q2, k_high[hk], sink_ref[h0 + 1], sink_ref[h0 + 3])
        o2 = (jnp.dot(p_low, v_low[hk], preferred_element_type=F32) * inv_low
              + jnp.dot(p_high, v_high[hk], preferred_element_type=F32) * inv_high)
        o_ref[:, c0 * LANES:(c0 + 1) * LANES] = o2[:BLOCK_Q].astype(o_ref.dtype)
        o_ref[:, c1 * LANES:(c1 + 1) * LANES] = o2[BLOCK_Q:].astype(o_ref.dtype)


def attention(q, kv, kv_ctx, sink, *, band):
    n = q.shape[0]
    nb = n // BLOCK_Q
    n_ctx = kv_ctx.shape[0]
    in_specs = [
        pl.BlockSpec(memory_space=pltpu.SMEM),
        pl.BlockSpec((BLOCK_Q, D_ATTN), lambda i: (i, 0)),
        pl.BlockSpec((n_ctx, 4 * LANES), lambda i: (0, 0)),
    ]
    args = [sink, q, kv_ctx]
    if band:
        blk = lambda f: pl.BlockSpec((BLOCK_Q, 4 * LANES), lambda i: (f(i), 0))
        in_specs += [blk(lambda i: jnp.maximum(i - 1, 0)), blk(lambda i: i),
                     blk(lambda i: jnp.minimum(i + 1, nb - 1))]
        args += [kv, kv, kv]
    return pl.pallas_call(
        functools.partial(_attn_kernel, nb=nb, band=band, n_ctx=n_ctx),
        grid=(nb,),
        in_specs=in_specs,
        out_specs=pl.BlockSpec((BLOCK_Q, D_ATTN), lambda i: (i, 0)),
        out_shape=jax.ShapeDtypeStruct((n, D_ATTN), BF16),
        compiler_params=_params("arbitrary"),
        name="attention",
    )(*args)


def _merge_kernel(*refs, row, moe):
    if moe == "comb":
        (x_ref, hf_ref, hb_ref, gr_ref, o_ref, mod_ref, gg_ref, wo_ref, gf_ref, wrh_ref, wrl_ref,
         xo_ref, h2_ref, comb_ref) = refs
    elif moe == "route":
        (x_ref, hf_ref, hb_ref, gr_ref, o_ref, mod_ref, gg_ref, wo_ref, gf_ref, wrh_ref, wrl_ref,
         xo_ref, h2_ref, mi_ref, mf_ref, cnt_ref) = refs
    else:
        x_ref, hf_ref, hb_ref, gr_ref, o_ref, mod_ref, gg_ref, wo_ref, gf_ref, xo_ref, h2_ref = refs
    mod = lambda off: mod_ref[0, row:row + 1, off:off + D_MODEL]
    y_rnn = (hf_ref[...].astype(F32) + hb_ref[...].astype(F32)) * jax.nn.gelu(gr_ref[...].astype(F32))
    y = jnp.concatenate([_rms(y_rnn) * gg_ref[:, :D_RNN],
                         _rms(o_ref[...].astype(F32)) * gg_ref[:, D_RNN:]], axis=1)
    out = jnp.dot(y.astype(BF16), wo_ref[0], preferred_element_type=F32)
    x = x_ref[...] + mod(GT_M) * out
    xo_ref[...] = x
    h2 = (_rms(x) * gf_ref[...]) * (1.0 + mod(SC_F)) + mod(SH_F)
    h2_ref[...] = h2.astype(h2_ref.dtype)
    if moe:
        hi = h2.astype(BF16)
        lo = (h2 - hi.astype(F32)).astype(BF16)
        dot = lambda a, b: jnp.dot(a, b, preferred_element_type=F32)
        logits = dot(hi, wrh_ref[0]) + (dot(hi, wrl_ref[0]) + dot(lo, wrh_ref[0]))
        lane = lax.broadcasted_iota(jnp.int32, logits.shape, 1)
        logits = jnp.where(lane < N_EXPERTS, logits, -jnp.inf)
        v1 = jnp.max(logits, axis=-1, keepdims=True)
        i1 = jnp.min(jnp.where(logits == v1, lane, LANES), axis=-1, keepdims=True)
        rest = jnp.where(lane == i1, -jnp.inf, logits)
        v2 = jnp.max(rest, axis=-1, keepdims=True)
        i2 = jnp.min(jnp.where(rest == v2, lane, LANES), axis=-1, keepdims=True)
        e2 = jnp.exp(v2 - v1)
        w1 = 1.0 / (1.0 + e2)
        w2 = e2 / (1.0 + e2)
        if moe == "comb":
            comb_ref[...] = jnp.where(lane == i1, w1, 0.0) + jnp.where(lane == i2, w2, 0.0)
        else:
            tm = logits.shape[0]
            chosen = (lane == i1) | (lane == i2)
            before = (lax.broadcasted_iota(jnp.int32, (tm, tm), 0)
                      > lax.broadcasted_iota(jnp.int32, (tm, tm), 1))
            rank = dot(jnp.where(before, 1.0, 0.0).astype(BF16), jnp.where(chosen, 1.0, 0.0).astype(BF16))
            r1 = jnp.sum(jnp.where(lane == i1, rank, 0.0), axis=-1, keepdims=True).astype(jnp.int32)
            r2 = jnp.sum(jnp.where(lane == i2, rank, 0.0), axis=-1, keepdims=True).astype(jnp.int32)
            meta_i = jnp.where(lane == 0, i1, jnp.where(lane == 1, i2,
                               jnp.where(lane == 2, r1, jnp.where(lane == 3, r2, 0))))
            mi_ref[...] = meta_i[:, :META_W]
            mf_ref[...] = jnp.where(lane == 0, w1, jnp.where(lane == 1, w2, 0.0))[:, :META_W]
            counts = jnp.sum(jnp.where(chosen, 1.0, 0.0), axis=0, keepdims=True)
            cnt_ref[0] = jnp.broadcast_to(counts, cnt_ref.shape[1:]).astype(jnp.int32)


META_W = 8


def merge(x, hf, hb, gr, o, mod, g_grp, w_out, g_ffn, router, *, layer, moe_layer, row, tm, moe):
    n = x.shape[0]
    tile = lambda w: pl.BlockSpec((tm, w), lambda i: (i, 0))
    whole = lambda a: pl.BlockSpec(a.shape, lambda i: (0,) * a.ndim)
    in_specs = [tile(D_MODEL), tile(D_RNN), tile(D_RNN), tile(D_RNN), tile(D_ATTN),
                whole(mod), whole(g_grp),
                pl.BlockSpec((1, D_MODEL, D_MODEL), lambda i: (layer, 0, 0)), whole(g_ffn)]
    args = [x, hf, hb, gr, o, mod, g_grp, w_out, g_ffn]
    out_specs = [tile(D_MODEL), tile(D_MODEL)]
    out_shape = [jax.ShapeDtypeStruct((n, D_MODEL), F32), jax.ShapeDtypeStruct((n, D_MODEL), BF16)]
    if moe:
        in_specs += [pl.BlockSpec((1, D_MODEL, LANES), lambda i: (moe_layer, 0, 0))] * 2
        args += list(router)
    if moe == "comb":
        out_specs.append(tile(LANES))
        out_shape.append(jax.ShapeDtypeStruct((n, LANES), F32))
    elif moe == "route":
        out_specs += [tile(META_W), tile(META_W), pl.BlockSpec((1, SUBLANES, LANES), lambda i: (i, 0, 0))]
        out_shape += [jax.ShapeDtypeStruct((n, META_W), jnp.int32), jax.ShapeDtypeStruct((n, META_W), F32),
                      jax.ShapeDtypeStruct((n // tm, SUBLANES, LANES), jnp.int32)]
    return pl.pallas_call(
        functools.partial(_merge_kernel, row=row, moe=moe),
        grid=(n // tm,),
        in_specs=in_specs,
        out_specs=out_specs,
        out_shape=out_shape,
        compiler_params=_params("arbitrary"),
        name="merge",
    )(*args)


def _ffn_kernel(*refs, row, n_exp, nf, moe, final):
    refs = list(refs)
    h_ref, x_ref, mod_ref = refs[:3]
    pos = 3
    comb_ref = None
    if moe:
        comb_ref = refs[pos]
        pos += 1
    gfin_ref = None
    if final:
        gfin_ref = refs[pos]
        pos += 1
    wg_ref, wu_ref, wd_ref, o_ref, acc_ref = refs[pos:pos + 5]
    tot_ref = refs[pos + 5] if moe else acc_ref
    e = pl.program_id(1)
    f = pl.program_id(2)

    @pl.when(f == 0)
    def _():
        acc_ref[...] = jnp.zeros_like(acc_ref)

    if moe:
        @pl.when((e == 0) & (f == 0))
        def _():
            tot_ref[...] = jnp.zeros_like(tot_ref)

    h = h_ref[...]
    a = jnp.dot(h, wg_ref[0], preferred_element_type=F32)
    b = jnp.dot(h, wu_ref[0], preferred_element_type=F32)
    act = (a * jax.nn.sigmoid(a)) * b
    acc_ref[...] += jnp.dot(act.astype(BF16), wd_ref[0], preferred_element_type=F32)

    if moe:
        @pl.when(f == nf - 1)
        def _():
            comb = comb_ref[...]
            lane = lax.broadcasted_iota(jnp.int32, comb.shape, 1)
            ce = jnp.sum(jnp.where(lane == e, comb, 0.0), axis=-1, keepdims=True)
            tot_ref[...] += ce * acc_ref[...]

    @pl.when((e == n_exp - 1) & (f == nf - 1))
    def _():
        x = x_ref[...] + mod_ref[0, row:row + 1, GT_F:GT_F + D_MODEL] * tot_ref[...]
        if final:
            x = _rms(x) * gfin_ref[...]
        o_ref[...] = x


def ffn(h2, x, mod, comb, g_final, wg, wu, wd, *, first, n_exp, row, tm, tf):
    n = x.shape[0]
    d_ff = wg.shape[2]
    nf = d_ff // tf
    moe = comb is not None
    final = g_final is not None
    tile = lambda w: pl.BlockSpec((tm, w), lambda t, e, f: (t, 0))
    in_specs = [tile(D_MODEL), tile(D_MODEL), pl.BlockSpec(mod.shape, lambda t, e, f: (0, 0, 0))]
    args = [h2, x, mod]
    if moe:
        in_specs.append(tile(LANES))
        args.append(comb)
    if final:
        in_specs.append(pl.BlockSpec((1, D_MODEL), lambda t, e, f: (0, 0)))
        args.append(g_final)
    in_specs += [
        pl.BlockSpec((1, D_MODEL, tf), lambda t, e, f: (first + e, 0, f)),
        pl.BlockSpec((1, D_MODEL, tf), lambda t, e, f: (first + e, 0, f)),
        pl.BlockSpec((1, tf, D_MODEL), lambda t, e, f: (first + e, f, 0)),
    ]
    args += [wg, wu, wd]
    scratch = [pltpu.VMEM((tm, D_MODEL), F32)]
    if moe:
        scratch.append(pltpu.VMEM((tm, D_MODEL), F32))
    return pl.pallas_call(
        functools.partial(_ffn_kernel, row=row, n_exp=n_exp, nf=nf, moe=moe, final=final),
        grid=(n // tm, n_exp, nf),
        in_specs=in_specs,
        out_specs=tile(D_MODEL),
        out_shape=jax.ShapeDtypeStruct((n, D_MODEL), F32),
        scratch_shapes=scratch,
        compiler_params=_params("arbitrary", "arbitrary", "arbitrary"),
        name="ffn",
    )(*args)


MOE_TILE = 512
MOE_TOKENS = 512
SEG_ALIGN = 16
PERM_ROWS = 2 * MOE_TOKENS + N_EXPERTS * SEG_ALIGN
CHUNKS = (64, 32, 16)


def _block_tables(cnt_tile, meta_i, tm):
    i32 = jnp.int32
    n_tiles = cnt_tile.shape[0]
    n = meta_i.shape[0]
    cnt = cnt_tile[:, 0, :N_EXPERTS]
    c16 = (cnt + SEG_ALIGN - 1) // SEG_ALIGN * SEG_ALIGN
    seg_off = jnp.cumsum(c16, axis=1) - c16
    run = jnp.cumsum(c16, axis=0) - c16
    total = jnp.sum(c16, axis=0)
    tiles = (total + MOE_TILE - 1) // MOE_TILE
    cum = jnp.cumsum(tiles)
    start = (cum - tiles) * MOE_TILE
    dst = start[None, :] + run
    experts = jnp.arange(N_EXPERTS, dtype=i32)
    off_tok = jnp.repeat(seg_off, tm, axis=0)
    pick = lambda e: jnp.sum(jnp.where(e[:, None] == experts[None, :], off_tok, 0), axis=1)
    d1 = (meta_i[:, 2] + pick(meta_i[:, 0])).astype(i32)
    d2 = (meta_i[:, 3] + pick(meta_i[:, 1])).astype(i32)
    d_rows = jnp.pad(jnp.stack([d1.reshape(n_tiles, tm), d2.reshape(n_tiles, tm)], axis=1),
                     ((0, 0), (0, SUBLANES - 2), (0, 0)))
    d_cols = jnp.pad(jnp.stack([d1, d2], axis=1), ((0, 0), (0, META_W - 2)))
    n_pad = _moe_rows(n)
    n_used = cum[-1]
    t = jnp.arange(n_pad // MOE_TILE, dtype=i32)
    te = jnp.sum((t[:, None] >= cum[None, :]).astype(i32), axis=1)
    used = t < n_used
    tile_expert = jnp.where(used, te, jnp.max(jnp.where(used, te, 0))).astype(i32)
    flat = lambda a: a.reshape(-1).astype(i32)
    return dict(seg_off=flat(seg_off), c16=flat(c16), dst=flat(dst), zero_start=(start + total).astype(i32),
                zero_len=(tiles * MOE_TILE - total).astype(i32), n_used=n_used.reshape(1).astype(i32),
                tile_expert=tile_expert, d_rows=d_rows, d_cols=d_cols)


def _moe_rows(n):
    worst = 2 * n + (n // MOE_TOKENS) * N_EXPERTS * SEG_ALIGN + N_EXPERTS * MOE_TILE
    return (worst + MOE_TILE - 1) // MOE_TILE * MOE_TILE


def _segment_copies(src, dst, src_off, dst_off, rows, sems, act):
    big, mid, small = CHUNKS
    n_big = lax.shift_right_logical(rows, big.bit_length() - 1)

    def copy(size, k_src, k_dst, sem):
        return pltpu.make_async_copy(src.at[pl.ds(pl.multiple_of(k_src, SEG_ALIGN), size)],
                                     dst.at[pl.ds(pl.multiple_of(k_dst, SEG_ALIGN), size)], sem)

    def body(k, c):
        act(copy(big, src_off + k * big, dst_off + k * big, sems.at[0]))
        return c

    lax.fori_loop(0, n_big, body, 0)
    done = n_big * big

    @pl.when((rows & mid) != 0)
    def _():
        act(copy(mid, src_off + done, dst_off + done, sems.at[1]))

    @pl.when((rows & small) != 0)
    def _():
        act(copy(small, src_off + done + (rows & mid), dst_off + done + (rows & mid), sems.at[2]))


ZERO_SIZES = (256, 128, 64, 32, 16)


def _zero_fill(zero_ref, dst, start, rows, sems, act):
    done = 0 * rows
    for s, size in enumerate(ZERO_SIZES):
        @pl.when((rows & size) != 0)
        def _(s=s, size=size, done=done):
            act(pltpu.make_async_copy(zero_ref.at[pl.ds(0, size)],
                                      dst.at[pl.ds(pl.multiple_of(start + done, SEG_ALIGN), size)], sems.at[s]))
        done = done + (rows & size)


def _start(copy):
    copy.start()


def _wait(copy):
    copy.wait()


def _dispatch_kernel(so_ref, c16_ref, dst_ref, zs_ref, zl_ref, nu_ref, h_ref, d_ref, x_hbm,
                      z_ref, zero_ref, sems, zsems, *, tm, n_tiles, n_pad):
    i = pl.program_id(0)
    d1 = d_ref[0, 0:1, :]
    d2 = d_ref[0, 1:2, :]
    r = lax.broadcasted_iota(jnp.int32, (PERM_ROWS, tm), 0)
    perm = jnp.where((r == d1) | (r == d2), 1.0, 0.0).astype(BF16)
    slot = i % 2
    z_ref[slot] = jnp.dot(perm, h_ref[...], preferred_element_type=F32).astype(BF16)

    def move(tile, buf, act):
        for e in range(N_EXPERTS):
            k = tile * N_EXPERTS + e
            _segment_copies(z_ref.at[buf], x_hbm, so_ref[k], dst_ref[k], c16_ref[k], sems.at[buf], act)

    @pl.when(i > 0)
    def _():
        move(i - 1, 1 - slot, _wait)

    move(i, slot, _start)

    @pl.when(i == n_tiles - 1)
    def _():
        move(i, slot, _wait)
        zero_ref[...] = jnp.zeros_like(zero_ref)
        for act in (_start, _wait):
            for e in range(N_EXPERTS):
                _zero_fill(zero_ref, x_hbm, zs_ref[e], zl_ref[e], zsems, act)

            def tail(t, c):
                act(pltpu.make_async_copy(zero_ref, x_hbm.at[pl.ds(pl.multiple_of(t * MOE_TILE, MOE_TILE),
                                                                   MOE_TILE)], zsems.at[len(ZERO_SIZES)]))
                return c

            lax.fori_loop(nu_ref[0], n_pad // MOE_TILE, tail, 0)


def moe_dispatch(tab, h2, *, tm):
    n = h2.shape[0]
    n_tiles = n // tm
    n_pad = _moe_rows(n)
    return pl.pallas_call(
        functools.partial(_dispatch_kernel, tm=tm, n_tiles=n_tiles, n_pad=n_pad),
        grid_spec=pltpu.PrefetchScalarGridSpec(
            num_scalar_prefetch=6,
            grid=(n_tiles,),
            in_specs=[pl.BlockSpec((tm, D_MODEL), lambda i, *_: (i, 0)),
                      pl.BlockSpec((1, SUBLANES, tm), lambda i, *_: (i, 0, 0))],
            out_specs=pl.BlockSpec(memory_space=pl.ANY),
            scratch_shapes=[pltpu.VMEM((2, PERM_ROWS, D_MODEL), BF16), pltpu.VMEM((MOE_TILE, D_MODEL), BF16),
                            pltpu.SemaphoreType.DMA((2, len(CHUNKS))),
                            pltpu.SemaphoreType.DMA((len(ZERO_SIZES) + 1,))],
        ),
        out_shape=jax.ShapeDtypeStruct((n_pad, D_MODEL), BF16),
        compiler_params=_params("arbitrary"),
        name="moe_dispatch",
    )(tab["seg_off"], tab["c16"], tab["dst"], tab["zero_start"], tab["zero_len"], tab["n_used"],
      h2, tab["d_rows"])


def _grouped_ffn_kernel(te_ref, nt_ref, x_ref, wg_ref, wu_ref, wd_ref, y_ref):
    t = pl.program_id(0)

    @pl.when(t < nt_ref[0])
    def _():
        x = x_ref[...]
        a = jnp.dot(x, wg_ref[0], preferred_element_type=F32)
        b = jnp.dot(x, wu_ref[0], preferred_element_type=F32)
        act = (a * jax.nn.sigmoid(a)) * b
        y_ref[...] = jnp.dot(act.astype(BF16), wd_ref[0], preferred_element_type=F32).astype(y_ref.dtype)

    @pl.when(t >= nt_ref[0])
    def _():
        y_ref[...] = jnp.zeros_like(y_ref)


def grouped_ffn(tile_expert, n_used, xs, wg, wu, wd, *, first):
    n_pad = xs.shape[0]
    d_ff = wg.shape[2]
    rows = pl.BlockSpec((MOE_TILE, D_MODEL), lambda t, te, nt: (t, 0))
    return pl.pallas_call(
        _grouped_ffn_kernel,
        grid_spec=pltpu.PrefetchScalarGridSpec(
            num_scalar_prefetch=2,
            grid=(n_pad // MOE_TILE,),
            in_specs=[rows,
                      pl.BlockSpec((1, D_MODEL, d_ff), lambda t, te, nt: (first + te[t], 0, 0)),
                      pl.BlockSpec((1, D_MODEL, d_ff), lambda t, te, nt: (first + te[t], 0, 0)),
                      pl.BlockSpec((1, d_ff, D_MODEL), lambda t, te, nt: (first + te[t], 0, 0))],
            out_specs=rows,
        ),
        out_shape=jax.ShapeDtypeStruct((n_pad, D_MODEL), BF16),
        compiler_params=_params("arbitrary"),
        name="grouped_ffn",
    )(tile_expert, n_used, xs, wg, wu, wd)


def _combine_kernel(*refs, row, tm, final):
    if final:
        so_ref, c16_ref, dst_ref, x_ref, dc_ref, mf_ref, mod_ref, gfin_ref, y_hbm, o_ref, yt_ref, sems = refs
    else:
        so_ref, c16_ref, dst_ref, x_ref, dc_ref, mf_ref, mod_ref, y_hbm, o_ref, yt_ref, sems = refs
    i = pl.program_id(0)
    n_steps = pl.num_programs(0)
    slot = i % 2

    def fetch(tile, buf, act):
        for e in range(N_EXPERTS):
            k = tile * N_EXPERTS + e
            _segment_copies(y_hbm, yt_ref.at[buf], dst_ref[k], so_ref[k], c16_ref[k], sems.at[buf], act)

    @pl.when(i == 0)
    def _():
        yt_ref[...] = jnp.zeros_like(yt_ref)
        fetch(0, 0, _start)

    fetch(i, slot, _wait)

    @pl.when(i + 1 < n_steps)
    def _():
        fetch(i + 1, 1 - slot, _start)

    yt = yt_ref[slot]
    r = lax.broadcasted_iota(jnp.int32, (tm, PERM_ROWS), 1)
    pick = lambda col: jnp.dot(jnp.where(r == dc_ref[:, col:col + 1], 1.0, 0.0).astype(BF16), yt,
                               preferred_element_type=F32)
    tot = mf_ref[:, 0:1] * pick(0) + mf_ref[:, 1:2] * pick(1)
    x = x_ref[...] + mod_ref[0, row:row + 1, GT_F:GT_F + D_MODEL] * tot
    if final:
        x = _rms(x) * gfin_ref[...]
    o_ref[...] = x


def moe_combine(tab, x, meta_f, mod, g_final, ys, *, row, tm):
    n = x.shape[0]
    final = g_final is not None
    tile = lambda w: pl.BlockSpec((tm, w), lambda i, *_: (i, 0))
    in_specs = [tile(D_MODEL), tile(META_W), tile(META_W), pl.BlockSpec(mod.shape, lambda i, *_: (0, 0, 0))]
    args = [x, tab["d_cols"], meta_f, mod]
    if final:
        in_specs.append(pl.BlockSpec((1, D_MODEL), lambda i, *_: (0, 0)))
        args.append(g_final)
    in_specs.append(pl.BlockSpec(memory_space=pl.ANY))
    args.append(ys)
    return pl.pallas_call(
        functools.partial(_combine_kernel, row=row, tm=tm, final=final),
        grid_spec=pltpu.PrefetchScalarGridSpec(
            num_scalar_prefetch=3,
            grid=(n // tm,),
            in_specs=in_specs,
            out_specs=tile(D_MODEL),
            scratch_shapes=[pltpu.VMEM((2, PERM_ROWS, D_MODEL), BF16),
                            pltpu.SemaphoreType.DMA((2, len(CHUNKS)))],
        ),
        out_shape=jax.ShapeDtypeStruct((n, D_MODEL), F32),
        compiler_params=_params("arbitrary"),
        name="moe_combine",
    )(tab["seg_off"], tab["c16"], tab["dst"], *args)


def _gate_weights(w_rg):
    depth = w_rg.shape[0]
    per = RNN_CHUNK // RNN_BLOCK
    nch = D_RNN // RNN_CHUNK
    wr = w_rg.astype(BF16).reshape(depth, 2, 2, nch, per, RNN_BLOCK, 1, RNN_BLOCK)
    on_diag = jnp.eye(per, dtype=bool).reshape(per, 1, per, 1)
    bd = jnp.where(on_diag, wr, jnp.zeros((), BF16))
    bd = bd.reshape(depth, 2, 2, nch, RNN_CHUNK, RNN_CHUNK)
    return jnp.concatenate([bd[:, :, 0], bd[:, :, 1]], axis=-1)


def _tile_rows(n, want):
    return want if n % want == 0 else n


def kernel(x, c, ctx, c_ctx, w_mod, b_mod, g_mix, g_ffn, w_in, conv_w, conv_b, w_rg, b_rg, lam,
           sink, g_grp, w_out, w_ffn_gate, w_ffn_up, w_ffn_down, w_router, w_exp_gate, w_exp_up,
           w_exp_down, g_final):
    assert x.shape[0] == 1 and ctx.shape[0] == 1
    depth = w_mod.shape[0]
    seq, n_ctx = x.shape[1], ctx.shape[1]
    xl, xc = x[0], ctx[0]

    cond8 = jnp.zeros((SUBLANES, D_MODEL), F32).at[0].set(c[0]).at[1].set(c_ctx)
    mods = adaln_all(cond8, w_mod, b_mod)
    cos, sin = rope_tables(seq)

    w_in_b = w_in.astype(BF16)
    w_out_b = w_out.astype(BF16)
    wg_gate = _gate_weights(w_rg)
    ffn_w = (w_ffn_gate.astype(BF16), w_ffn_up.astype(BF16), w_ffn_down.astype(BF16))
    exp_w = tuple(w.astype(BF16).reshape((-1,) + w.shape[2:]) for w in (w_exp_gate, w_exp_up, w_exp_down))
    wr_pad = jnp.pad(w_router, ((0, 0), (0, 0), (0, LANES - N_EXPERTS)))
    wr_hi = wr_pad.astype(BF16)
    wr_lo = (wr_pad - wr_hi.astype(F32)).astype(BF16)
    zeros_h0 = jnp.zeros((SUBLANES, D_RNN), F32)

    tm = _tile_rows(seq, 512)
    tm_ffn = _tile_rows(seq, 1024)

    for l in range(depth):
        last = l == depth - 1
        moe = l % 2 == 1
        j = l // 2
        mod = mods[l:l + 1]
        gm, gf, gg = g_mix[l:l + 1], g_ffn[l:l + 1], g_grp[l:l + 1]

        xr_c, gr_c, q_c, kv_c = in_proj(xc, mod, gm, w_in_b, None, None, layer=l, row=1, tm=n_ctx)
        xr_l, gr_l, q_l, kv_l = in_proj(xl, mod, gm, w_in_b, cos, sin, layer=l, row=0, tm=tm)

        rnn_args = (conv_w[l], conv_b[l:l + 1], wg_gate, b_rg[l], lam[l])
        hf_c, hb_c, h_ctx_end = rnn_bidir(xr_c, *rnn_args, zeros_h0, layer=l, tm=n_ctx)
        hf_l, hb_l, _ = rnn_bidir(xr_l, *rnn_args, h_ctx_end, layer=l, tm=tm)

        o_l = attention(q_l, kv_l, kv_c, sink[l], band=True)

        router = (wr_hi, wr_lo) if moe else None
        weights = exp_w if moe else ffn_w
        first = j * N_EXPERTS if moe else j
        tf = 512
        merge_l = functools.partial(merge, layer=l, moe_layer=j)

        g_fin = g_final.reshape(1, D_MODEL) if last else None
        if moe:
            xl, h2, meta_i, meta_f, cnt = merge_l(xl, hf_l, hb_l, gr_l, o_l, mod, gg, w_out_b, gf, router,
                                                  row=0, tm=tm, moe="route")
            assert tm == MOE_TOKENS
            tab = _block_tables(cnt, meta_i, tm)
            xs = moe_dispatch(tab, h2, tm=tm)
            ys = grouped_ffn(tab["tile_expert"], tab["n_used"], xs, *weights, first=first)
            xl = moe_combine(tab, xl, meta_f, mod, g_fin, ys, row=0, tm=tm)
        else:
            xl, h2 = merge_l(xl, hf_l, hb_l, gr_l, o_l, mod, gg, w_out_b, gf, None, row=0, tm=tm, moe=None)
            xl = ffn(h2, xl, mod, None, g_fin, *weights, first=first, n_exp=1, row=0, tm=tm_ffn, tf=tf)

        if not last:
            o_c = attention(q_c, kv_c, kv_c, sink[l], band=False)
            res = merge_l(xc, hf_c, hb_c, gr_c, o_c, mod, gg, w_out_b, gf, router, row=1, tm=n_ctx,
                          moe="comb" if moe else None)
            xc, h2c = res[0], res[1]
            comb_c = res[2] if moe else None
            xc = ffn(h2c, xc, mod, comb_c, None, *weights, first=first, n_exp=N_EXPERTS if moe else 1,
                     row=1, tm=n_ctx, tf=tf)

    return xl[None]
```

```python
import functools

import jax
import jax.numpy as jnp
from jax import lax
from jax.experimental import pallas as pl
from jax.experimental.pallas import tpu as pltpu

F32 = jnp.float32
BF16 = jnp.bfloat16

D_MODEL = 1024
D_RNN = 512
D_ATTN = 512
D_KV = 128
HEAD_DIM = 64
N_HEADS = 8
N_KV_HEADS = 2
HEADS_PER_KV = N_HEADS // N_KV_HEADS
D_IN = 2 * D_RNN + D_ATTN + 2 * D_KV
RNN_BLOCK = 64
RNN_CHUNK = 256
CONV_W = 4
LRU_C = 8.0
WINDOW = 128
BLOCK_Q = 128
GRID_W = 64
ROPE_BASE = 10000.0
N_EXPERTS = 8
EPS = 1e-6
NEG_INF = -1e30

LANES = 128
SUBLANES = 8
VMEM_LIMIT = 56 * 1024 * 1024

SH_M, SC_M, GT_M, SH_F, SC_F, GT_F = (k * D_MODEL for k in range(6))


def _params(*sem):
    return pltpu.CompilerParams(dimension_semantics=sem, vmem_limit_bytes=VMEM_LIMIT)


def _rms(x):
    return x * lax.rsqrt(jnp.mean(x * x, axis=-1, keepdims=True) + EPS)


SUB_ROWS = 256


def _row_blocks(tm):
    step = SUB_ROWS if tm % SUB_ROWS == 0 else tm
    return [slice(r, r + step) for r in range(0, tm, step)]


def _adaln_kernel(c_ref, w_ref, b_ref, o_ref):
    c = c_ref[...]
    s = c * jax.nn.sigmoid(c)
    w = w_ref[0]
    s_hi, w_hi = s.astype(BF16), w.astype(BF16)
    s_lo = (s - s_hi.astype(F32)).astype(BF16)
    w_lo = (w - w_hi.astype(F32)).astype(BF16)
    dot = lambda a, b: jnp.dot(a, b, preferred_element_type=F32)
    o_ref[0] = dot(s_hi, w_hi) + (dot(s_hi, w_lo) + dot(s_lo, w_hi)) + b_ref[0]


def adaln_all(cond8, w_mod, b_mod):
    depth = w_mod.shape[0]
    nc = 1536
    return pl.pallas_call(
        _adaln_kernel,
        grid=(depth, 6 * D_MODEL // nc),
        in_specs=[
            pl.BlockSpec((SUBLANES, D_MODEL), lambda l, j: (0, 0)),
            pl.BlockSpec((1, D_MODEL, nc), lambda l, j: (l, 0, j)),
            pl.BlockSpec((1, 1, nc), lambda l, j: (l, 0, j)),
        ],
        out_specs=pl.BlockSpec((1, SUBLANES, nc), lambda l, j: (l, 0, j)),
        out_shape=jax.ShapeDtypeStruct((depth, SUBLANES, 6 * D_MODEL), F32),
        compiler_params=_params("arbitrary", "arbitrary"),
        name="adaln",
    )(cond8, w_mod, b_mod.reshape(depth, 1, 6 * D_MODEL))


ROPE_ROWS = SUBLANES * GRID_W


def _rope_kernel(f_ref, c_ref, s_ref):
    i = pl.program_id(0)
    lane = lax.broadcasted_iota(jnp.int32, (GRID_W, LANES), 1)
    col_axis = ((lane >> 5) & 1) == 1
    sign = jnp.where(((lane >> 4) & 1) == 0, -1.0, 1.0).astype(F32)
    freq = f_ref[...]
    rowpos = (i * SUBLANES + lax.broadcasted_iota(jnp.int32, (SUBLANES, LANES), 0)).astype(F32)
    colpos = lax.broadcasted_iota(jnp.int32, (GRID_W, LANES), 0).astype(F32)
    ang_r = rowpos * freq
    ang_c = colpos * freq
    cr, sr = jnp.cos(ang_r), jnp.sin(ang_r)
    cc, sc = jnp.cos(ang_c), jnp.sin(ang_c) * sign
    for g in range(SUBLANES):
        rows = slice(g * GRID_W, (g + 1) * GRID_W)
        c_ref[rows, :] = jnp.where(col_axis, cc, jnp.broadcast_to(cr[g:g + 1, :], (GRID_W, LANES)))
        s_ref[rows, :] = jnp.where(col_axis, sc, jnp.broadcast_to(sr[g:g + 1, :], (GRID_W, LANES)) * sign)


def rope_tables(seq):
    axis_dim = HEAD_DIM // 2
    freqs = ROPE_BASE ** (-jnp.arange(0, axis_dim, 2, dtype=F32) / axis_dim)
    freq_lane = jnp.tile(freqs, LANES // freqs.shape[0]).reshape(1, LANES)
    return pl.pallas_call(
        _rope_kernel,
        grid=(seq // ROPE_ROWS,),
        in_specs=[pl.BlockSpec((1, LANES), lambda i: (0, 0))],
        out_specs=[pl.BlockSpec((ROPE_ROWS, LANES), lambda i: (i, 0))] * 2,
        out_shape=[jax.ShapeDtypeStruct((seq, LANES), F32)] * 2,
        compiler_params=_params("arbitrary"),
        name="rope_tables",
    )(freq_lane)


def _rope_chunk(x, cos, sin):
    lane = lax.broadcasted_iota(jnp.int32, x.shape, 1)
    first_half = ((lane >> 4) & 1) == 0
    partner = jnp.where(first_half, pltpu.roll(x, LANES - 16, axis=1), pltpu.roll(x, 16, axis=1))
    return x * cos + partner * sin


def _in_proj_kernel(*refs, row, rope):
    if rope:
        x_ref, mod_ref, g_ref, w_ref, cos_ref, sin_ref, xr_ref, gr_ref, q_ref, kv_ref = refs
    else:
        x_ref, mod_ref, g_ref, w_ref, xr_ref, gr_ref, q_ref, kv_ref = refs
    shift = mod_ref[0, row:row + 1, SH_M:SH_M + D_MODEL]
    scale = mod_ref[0, row:row + 1, SC_M:SC_M + D_MODEL]
    q0 = 2 * D_RNN
    for rows in _row_blocks(x_ref.shape[0]):
        x = x_ref[rows, :]
        h = (_rms(x) * g_ref[...]) * (1.0 + scale) + shift
        p = jnp.dot(h.astype(BF16), w_ref[0], preferred_element_type=F32)
        xr_ref[rows, :] = p[:, 0:D_RNN]
        gr_ref[rows, :] = p[:, D_RNN:2 * D_RNN].astype(gr_ref.dtype)
        if rope:
            cos, sin = cos_ref[rows, :], sin_ref[rows, :]
        for c in range(D_ATTN // LANES):
            qc = p[:, q0 + c * LANES:q0 + (c + 1) * LANES]
            if rope:
                qc = _rope_chunk(qc, cos, sin)
            q_ref[rows, c * LANES:(c + 1) * LANES] = (qc * (HEAD_DIM ** -0.5)).astype(BF16)
        k = p[:, q0 + D_ATTN:q0 + D_ATTN + D_KV]
        v = p[:, q0 + D_ATTN + D_KV:q0 + D_ATTN + 2 * D_KV]
        if rope:
            k = _rope_chunk(k, cos, sin)
        kv_ref[rows, 0:LANES] = k.astype(BF16)
        kv_ref[rows, LANES:2 * LANES] = pltpu.roll(k, HEAD_DIM, axis=1).astype(BF16)
        kv_ref[rows, 2 * LANES:3 * LANES] = v.astype(BF16)
        kv_ref[rows, 3 * LANES:4 * LANES] = pltpu.roll(v, HEAD_DIM, axis=1).astype(BF16)


def in_proj(x, mod, g_mix, w_in, cos, sin, *, layer, row, tm):
    n = x.shape[0]
    rope = cos is not None
    tile = lambda w: pl.BlockSpec((tm, w), lambda i: (i, 0))
    in_specs = [
        tile(D_MODEL),
        pl.BlockSpec((1, SUBLANES, 6 * D_MODEL), lambda i: (0, 0, 0)),
        pl.BlockSpec((1, D_MODEL), lambda i: (0, 0)),
        pl.BlockSpec((1, D_MODEL, D_IN), lambda i: (layer, 0, 0)),
    ]
    args = [x, mod, g_mix, w_in]
    if rope:
        in_specs += [tile(LANES), tile(LANES)]
        args += [cos, sin]
    return pl.pallas_call(
        functools.partial(_in_proj_kernel, row=row, rope=rope),
        grid=(n // tm,),
        in_specs=in_specs,
        out_specs=[tile(D_RNN), tile(D_RNN), tile(D_ATTN), tile(4 * LANES)],
        out_shape=[
            jax.ShapeDtypeStruct((n, D_RNN), F32),
            jax.ShapeDtypeStruct((n, D_RNN), BF16),
            jax.ShapeDtypeStruct((n, D_ATTN), BF16),
            jax.ShapeDtypeStruct((n, 4 * LANES), BF16),
        ],
        compiler_params=_params("arbitrary"),
        name="in_proj",
    )(*args)


RNN_SLABS = D_RNN // LANES
SEG_PAD = 4


def _rnn_kernel(pf_ref, mf_ref, nf_ref, pb_ref, mb_ref, nb_ref, cw_ref, cb_ref, wg_ref, bg_ref,
                lam_ref, h0_ref, hf_ref, hb_ref, hl_ref, ext_ref, xs_ref, u_ref, a_ref, b_ref, hs_ref,
                cf_ref, cbk_ref, *, tm, nt):
    i = pl.program_id(0)
    seg = tm // SUBLANES
    pitch = seg + SEG_PAD
    slab = lambda c: slice(c * LANES, (c + 1) * LANES)

    @pl.when(i == 0)
    def _():
        cf_ref[...] = jnp.broadcast_to(h0_ref[0:1, :], (SUBLANES, D_RNN))
        cbk_ref[...] = jnp.broadcast_to(h0_ref[1:2, :], (SUBLANES, D_RNN))

    softplus_neg_lam = jax.nn.softplus(-lam_ref[...])
    seg_id = lax.broadcasted_iota(jnp.int32, (SUBLANES, D_RNN), 0)

    def run(d, prev_ref, main_ref, next_ref, at_start, at_end, out_ref, carry_ref):
        ext_ref[0:SUBLANES, :] = jnp.where(at_start, 0.0, prev_ref[...])
        ext_ref[SUBLANES:tm + SUBLANES, :] = main_ref[...]
        ext_ref[tm + SUBLANES:tm + 2 * SUBLANES, :] = jnp.where(at_end, 0.0, next_ref[...])
        halo = SUBLANES - CONV_W // 2
        for j in range(SUBLANES):
            for c in range(RNN_SLABS):
                xs_ref[c, j * pitch:j * pitch + seg + SUBLANES, :] = (
                    ext_ref[j * seg + halo:j * seg + halo + seg + SUBLANES, slab(c)])
        for g in range(seg):
            for c in range(RNN_SLABS):
                acc = cb_ref[:, slab(c)]
                for k in range(CONV_W):
                    acc = acc + xs_ref[c, pl.ds(g + k, SUBLANES, stride=pitch), :] * cw_ref[k:k + 1, slab(c)]
                u_ref[g * SUBLANES:(g + 1) * SUBLANES, slab(c)] = acc
        u = u_ref[...]
        ub = u.astype(BF16)
        for c in range(D_RNN // RNN_CHUNK):
            ch = slice(c * RNN_CHUNK, (c + 1) * RNN_CHUNK)
            g = jnp.dot(ub[:, ch], wg_ref[0, d, c], preferred_element_type=F32)
            r = jax.nn.sigmoid(g[:, :RNN_CHUNK] + bg_ref[d, 0:1, ch])
            ig = jax.nn.sigmoid(g[:, RNN_CHUNK:] + bg_ref[d, 1:2, ch])
            log_a = (-LRU_C * r) * softplus_neg_lam[d:d + 1, ch]
            a = jnp.exp(log_a)
            mult = jnp.sqrt(jnp.maximum(1.0 - a * a, 0.0))
            a_ref[:, ch] = a
            b_ref[:, ch] = mult * (ig * u[:, ch])

        def scan(n, hp):
            g = n if d == 0 else seg - 1 - n
            off = pl.multiple_of(g * SUBLANES, SUBLANES)
            a_g = a_ref[pl.ds(off, SUBLANES), :]
            h = a_g * hp[0] + b_ref[pl.ds(off, SUBLANES), :]
            p = a_g * hp[1]
            b_ref[pl.ds(off, SUBLANES), :] = h
            a_ref[pl.ds(off, SUBLANES), :] = p
            return h, p

        zeros = jnp.zeros((SUBLANES, D_RNN), F32)
        h_end, p_end = lax.fori_loop(0, seg, scan, (zeros, zeros + 1.0), unroll=8)

        carry = carry_ref[...]
        enter = zeros
        for n in range(SUBLANES):
            j = n if d == 0 else SUBLANES - 1 - n
            enter = jnp.where(seg_id == j, carry, enter)
            carry = jnp.broadcast_to(h_end[j:j + 1, :] + p_end[j:j + 1, :] * carry[0:1, :], (SUBLANES, D_RNN))
        carry_ref[...] = carry

        for g in range(seg):
            rows = slice(g * SUBLANES, (g + 1) * SUBLANES)
            h = b_ref[rows, :] + a_ref[rows, :] * enter
            for c in range(RNN_SLABS):
                hs_ref[c, pl.ds(g, SUBLANES, stride=pitch), :] = h[:, slab(c)]
        for j in range(SUBLANES):
            for c in range(RNN_SLABS):
                out_ref[j * seg:(j + 1) * seg, slab(c)] = (
                    hs_ref[c, j * pitch:j * pitch + seg, :].astype(out_ref.dtype))

    run(0, pf_ref, mf_ref, nf_ref, i == 0, i == nt - 1, hf_ref, cf_ref)
    run(1, pb_ref, mb_ref, nb_ref, i == nt - 1, i == 0, hb_ref, cbk_ref)

    @pl.when(i == nt - 1)
    def _():
        rows = lax.broadcasted_iota(jnp.int32, (SUBLANES, D_RNN), 0)
        hl_ref[...] = jnp.where(rows == 0, cf_ref[...], jnp.where(rows == 1, cbk_ref[...], 0.0))


def rnn_bidir(xr, conv_w, conv_b, wg, bg, lam, h0, *, layer, tm):
    n = xr.shape[0]
    nt = n // tm
    per = tm // SUBLANES
    last8 = n // SUBLANES - 1
    fwd = lambda i: i
    bwd = lambda i: nt - 1 - i
    main = lambda t: pl.BlockSpec((tm, D_RNN), lambda i: (t(i), 0))
    prev = lambda t: pl.BlockSpec((SUBLANES, D_RNN), lambda i: (jnp.maximum(t(i) * per - 1, 0), 0))
    nxt = lambda t: pl.BlockSpec((SUBLANES, D_RNN), lambda i: (jnp.minimum((t(i) + 1) * per, last8), 0))
    whole = lambda a: pl.BlockSpec(a.shape, lambda i: (0,) * a.ndim)
    return pl.pallas_call(
        functools.partial(_rnn_kernel, tm=tm, nt=nt),
        grid=(nt,),
        in_specs=[prev(fwd), main(fwd), nxt(fwd), prev(bwd), main(bwd), nxt(bwd),
                  whole(conv_w), whole(conv_b),
                  pl.BlockSpec((1,) + wg.shape[1:], lambda i: (layer, 0, 0, 0, 0)),
                  whole(bg), whole(lam), whole(h0)],
        out_specs=[main(fwd), main(bwd), pl.BlockSpec((SUBLANES, D_RNN), lambda i: (0, 0))],
        out_shape=[
            jax.ShapeDtypeStruct((n, D_RNN), BF16),
            jax.ShapeDtypeStruct((n, D_RNN), BF16),
            jax.ShapeDtypeStruct((SUBLANES, D_RNN), F32),
        ],
        scratch_shapes=[
            pltpu.VMEM((tm + 2 * SUBLANES, D_RNN), F32),
            pltpu.VMEM((RNN_SLABS, tm + SUBLANES * SEG_PAD + SUBLANES, LANES), F32),
            pltpu.VMEM((tm, D_RNN), F32),
            pltpu.VMEM((tm, D_RNN), F32),
            pltpu.VMEM((tm, D_RNN), F32),
            pltpu.VMEM((RNN_SLABS, tm + SUBLANES * SEG_PAD, LANES), F32),
            pltpu.VMEM((SUBLANES, D_RNN), F32),
            pltpu.VMEM((SUBLANES, D_RNN), F32),
        ],
        compiler_params=_params("arbitrary"),
        name="rnn_bidir",
    )(xr, xr, xr, xr, xr, xr, conv_w, conv_b, wg, bg, lam, h0)


def _attn_kernel(*refs, nb, band, n_ctx):
    if band:
        sink_ref, q_ref, kvx_ref, kvp_ref, kvc_ref, kvn_ref, o_ref = refs
        kv = jnp.concatenate([kvx_ref[...], kvp_ref[...], kvc_ref[...], kvn_ref[...]], axis=0)
    else:
        sink_ref, q_ref, kvx_ref, o_ref = refs
        kv = kvx_ref[...]
    i = pl.program_id(0)
    nk = kv.shape[0]
    lane = lax.broadcasted_iota(jnp.int32, (nk, LANES), 1)
    low = lane < HEAD_DIM
    zero = jnp.zeros((), BF16)
    k, ks, v, vs = (kv[:, j * LANES:(j + 1) * LANES] for j in range(4))
    k_low = (jnp.where(low, k, zero), jnp.where(low, ks, zero))
    k_high = (jnp.where(low, zero, ks), jnp.where(low, zero, k))
    v_low = (jnp.where(low, v, zero), jnp.where(low, vs, zero))
    v_high = (jnp.where(low, zero, vs), jnp.where(low, zero, v))

    rows2 = 2 * BLOCK_Q
    r = lax.broadcasted_iota(jnp.int32, (rows2, nk), 0) & (BLOCK_Q - 1)
    if band:
        jb = lax.broadcasted_iota(jnp.int32, (rows2, nk), 1) - n_ctx
        valid = (jb < 0) | ((jb >= r) & (jb <= r + 2 * WINDOW)
                            & ((jb >= BLOCK_Q) | (i > 0)) & ((jb < 2 * BLOCK_Q) | (i < nb - 1)))
    upper = lax.broadcasted_iota(jnp.int32, (rows2, 1), 0) >= BLOCK_Q

    def probs(q2, kmat, sink_a, sink_b):
        s = lax.dot_general(q2, kmat, (((1,), (1,)), ((), ())), preferred_element_type=F32)
        if band:
            s = jnp.where(valid, s, NEG_INF)
        sink = jnp.where(upper, sink_b, sink_a)
        m = jnp.maximum(jnp.max(s, axis=-1, keepdims=True), sink)
        p = jnp.exp(s - m)
        den = jnp.sum(p, axis=-1, keepdims=True) + jnp.exp(sink - m)
        return p.astype(BF16), 1.0 / den

    for hk in range(N_KV_HEADS):
        c0, c1 = 2 * hk, 2 * hk + 1
        q2 = jnp.concatenate([q_ref[:, c0 * LANES:(c0 + 1) * LANES],
                              q_ref[:, c1 * LANES:(c1 + 1) * LANES]], axis=0)
        h0 = hk * HEADS_PER_KV
        p_low, inv_low = probs(q2, k_low[hk], sink_ref[h0], sink_ref[h0 + 2])
        p_high, inv_high = probs(q2, k_high[hk], sink_ref[h0 + 1], sink_ref[h0 + 3])
        o2 = (jnp.dot(p_low, v_low[hk], preferred_element_type=F32) * inv_low
              + jnp.dot(p_high, v_high[hk], preferred_element_type=F32) * inv_high)
        o_ref[:, c0 * LANES:(c0 + 1) * LANES] = o2[:BLOCK_Q].astype(o_ref.dtype)
        o_ref[:, c1 * LANES:(c1 + 1) * LANES] = o2[BLOCK_Q:].astype(o_ref.dtype)


def attention(q, kv, kv_ctx, sink, *, band):
    n = q.shape[0]
    nb = n // BLOCK_Q
    n_ctx = kv_ctx.shape[0]
    in_specs = [
        pl.BlockSpec(memory_space=pltpu.SMEM),
        pl.BlockSpec((BLOCK_Q, D_ATTN), lambda i: (i, 0)),
        pl.BlockSpec((n_ctx, 4 * LANES), lambda i: (0, 0)),
    ]
    args = [sink, q, kv_ctx]
    if band:
        blk = lambda f: pl.BlockSpec((BLOCK_Q, 4 * LANES), lambda i: (f(i), 0))
        in_specs += [blk(lambda i: jnp.maximum(i - 1, 0)), blk(lambda i: i),
                     blk(lambda i: jnp.minimum(i + 1, nb - 1))]
        args += [kv, kv, kv]
    return pl.pallas_call(
        functools.partial(_attn_kernel, nb=nb, band=band, n_ctx=n_ctx),
        grid=(nb,),
        in_specs=in_specs,
        out_specs=pl.BlockSpec((BLOCK_Q, D_ATTN), lambda i: (i, 0)),
        out_shape=jax.ShapeDtypeStruct((n, D_ATTN), BF16),
        compiler_params=_params("arbitrary"),
        name="attention",
    )(*args)


def _merge_kernel(*refs, row, moe):
    if moe == "comb":
        (x_ref, hf_ref, hb_ref, gr_ref, o_ref, mod_ref, gg_ref, wo_ref, gf_ref, wrh_ref, wrl_ref,
         xo_ref, h2_ref, comb_ref) = refs
    elif moe == "route":
        (x_ref, hf_ref, hb_ref, gr_ref, o_ref, mod_ref, gg_ref, wo_ref, gf_ref, wrh_ref, wrl_ref,
         xo_ref, h2_ref, mi_ref, mf_ref, cnt_ref) = refs
    else:
        x_ref, hf_ref, hb_ref, gr_ref, o_ref, mod_ref, gg_ref, wo_ref, gf_ref, xo_ref, h2_ref = refs
    mod = lambda off: mod_ref[0, row:row + 1, off:off + D_MODEL]
    dot = lambda a, b: jnp.dot(a, b, preferred_element_type=F32)
    tm = x_ref.shape[0]
    picks = []
    for rows in _row_blocks(tm):
        y_rnn = ((hf_ref[rows, :].astype(F32) + hb_ref[rows, :].astype(F32))
                 * jax.nn.gelu(gr_ref[rows, :].astype(F32)))
        y = jnp.concatenate([_rms(y_rnn) * gg_ref[:, :D_RNN],
                             _rms(o_ref[rows, :].astype(F32)) * gg_ref[:, D_RNN:]], axis=1)
        x = x_ref[rows, :] + mod(GT_M) * dot(y.astype(BF16), wo_ref[0])
        xo_ref[rows, :] = x
        h2 = (_rms(x) * gf_ref[...]) * (1.0 + mod(SC_F)) + mod(SH_F)
        h2_ref[rows, :] = h2.astype(h2_ref.dtype)
        if not moe:
            continue
        hi = h2.astype(BF16)
        lo = (h2 - hi.astype(F32)).astype(BF16)
        logits = dot(hi, wrh_ref[0]) + (dot(hi, wrl_ref[0]) + dot(lo, wrh_ref[0]))
        lane = lax.broadcasted_iota(jnp.int32, logits.shape, 1)
        logits = jnp.where(lane < N_EXPERTS, logits, -jnp.inf)
        v1 = jnp.max(logits, axis=-1, keepdims=True)
        i1 = jnp.min(jnp.where(logits == v1, lane, LANES), axis=-1, keepdims=True)
        rest = jnp.where(lane == i1, -jnp.inf, logits)
        v2 = jnp.max(rest, axis=-1, keepdims=True)
        i2 = jnp.min(jnp.where(rest == v2, lane, LANES), axis=-1, keepdims=True)
        e2 = jnp.exp(v2 - v1)
        w1 = 1.0 / (1.0 + e2)
        w2 = e2 / (1.0 + e2)
        if moe == "comb":
            comb_ref[rows, :] = jnp.where(lane == i1, w1, 0.0) + jnp.where(lane == i2, w2, 0.0)
        else:
            mf_ref[rows, :] = jnp.where(lane == 0, w1, jnp.where(lane == 1, w2, 0.0))[:, :META_W]
            picks.append((i1, i2))
    if moe == "route":
        i1 = jnp.concatenate([p[0] for p in picks], axis=0)
        i2 = jnp.concatenate([p[1] for p in picks], axis=0)
        lane = lax.broadcasted_iota(jnp.int32, (tm, LANES), 1)
        chosen = (lane == i1) | (lane == i2)
        before = (lax.broadcasted_iota(jnp.int32, (tm, tm), 0)
                  > lax.broadcasted_iota(jnp.int32, (tm, tm), 1))
        rank = dot(jnp.where(before, 1.0, 0.0).astype(BF16), jnp.where(chosen, 1.0, 0.0).astype(BF16))
        r1 = jnp.sum(jnp.where(lane == i1, rank, 0.0), axis=-1, keepdims=True).astype(jnp.int32)
        r2 = jnp.sum(jnp.where(lane == i2, rank, 0.0), axis=-1, keepdims=True).astype(jnp.int32)
        meta_i = jnp.where(lane == 0, i1, jnp.where(lane == 1, i2,
                           jnp.where(lane == 2, r1, jnp.where(lane == 3, r2, 0))))
        mi_ref[...] = meta_i[:, :META_W]
        counts = jnp.sum(jnp.where(chosen, 1.0, 0.0), axis=0, keepdims=True)
        cnt_ref[0] = jnp.broadcast_to(counts, cnt_ref.shape[1:]).astype(jnp.int32)


META_W = 8


def merge(x, hf, hb, gr, o, mod, g_grp, w_out, g_ffn, router, *, layer, moe_layer, row, tm, moe):
    n = x.shape[0]
    tile = lambda w: pl.BlockSpec((tm, w), lambda i: (i, 0))
    whole = lambda a: pl.BlockSpec(a.shape, lambda i: (0,) * a.ndim)
    in_specs = [tile(D_MODEL), tile(D_RNN), tile(D_RNN), tile(D_RNN), tile(D_ATTN),
                whole(mod), whole(g_grp),
                pl.BlockSpec((1, D_MODEL, D_MODEL), lambda i: (layer, 0, 0)), whole(g_ffn)]
    args = [x, hf, hb, gr, o, mod, g_grp, w_out, g_ffn]
    out_specs = [tile(D_MODEL), tile(D_MODEL)]
    out_shape = [jax.ShapeDtypeStruct((n, D_MODEL), F32), jax.ShapeDtypeStruct((n, D_MODEL), BF16)]
    if moe:
        in_specs += [pl.BlockSpec((1, D_MODEL, LANES), lambda i: (moe_layer, 0, 0))] * 2
        args += list(router)
    if moe == "comb":
        out_specs.append(tile(LANES))
        out_shape.append(jax.ShapeDtypeStruct((n, LANES), F32))
    elif moe == "route":
        out_specs += [tile(META_W), tile(META_W), pl.BlockSpec((1, SUBLANES, LANES), lambda i: (i, 0, 0))]
        out_shape += [jax.ShapeDtypeStruct((n, META_W), jnp.int32), jax.ShapeDtypeStruct((n, META_W), F32),
                      jax.ShapeDtypeStruct((n // tm, SUBLANES, LANES), jnp.int32)]
    return pl.pallas_call(
        functools.partial(_merge_kernel, row=row, moe=moe),
        grid=(n // tm,),
        in_specs=in_specs,
        out_specs=out_specs,
        out_shape=out_shape,
        compiler_params=_params("arbitrary"),
        name="merge",
    )(*args)


def _ffn_kernel(*refs, row, n_exp, nf, moe, final):
    refs = list(refs)
    h_ref, x_ref, mod_ref = refs[:3]
    pos = 3
    comb_ref = None
    if moe:
        comb_ref = refs[pos]
        pos += 1
    gfin_ref = None
    if final:
        gfin_ref = refs[pos]
        pos += 1
    wg_ref, wu_ref, wd_ref, o_ref, acc_ref = refs[pos:pos + 5]
    tot_ref = refs[pos + 5] if moe else acc_ref
    e = pl.program_id(1)
    f = pl.program_id(2)

    @pl.when(f == 0)
    def _():
        acc_ref[...] = jnp.zeros_like(acc_ref)

    if moe:
        @pl.when((e == 0) & (f == 0))
        def _():
            tot_ref[...] = jnp.zeros_like(tot_ref)

    h = h_ref[...]
    a = jnp.dot(h, wg_ref[0], preferred_element_type=F32)
    b = jnp.dot(h, wu_ref[0], preferred_element_type=F32)
    act = (a * jax.nn.sigmoid(a)) * b
    acc_ref[...] += jnp.dot(act.astype(BF16), wd_ref[0], preferred_element_type=F32)

    if moe:
        @pl.when(f == nf - 1)
        def _():
            comb = comb_ref[...]
            lane = lax.broadcasted_iota(jnp.int32, comb.shape, 1)
            ce = jnp.sum(jnp.where(lane == e, comb, 0.0), axis=-1, keepdims=True)
            tot_ref[...] += ce * acc_ref[...]

    @pl.when((e == n_exp - 1) & (f == nf - 1))
    def _():
        x = x_ref[...] + mod_ref[0, row:row + 1, GT_F:GT_F + D_MODEL] * tot_ref[...]
        if final:
            x = _rms(x) * gfin_ref[...]
        o_ref[...] = x


def ffn(h2, x, mod, comb, g_final, wg, wu, wd, *, first, n_exp, row, tm, tf):
    n = x.shape[0]
    d_ff = wg.shape[2]
    nf = d_ff // tf
    moe = comb is not None
    final = g_final is not None
    tile = lambda w: pl.BlockSpec((tm, w), lambda t, e, f: (t, 0))
    in_specs = [tile(D_MODEL), tile(D_MODEL), pl.BlockSpec(mod.shape, lambda t, e, f: (0, 0, 0))]
    args = [h2, x, mod]
    if moe:
        in_specs.append(tile(LANES))
        args.append(comb)
    if final:
        in_specs.append(pl.BlockSpec((1, D_MODEL), lambda t, e, f: (0, 0)))
        args.append(g_final)
    in_specs += [
        pl.BlockSpec((1, D_MODEL, tf), lambda t, e, f: (first + e, 0, f)),
        pl.BlockSpec((1, D_MODEL, tf), lambda t, e, f: (first + e, 0, f)),
        pl.BlockSpec((1, tf, D_MODEL), lambda t, e, f: (first + e, f, 0)),
    ]
    args += [wg, wu, wd]
    scratch = [pltpu.VMEM((tm, D_MODEL), F32)]
    if moe:
        scratch.append(pltpu.VMEM((tm, D_MODEL), F32))
    return pl.pallas_call(
        functools.partial(_ffn_kernel, row=row, n_exp=n_exp, nf=nf, moe=moe, final=final),
        grid=(n // tm, n_exp, nf),
        in_specs=in_specs,
        out_specs=tile(D_MODEL),
        out_shape=jax.ShapeDtypeStruct((n, D_MODEL), F32),
        scratch_shapes=scratch,
        compiler_params=_params("arbitrary", "arbitrary", "arbitrary"),
        name="ffn",
    )(*args)


MOE_TILE = 512
MOE_TOKENS = 512
SEG_ALIGN = 16
PERM_ROWS = 2 * MOE_TOKENS + N_EXPERTS * SEG_ALIGN
CHUNKS = (64, 32, 16)


def _block_tables(cnt_tile, meta_i, tm):
    i32 = jnp.int32
    n_tiles = cnt_tile.shape[0]
    n = meta_i.shape[0]
    cnt = cnt_tile[:, 0, :N_EXPERTS]
    c16 = (cnt + SEG_ALIGN - 1) // SEG_ALIGN * SEG_ALIGN
    seg_off = jnp.cumsum(c16, axis=1) - c16
    run = jnp.cumsum(c16, axis=0) - c16
    total = jnp.sum(c16, axis=0)
    tiles = (total + MOE_TILE - 1) // MOE_TILE
    cum = jnp.cumsum(tiles)
    start = (cum - tiles) * MOE_TILE
    dst = start[None, :] + run
    experts = jnp.arange(N_EXPERTS, dtype=i32)
    off_tok = jnp.repeat(seg_off, tm, axis=0)
    pick = lambda e: jnp.sum(jnp.where(e[:, None] == experts[None, :], off_tok, 0), axis=1)
    d1 = (meta_i[:, 2] + pick(meta_i[:, 0])).astype(i32)
    d2 = (meta_i[:, 3] + pick(meta_i[:, 1])).astype(i32)
    d_rows = jnp.pad(jnp.stack([d1.reshape(n_tiles, tm), d2.reshape(n_tiles, tm)], axis=1),
                     ((0, 0), (0, SUBLANES - 2), (0, 0)))
    d_cols = jnp.pad(jnp.stack([d1, d2], axis=1), ((0, 0), (0, META_W - 2)))
    n_pad = _moe_rows(n)
    n_used = cum[-1]
    t = jnp.arange(n_pad // MOE_TILE, dtype=i32)
    te = jnp.sum((t[:, None] >= cum[None, :]).astype(i32), axis=1)
    used = t < n_used
    tile_expert = jnp.where(used, te, jnp.max(jnp.where(used, te, 0))).astype(i32)
    flat = lambda a: a.reshape(-1).astype(i32)
    return dict(seg_off=flat(seg_off), c16=flat(c16), dst=flat(dst), zero_start=(start + total).astype(i32),
                zero_len=(tiles * MOE_TILE - total).astype(i32), n_used=n_used.reshape(1).astype(i32),
                tile_expert=tile_expert, d_rows=d_rows, d_cols=d_cols)


def _moe_rows(n):
    worst = 2 * n + (n // MOE_TOKENS) * N_EXPERTS * SEG_ALIGN + N_EXPERTS * MOE_TILE
    return (worst + MOE_TILE - 1) // MOE_TILE * MOE_TILE


def _segment_copies(src, dst, src_off, dst_off, rows, sems, buf, act):
    big, mid, small = CHUNKS
    n_big = lax.shift_right_logical(rows, big.bit_length() - 1)
    sem = lambda c: sems.at[buf * len(CHUNKS) + c]

    def copy(size, k_src, k_dst, c):
        return pltpu.make_async_copy(src.at[pl.ds(pl.multiple_of(k_src, SEG_ALIGN), size)],
                                     dst.at[pl.ds(pl.multiple_of(k_dst, SEG_ALIGN), size)], sem(c))

    def body(k, c):
        act(copy(big, src_off + k * big, dst_off + k * big, 0))
        return c

    lax.fori_loop(0, n_big, body, 0)
    done = n_big * big

    @pl.when((rows & mid) != 0)
    def _():
        act(copy(mid, src_off + done, dst_off + done, 1))

    @pl.when((rows & small) != 0)
    def _():
        act(copy(small, src_off + done + (rows & mid), dst_off + done + (rows & mid), 2))


ZERO_SIZES = (256, 128, 64, 32, 16)


def _zero_fill(zero_ref, dst, start, rows, sems, act):
    done = 0 * rows
    for s, size in enumerate(ZERO_SIZES):
        @pl.when((rows & size) != 0)
        def _(s=s, size=size, done=done):
            act(pltpu.make_async_copy(zero_ref.at[pl.ds(0, size)],
                                      dst.at[pl.ds(pl.multiple_of(start + done, SEG_ALIGN), size)], sems.at[s]))
        done = done + (rows & size)


def _start(copy):
    copy.start()


def _wait(copy):
    copy.wait()


def _dispatch_kernel(so_ref, c16_ref, dst_ref, zs_ref, zl_ref, nu_ref, h_ref, d_ref, x_hbm,
                      z_ref, zero_ref, sems, zsems, *, tm, n_tiles, n_pad):
    i = pl.program_id(0)
    d1 = d_ref[0, 0:1, :]
    d2 = d_ref[0, 1:2, :]
    r = lax.broadcasted_iota(jnp.int32, (PERM_ROWS, tm), 0)
    perm = jnp.where((r == d1) | (r == d2), 1.0, 0.0).astype(BF16)
    slot = i % 2
    z_ref[slot] = jnp.dot(perm, h_ref[...], preferred_element_type=F32).astype(BF16)

    def move(tile, buf, act):
        for e in range(N_EXPERTS):
            k = tile * N_EXPERTS + e
            _segment_copies(z_ref.at[buf], x_hbm, so_ref[k], dst_ref[k], c16_ref[k], sems, buf, act)

    @pl.when(i > 0)
    def _():
        move(i - 1, 1 - slot, _wait)

    move(i, slot, _start)

    @pl.when(i == n_tiles - 1)
    def _():
        move(i, slot, _wait)
        zero_ref[...] = jnp.zeros_like(zero_ref)
        for act in (_start, _wait):
            for e in range(N_EXPERTS):
                _zero_fill(zero_ref, x_hbm, zs_ref[e], zl_ref[e], zsems, act)

            def tail(t, c):
                act(pltpu.make_async_copy(zero_ref, x_hbm.at[pl.ds(pl.multiple_of(t * MOE_TILE, MOE_TILE),
                                                                   MOE_TILE)], zsems.at[len(ZERO_SIZES)]))
                return c

            lax.fori_loop(nu_ref[0], n_pad // MOE_TILE, tail, 0)


def moe_dispatch(tab, h2, *, tm):
    n = h2.shape[0]
    n_tiles = n // tm
    n_pad = _moe_rows(n)
    return pl.pallas_call(
        functools.partial(_dispatch_kernel, tm=tm, n_tiles=n_tiles, n_pad=n_pad),
        grid_spec=pltpu.PrefetchScalarGridSpec(
            num_scalar_prefetch=6,
            grid=(n_tiles,),
            in_specs=[pl.BlockSpec((tm, D_MODEL), lambda i, *_: (i, 0)),
                      pl.BlockSpec((1, SUBLANES, tm), lambda i, *_: (i, 0, 0))],
            out_specs=pl.BlockSpec(memory_space=pl.ANY),
            scratch_shapes=[pltpu.VMEM((2, PERM_ROWS, D_MODEL), BF16), pltpu.VMEM((MOE_TILE, D_MODEL), BF16),
                            pltpu.SemaphoreType.DMA((2 * len(CHUNKS),)),
                            pltpu.SemaphoreType.DMA((len(ZERO_SIZES) + 1,))],
        ),
        out_shape=jax.ShapeDtypeStruct((n_pad, D_MODEL), BF16),
        compiler_params=_params("arbitrary"),
        name="moe_dispatch",
    )(tab["seg_off"], tab["c16"], tab["dst"], tab["zero_start"], tab["zero_len"], tab["n_used"],
      h2, tab["d_rows"])


def _grouped_ffn_kernel(te_ref, nt_ref, x_ref, wg_ref, wu_ref, wd_ref, y_ref):
    t = pl.program_id(0)

    @pl.when(t < nt_ref[0])
    def _():
        x = x_ref[...]
        a = jnp.dot(x, wg_ref[0], preferred_element_type=F32)
        b = jnp.dot(x, wu_ref[0], preferred_element_type=F32)
        act = (a * jax.nn.sigmoid(a)) * b
        y_ref[...] = jnp.dot(act.astype(BF16), wd_ref[0], preferred_element_type=F32).astype(y_ref.dtype)

    @pl.when(t >= nt_ref[0])
    def _():
        y_ref[...] = jnp.zeros_like(y_ref)


def grouped_ffn(tile_expert, n_used, xs, wg, wu, wd, *, first):
    n_pad = xs.shape[0]
    d_ff = wg.shape[2]
    rows = pl.BlockSpec((MOE_TILE, D_MODEL), lambda t, te, nt: (t, 0))
    return pl.pallas_call(
        _grouped_ffn_kernel,
        grid_spec=pltpu.PrefetchScalarGridSpec(
            num_scalar_prefetch=2,
            grid=(n_pad // MOE_TILE,),
            in_specs=[rows,
                      pl.BlockSpec((1, D_MODEL, d_ff), lambda t, te, nt: (first + te[t], 0, 0)),
                      pl.BlockSpec((1, D_MODEL, d_ff), lambda t, te, nt: (first + te[t], 0, 0)),
                      pl.BlockSpec((1, d_ff, D_MODEL), lambda t, te, nt: (first + te[t], 0, 0))],
            out_specs=rows,
        ),
        out_shape=jax.ShapeDtypeStruct((n_pad, D_MODEL), BF16),
        compiler_params=_params("arbitrary"),
        name="grouped_ffn",
    )(tile_expert, n_used, xs, wg, wu, wd)


def _combine_kernel(*refs, row, tm, final):
    if final:
        so_ref, c16_ref, dst_ref, x_ref, dc_ref, mf_ref, mod_ref, gfin_ref, y_hbm, o_ref, yt_ref, sems = refs
    else:
        so_ref, c16_ref, dst_ref, x_ref, dc_ref, mf_ref, mod_ref, y_hbm, o_ref, yt_ref, sems = refs
    i = pl.program_id(0)
    n_steps = pl.num_programs(0)
    slot = i % 2

    def fetch(tile, buf, act):
        for e in range(N_EXPERTS):
            k = tile * N_EXPERTS + e
            _segment_copies(y_hbm, yt_ref.at[buf], dst_ref[k], so_ref[k], c16_ref[k], sems, buf, act)

    @pl.when(i == 0)
    def _():
        yt_ref[...] = jnp.zeros_like(yt_ref)
        fetch(0, 0, _start)

    fetch(i, slot, _wait)

    @pl.when(i + 1 < n_steps)
    def _():
        fetch(i + 1, 1 - slot, _start)

    yt = yt_ref[slot]
    r = lax.broadcasted_iota(jnp.int32, (tm, PERM_ROWS), 1)
    pick = lambda col: jnp.dot(jnp.where(r == dc_ref[:, col:col + 1], 1.0, 0.0).astype(BF16), yt,
                               preferred_element_type=F32)
    tot = mf_ref[:, 0:1] * pick(0) + mf_ref[:, 1:2] * pick(1)
    x = x_ref[...] + mod_ref[0, row:row + 1, GT_F:GT_F + D_MODEL] * tot
    if final:
        x = _rms(x) * gfin_ref[...]
    o_ref[...] = x


def moe_combine(tab, x, meta_f, mod, g_final, ys, *, row, tm):
    n = x.shape[0]
    final = g_final is not None
    tile = lambda w: pl.BlockSpec((tm, w), lambda i, *_: (i, 0))
    in_specs = [tile(D_MODEL), tile(META_W), tile(META_W), pl.BlockSpec(mod.shape, lambda i, *_: (0, 0, 0))]
    args = [x, tab["d_cols"], meta_f, mod]
    if final:
        in_specs.append(pl.BlockSpec((1, D_MODEL), lambda i, *_: (0, 0)))
        args.append(g_final)
    in_specs.append(pl.BlockSpec(memory_space=pl.ANY))
    args.append(ys)
    return pl.pallas_call(
        functools.partial(_combine_kernel, row=row, tm=tm, final=final),
        grid_spec=pltpu.PrefetchScalarGridSpec(
            num_scalar_prefetch=3,
            grid=(n // tm,),
            in_specs=in_specs,
            out_specs=tile(D_MODEL),
            scratch_shapes=[pltpu.VMEM((2, PERM_ROWS, D_MODEL), BF16),
                            pltpu.SemaphoreType.DMA((2 * len(CHUNKS),))],
        ),
        out_shape=jax.ShapeDtypeStruct((n, D_MODEL), F32),
        compiler_params=_params("arbitrary"),
        name="moe_combine",
    )(tab["seg_off"], tab["c16"], tab["dst"], *args)


def _gate_weights(w_rg):
    depth = w_rg.shape[0]
    per = RNN_CHUNK // RNN_BLOCK
    nch = D_RNN // RNN_CHUNK
    wr = w_rg.astype(BF16).reshape(depth, 2, 2, nch, per, RNN_BLOCK, 1, RNN_BLOCK)
    on_diag = jnp.eye(per, dtype=bool).reshape(per, 1, per, 1)
    bd = jnp.where(on_diag, wr, jnp.zeros((), BF16))
    bd = bd.reshape(depth, 2, 2, nch, RNN_CHUNK, RNN_CHUNK)
    return jnp.concatenate([bd[:, :, 0], bd[:, :, 1]], axis=-1)


def _tile_rows(n, want):
    return want if n % want == 0 else n


def kernel(x, c, ctx, c_ctx, w_mod, b_mod, g_mix, g_ffn, w_in, conv_w, conv_b, w_rg, b_rg, lam,
           sink, g_grp, w_out, w_ffn_gate, w_ffn_up, w_ffn_down, w_router, w_exp_gate, w_exp_up,
           w_exp_down, g_final):
    assert x.shape[0] == 1 and ctx.shape[0] == 1
    depth = w_mod.shape[0]
    seq, n_ctx = x.shape[1], ctx.shape[1]
    xl, xc = x[0], ctx[0]

    cond8 = jnp.zeros((SUBLANES, D_MODEL), F32).at[0].set(c[0]).at[1].set(c_ctx)
    mods = adaln_all(cond8, w_mod, b_mod)
    cos, sin = rope_tables(seq)

    w_in_b = w_in.astype(BF16)
    w_out_b = w_out.astype(BF16)
    wg_gate = _gate_weights(w_rg)
    ffn_w = (w_ffn_gate.astype(BF16), w_ffn_up.astype(BF16), w_ffn_down.astype(BF16))
    exp_w = tuple(w.astype(BF16).reshape((-1,) + w.shape[2:]) for w in (w_exp_gate, w_exp_up, w_exp_down))
    wr_pad = jnp.pad(w_router, ((0, 0), (0, 0), (0, LANES - N_EXPERTS)))
    wr_hi = wr_pad.astype(BF16)
    wr_lo = (wr_pad - wr_hi.astype(F32)).astype(BF16)
    zeros_h0 = jnp.zeros((SUBLANES, D_RNN), F32)

    tm = _tile_rows(seq, 512)
    tm_ffn = _tile_rows(seq, 1024)

    for l in range(depth):
        last = l == depth - 1
        moe = l % 2 == 1
        j = l // 2
        mod = mods[l:l + 1]
        gm, gf, gg = g_mix[l:l + 1], g_ffn[l:l + 1], g_grp[l:l + 1]

        xr_c, gr_c, q_c, kv_c = in_proj(xc, mod, gm, w_in_b, None, None, layer=l, row=1, tm=n_ctx)
        xr_l, gr_l, q_l, kv_l = in_proj(xl, mod, gm, w_in_b, cos, sin, layer=l, row=0, tm=tm_ffn)

        rnn_args = (conv_w[l], conv_b[l:l + 1], wg_gate, b_rg[l], lam[l])
        hf_c, hb_c, h_ctx_end = rnn_bidir(xr_c, *rnn_args, zeros_h0, layer=l, tm=n_ctx)
        hf_l, hb_l, _ = rnn_bidir(xr_l, *rnn_args, h_ctx_end, layer=l, tm=tm)

        o_l = attention(q_l, kv_l, kv_c, sink[l], band=True)

        router = (wr_hi, wr_lo) if moe else None
        weights = exp_w if moe else ffn_w
        first = j * N_EXPERTS if moe else j
        tf = 512 if moe else 1024
        merge_l = functools.partial(merge, layer=l, moe_layer=j)

        g_fin = g_final.reshape(1, D_MODEL) if last else None
        if moe:
            xl, h2, meta_i, meta_f, cnt = merge_l(xl, hf_l, hb_l, gr_l, o_l, mod, gg, w_out_b, gf, router,
                                                  row=0, tm=tm, moe="route")
            assert tm == MOE_TOKENS
            tab = _block_tables(cnt, meta_i, tm)
            xs = moe_dispatch(tab, h2, tm=tm)
            ys = grouped_ffn(tab["tile_expert"], tab["n_used"], xs, *weights, first=first)
            xl = moe_combine(tab, xl, meta_f, mod, g_fin, ys, row=0, tm=tm)
        else:
            xl, h2 = merge_l(xl, hf_l, hb_l, gr_l, o_l, mod, gg, w_out_b, gf, None, row=0, tm=tm, moe=None)
            xl = ffn(h2, xl, mod, None, g_fin, *weights, first=first, n_exp=1, row=0, tm=tm_ffn, tf=tf)

        if not last:
            o_c = attention(q_c, kv_c, kv_c, sink[l], band=False)
            res = merge_l(xc, hf_c, hb_c, gr_c, o_c, mod, gg, w_out_b, gf, router, row=1, tm=n_ctx,
                          moe="comb" if moe else None)
            xc, h2c = res[0], res[1]
            comb_c = res[2] if moe else None
            xc = ffn(h2c, xc, mod, comb_c, None, *weights, first=first, n_exp=N_EXPERTS if moe else 1,
                     row=1, tm=n_ctx, tf=tf)

    return xl[None]
```

```python
import functools

import jax
import jax.numpy as jnp
from jax import lax
from jax.experimental import pallas as pl
from jax.experimental.pallas import tpu as pltpu

F32 = jnp.float32
BF16 = jnp.bfloat16

D_MODEL = 1024
D_RNN = 512
D_ATTN = 512
D_KV = 128
HEAD_DIM = 64
N_HEADS = 8
N_KV_HEADS = 2
HEADS_PER_KV = N_HEADS // N_KV_HEADS
D_IN = 2 * D_RNN + D_ATTN + 2 * D_KV
RNN_BLOCK = 64
RNN_CHUNK = 256
CONV_W = 4
LRU_C = 8.0
WINDOW = 128
BLOCK_Q = 128
GRID_W = 64
ROPE_BASE = 10000.0
N_EXPERTS = 8
EPS = 1e-6
NEG_INF = -1e30

LANES = 128
SUBLANES = 8
VMEM_LIMIT = 56 * 1024 * 1024

SH_M, SC_M, GT_M, SH_F, SC_F, GT_F = (k * D_MODEL for k in range(6))


def _params(*sem):
    return pltpu.CompilerParams(dimension_semantics=sem, vmem_limit_bytes=VMEM_LIMIT)


def _rms(x):
    return x * lax.rsqrt(jnp.mean(x * x, axis=-1, keepdims=True) + EPS)


SUB_ROWS = 256


def _row_blocks(tm):
    step = SUB_ROWS if tm % SUB_ROWS == 0 else tm
    return [slice(r, r + step) for r in range(0, tm, step)]


def _adaln_kernel(c_ref, w_ref, b_ref, o_ref):
    c = c_ref[...]
    s = c * jax.nn.sigmoid(c)
    w = w_ref[0]
    s_hi, w_hi = s.astype(BF16), w.astype(BF16)
    s_lo = (s - s_hi.astype(F32)).astype(BF16)
    w_lo = (w - w_hi.astype(F32)).astype(BF16)
    dot = lambda a, b: jnp.dot(a, b, preferred_element_type=F32)
    o_ref[0] = dot(s_hi, w_hi) + (dot(s_hi, w_lo) + dot(s_lo, w_hi)) + b_ref[0]


def adaln_all(cond8, w_mod, b_mod):
    depth = w_mod.shape[0]
    nc = 1536
    return pl.pallas_call(
        _adaln_kernel,
        grid=(depth, 6 * D_MODEL // nc),
        in_specs=[
            pl.BlockSpec((SUBLANES, D_MODEL), lambda l, j: (0, 0)),
            pl.BlockSpec((1, D_MODEL, nc), lambda l, j: (l, 0, j)),
            pl.BlockSpec((1, 1, nc), lambda l, j: (l, 0, j)),
        ],
        out_specs=pl.BlockSpec((1, SUBLANES, nc), lambda l, j: (l, 0, j)),
        out_shape=jax.ShapeDtypeStruct((depth, SUBLANES, 6 * D_MODEL), F32),
        compiler_params=_params("arbitrary", "arbitrary"),
        name="adaln",
    )(cond8, w_mod, b_mod.reshape(depth, 1, 6 * D_MODEL))


ROPE_ROWS = SUBLANES * GRID_W


def _rope_kernel(f_ref, c_ref, s_ref):
    i = pl.program_id(0)
    lane = lax.broadcasted_iota(jnp.int32, (GRID_W, LANES), 1)
    col_axis = ((lane >> 5) & 1) == 1
    sign = jnp.where(((lane >> 4) & 1) == 0, -1.0, 1.0).astype(F32)
    freq = f_ref[...]
    rowpos = (i * SUBLANES + lax.broadcasted_iota(jnp.int32, (SUBLANES, LANES), 0)).astype(F32)
    colpos = lax.broadcasted_iota(jnp.int32, (GRID_W, LANES), 0).astype(F32)
    ang_r = rowpos * freq
    ang_c = colpos * freq
    cr, sr = jnp.cos(ang_r), jnp.sin(ang_r)
    cc, sc = jnp.cos(ang_c), jnp.sin(ang_c) * sign
    for g in range(SUBLANES):
        rows = slice(g * GRID_W, (g + 1) * GRID_W)
        c_ref[rows, :] = jnp.where(col_axis, cc, jnp.broadcast_to(cr[g:g + 1, :], (GRID_W, LANES)))
        s_ref[rows, :] = jnp.where(col_axis, sc, jnp.broadcast_to(sr[g:g + 1, :], (GRID_W, LANES)) * sign)


def rope_tables(seq):
    axis_dim = HEAD_DIM // 2
    freqs = ROPE_BASE ** (-jnp.arange(0, axis_dim, 2, dtype=F32) / axis_dim)
    freq_lane = jnp.tile(freqs, LANES // freqs.shape[0]).reshape(1, LANES)
    return pl.pallas_call(
        _rope_kernel,
        grid=(seq // ROPE_ROWS,),
        in_specs=[pl.BlockSpec((1, LANES), lambda i: (0, 0))],
        out_specs=[pl.BlockSpec((ROPE_ROWS, LANES), lambda i: (i, 0))] * 2,
        out_shape=[jax.ShapeDtypeStruct((seq, LANES), F32)] * 2,
        compiler_params=_params("arbitrary"),
        name="rope_tables",
    )(freq_lane)


def _rope_chunk(x, cos, sin):
    lane = lax.broadcasted_iota(jnp.int32, x.shape, 1)
    first_half = ((lane >> 4) & 1) == 0
    partner = jnp.where(first_half, pltpu.roll(x, LANES - 16, axis=1), pltpu.roll(x, 16, axis=1))
    return x * cos + partner * sin


def _in_proj_kernel(*refs, row, rope):
    if rope:
        x_ref, mod_ref, g_ref, w_ref, cos_ref, sin_ref, xr_ref, gr_ref, q_ref, kv_ref = refs
    else:
        x_ref, mod_ref, g_ref, w_ref, xr_ref, gr_ref, q_ref, kv_ref = refs
    shift = mod_ref[0, row:row + 1, SH_M:SH_M + D_MODEL]
    scale = mod_ref[0, row:row + 1, SC_M:SC_M + D_MODEL]
    q0 = 2 * D_RNN
    for rows in _row_blocks(x_ref.shape[0]):
        x = x_ref[rows, :]
        h = (_rms(x) * g_ref[...]) * (1.0 + scale) + shift
        p = jnp.dot(h.astype(BF16), w_ref[0], preferred_element_type=F32)
        xr_ref[rows, :] = p[:, 0:D_RNN]
        gr_ref[rows, :] = p[:, D_RNN:2 * D_RNN].astype(gr_ref.dtype)
        if rope:
            cos, sin = cos_ref[rows, :], sin_ref[rows, :]
        for c in range(D_ATTN // LANES):
            qc = p[:, q0 + c * LANES:q0 + (c + 1) * LANES]
            if rope:
                qc = _rope_chunk(qc, cos, sin)
            q_ref[rows, c * LANES:(c + 1) * LANES] = (qc * (HEAD_DIM ** -0.5)).astype(BF16)
        k = p[:, q0 + D_ATTN:q0 + D_ATTN + D_KV]
        v = p[:, q0 + D_ATTN + D_KV:q0 + D_ATTN + 2 * D_KV]
        if rope:
            k = _rope_chunk(k, cos, sin)
        kv_ref[rows, 0:LANES] = k.astype(BF16)
        kv_ref[rows, LANES:2 * LANES] = pltpu.roll(k, HEAD_DIM, axis=1).astype(BF16)
        kv_ref[rows, 2 * LANES:3 * LANES] = v.astype(BF16)
        kv_ref[rows, 3 * LANES:4 * LANES] = pltpu.roll(v, HEAD_DIM, axis=1).astype(BF16)


def in_proj(x, mod, g_mix, w_in, cos, sin, *, layer, row, tm):
    n = x.shape[0]
    rope = cos is not None
    tile = lambda w: pl.BlockSpec((tm, w), lambda i: (i, 0))
    in_specs = [
        tile(D_MODEL),
        pl.BlockSpec((1, SUBLANES, 6 * D_MODEL), lambda i: (0, 0, 0)),
        pl.BlockSpec((1, D_MODEL), lambda i: (0, 0)),
        pl.BlockSpec((1, D_MODEL, D_IN), lambda i: (layer, 0, 0)),
    ]
    args = [x, mod, g_mix, w_in]
    if rope:
        in_specs += [tile(LANES), tile(LANES)]
        args += [cos, sin]
    return pl.pallas_call(
        functools.partial(_in_proj_kernel, row=row, rope=rope),
        grid=(n // tm,),
        in_specs=in_specs,
        out_specs=[tile(D_RNN), tile(D_RNN), tile(D_ATTN), tile(4 * LANES)],
        out_shape=[
            jax.ShapeDtypeStruct((n, D_RNN), F32),
            jax.ShapeDtypeStruct((n, D_RNN), BF16),
            jax.ShapeDtypeStruct((n, D_ATTN), BF16),
            jax.ShapeDtypeStruct((n, 4 * LANES), BF16),
        ],
        compiler_params=_params("arbitrary"),
        name="in_proj",
    )(*args)


RNN_SLABS = D_RNN // LANES
SEG_PAD = 4


def _rnn_kernel(pf_ref, mf_ref, nf_ref, pb_ref, mb_ref, nb_ref, cw_ref, cb_ref, wg_ref, bg_ref,
                lam_ref, h0_ref, hf_ref, hb_ref, hl_ref, ext_ref, xs_ref, u_ref, a_ref, b_ref, hs_ref,
                cf_ref, cbk_ref, *, tm, nt):
    i = pl.program_id(0)
    seg = tm // SUBLANES
    pitch = seg + SEG_PAD
    slab = lambda c: slice(c * LANES, (c + 1) * LANES)

    @pl.when(i == 0)
    def _():
        cf_ref[...] = jnp.broadcast_to(h0_ref[0:1, :], (SUBLANES, D_RNN))
        cbk_ref[...] = jnp.broadcast_to(h0_ref[1:2, :], (SUBLANES, D_RNN))

    softplus_neg_lam = jax.nn.softplus(-lam_ref[...])
    seg_id = lax.broadcasted_iota(jnp.int32, (SUBLANES, D_RNN), 0)

    def run(d, prev_ref, main_ref, next_ref, at_start, at_end, out_ref, carry_ref):
        ext_ref[0:SUBLANES, :] = jnp.where(at_start, 0.0, prev_ref[...])
        ext_ref[SUBLANES:tm + SUBLANES, :] = main_ref[...]
        ext_ref[tm + SUBLANES:tm + 2 * SUBLANES, :] = jnp.where(at_end, 0.0, next_ref[...])
        halo = SUBLANES - CONV_W // 2
        for j in range(SUBLANES):
            for c in range(RNN_SLABS):
                xs_ref[c, j * pitch:j * pitch + seg + SUBLANES, :] = (
                    ext_ref[j * seg + halo:j * seg + halo + seg + SUBLANES, slab(c)])
        for g in range(seg):
            for c in range(RNN_SLABS):
                acc = cb_ref[:, slab(c)]
                for k in range(CONV_W):
                    acc = acc + xs_ref[c, pl.ds(g + k, SUBLANES, stride=pitch), :] * cw_ref[k:k + 1, slab(c)]
                u_ref[g * SUBLANES:(g + 1) * SUBLANES, slab(c)] = acc
        u = u_ref[...]
        ub = u.astype(BF16)
        for c in range(D_RNN // RNN_CHUNK):
            ch = slice(c * RNN_CHUNK, (c + 1) * RNN_CHUNK)
            g = jnp.dot(ub[:, ch], wg_ref[0, d, c], preferred_element_type=F32)
            r = jax.nn.sigmoid(g[:, :RNN_CHUNK] + bg_ref[d, 0:1, ch])
            ig = jax.nn.sigmoid(g[:, RNN_CHUNK:] + bg_ref[d, 1:2, ch])
            log_a = (-LRU_C * r) * softplus_neg_lam[d:d + 1, ch]
            a = jnp.exp(log_a)
            mult = jnp.sqrt(jnp.maximum(1.0 - a * a, 0.0))
            a_ref[:, ch] = a
            b_ref[:, ch] = mult * (ig * u[:, ch])

        def scan(n, hp):
            g = n if d == 0 else seg - 1 - n
            off = pl.multiple_of(g * SUBLANES, SUBLANES)
            a_g = a_ref[pl.ds(off, SUBLANES), :]
            h = a_g * hp[0] + b_ref[pl.ds(off, SUBLANES), :]
            p = a_g * hp[1]
            b_ref[pl.ds(off, SUBLANES), :] = h
            a_ref[pl.ds(off, SUBLANES), :] = p
            return h, p

        zeros = jnp.zeros((SUBLANES, D_RNN), F32)
        h_end, p_end = lax.fori_loop(0, seg, scan, (zeros, zeros + 1.0), unroll=8)

        carry = carry_ref[...]
        enter = zeros
        for n in range(SUBLANES):
            j = n if d == 0 else SUBLANES - 1 - n
            enter = jnp.where(seg_id == j, carry, enter)
            carry = jnp.broadcast_to(h_end[j:j + 1, :] + p_end[j:j + 1, :] * carry[0:1, :], (SUBLANES, D_RNN))
        carry_ref[...] = carry

        for g in range(seg):
            rows = slice(g * SUBLANES, (g + 1) * SUBLANES)
            h = b_ref[rows, :] + a_ref[rows, :] * enter
            for c in range(RNN_SLABS):
                hs_ref[c, pl.ds(g, SUBLANES, stride=pitch), :] = h[:, slab(c)]
        for j in range(SUBLANES):
            for c in range(RNN_SLABS):
                out_ref[j * seg:(j + 1) * seg, slab(c)] = (
                    hs_ref[c, j * pitch:j * pitch + seg, :].astype(out_ref.dtype))

    run(0, pf_ref, mf_ref, nf_ref, i == 0, i == nt - 1, hf_ref, cf_ref)
    run(1, pb_ref, mb_ref, nb_ref, i == nt - 1, i == 0, hb_ref, cbk_ref)

    @pl.when(i == nt - 1)
    def _():
        rows = lax.broadcasted_iota(jnp.int32, (SUBLANES, D_RNN), 0)
        hl_ref[...] = jnp.where(rows == 0, cf_ref[...], jnp.where(rows == 1, cbk_ref[...], 0.0))


def rnn_bidir(xr, conv_w, conv_b, wg, bg, lam, h0, *, layer, tm):
    n = xr.shape[0]
    nt = n // tm
    per = tm // SUBLANES
    last8 = n // SUBLANES - 1
    fwd = lambda i: i
    bwd = lambda i: nt - 1 - i
    main = lambda t: pl.BlockSpec((tm, D_RNN), lambda i: (t(i), 0))
    prev = lambda t: pl.BlockSpec((SUBLANES, D_RNN), lambda i: (jnp.maximum(t(i) * per - 1, 0), 0))
    nxt = lambda t: pl.BlockSpec((SUBLANES, D_RNN), lambda i: (jnp.minimum((t(i) + 1) * per, last8), 0))
    whole = lambda a: pl.BlockSpec(a.shape, lambda i: (0,) * a.ndim)
    return pl.pallas_call(
        functools.partial(_rnn_kernel, tm=tm, nt=nt),
        grid=(nt,),
        in_specs=[prev(fwd), main(fwd), nxt(fwd), prev(bwd), main(bwd), nxt(bwd),
                  whole(conv_w), whole(conv_b),
                  pl.BlockSpec((1,) + wg.shape[1:], lambda i: (layer, 0, 0, 0, 0)),
                  whole(bg), whole(lam), whole(h0)],
        out_specs=[main(fwd), main(bwd), pl.BlockSpec((SUBLANES, D_RNN), lambda i: (0, 0))],
        out_shape=[
            jax.ShapeDtypeStruct((n, D_RNN), BF16),
            jax.ShapeDtypeStruct((n, D_RNN), BF16),
            jax.ShapeDtypeStruct((SUBLANES, D_RNN), F32),
        ],
        scratch_shapes=[
            pltpu.VMEM((tm + 2 * SUBLANES, D_RNN), F32),
            pltpu.VMEM((RNN_SLABS, tm + SUBLANES * SEG_PAD + SUBLANES, LANES), F32),
            pltpu.VMEM((tm, D_RNN), F32),
            pltpu.VMEM((tm, D_RNN), F32),
            pltpu.VMEM((tm, D_RNN), F32),
            pltpu.VMEM((RNN_SLABS, tm + SUBLANES * SEG_PAD, LANES), F32),
            pltpu.VMEM((SUBLANES, D_RNN), F32),
            pltpu.VMEM((SUBLANES, D_RNN), F32),
        ],
        compiler_params=_params("arbitrary"),
        name="rnn_bidir",
    )(xr, xr, xr, xr, xr, xr, conv_w, conv_b, wg, bg, lam, h0)


def _attn_kernel(*refs, nb, band, n_ctx):
    if band:
        sink_ref, q_ref, kvx_ref, kvp_ref, kvc_ref, kvn_ref, o_ref = refs
        kv = jnp.concatenate([kvx_ref[...], kvp_ref[...], kvc_ref[...], kvn_ref[...]], axis=0)
    else:
        sink_ref, q_ref, kvx_ref, o_ref = refs
        kv = kvx_ref[...]
    i = pl.program_id(0)
    nk = kv.shape[0]
    lane = lax.broadcasted_iota(jnp.int32, (nk, LANES), 1)
    low = lane < HEAD_DIM
    zero = jnp.zeros((), BF16)
    k, ks, v, vs = (kv[:, j * LANES:(j + 1) * LANES] for j in range(4))
    k_low = (jnp.where(low, k, zero), jnp.where(low, ks, zero))
    k_high = (jnp.where(low, zero, ks), jnp.where(low, zero, k))
    v_low = (jnp.where(low, v, zero), jnp.where(low, vs, zero))
    v_high = (jnp.where(low, zero, vs), jnp.where(low, zero, v))

    rows2 = 2 * BLOCK_Q
    r = lax.broadcasted_iota(jnp.int32, (rows2, nk), 0) & (BLOCK_Q - 1)
    if band:
        jb = lax.broadcasted_iota(jnp.int32, (rows2, nk), 1) - n_ctx
        valid = (jb < 0) | ((jb >= r) & (jb <= r + 2 * WINDOW)
                            & ((jb >= BLOCK_Q) | (i > 0)) & ((jb < 2 * BLOCK_Q) | (i < nb - 1)))
    upper = lax.broadcasted_iota(jnp.int32, (rows2, 1), 0) >= BLOCK_Q

    def probs(q2, kmat, sink_a, sink_b):
        s = lax.dot_general(q2, kmat, (((1,), (1,)), ((), ())), preferred_element_type=F32)
        if band:
            s = jnp.where(valid, s, NEG_INF)
        sink = jnp.where(upper, sink_b, sink_a)
        m = jnp.maximum(jnp.max(s, axis=-1, keepdims=True), sink)
        p = jnp.exp(s - m)
        den = jnp.sum(p, axis=-1, keepdims=True) + jnp.exp(sink - m)
        return p.astype(BF16), 1.0 / den

    for hk in range(N_KV_HEADS):
        c0, c1 = 2 * hk, 2 * hk + 1
        q2 = jnp.concatenate([q_ref[:, c0 * LANES:(c0 + 1) * LANES],
                              q_ref[:, c1 * LANES:(c1 + 1) * LANES]], axis=0)
        h0 = hk * HEADS_PER_KV
        p_low, inv_low = probs(q2, k_low[hk], sink_ref[h0], sink_ref[h0 + 2])
        p_high, inv_high = probs(q2, k_high[hk], sink_ref[h0 + 1], sink_ref[h0 + 3])
        o2 = (jnp.dot(p_low, v_low[hk], preferred_element_type=F32) * inv_low
              + jnp.dot(p_high, v_high[hk], preferred_element_type=F32) * inv_high)
        o_ref[:, c0 * LANES:(c0 + 1) * LANES] = o2[:BLOCK_Q].astype(o_ref.dtype)
        o_ref[:, c1 * LANES:(c1 + 1) * LANES] = o2[BLOCK_Q:].astype(o_ref.dtype)


def attention(q, kv, kv_ctx, sink, *, band):
    n = q.shape[0]
    nb = n // BLOCK_Q
    n_ctx = kv_ctx.shape[0]
    in_specs = [
        pl.BlockSpec(memory_space=pltpu.SMEM),
        pl.BlockSpec((BLOCK_Q, D_ATTN), lambda i: (i, 0)),
        pl.BlockSpec((n_ctx, 4 * LANES), lambda i: (0, 0)),
    ]
    args = [sink, q, kv_ctx]
    if band:
        blk = lambda f: pl.BlockSpec((BLOCK_Q, 4 * LANES), lambda i: (f(i), 0))
        in_specs += [blk(lambda i: jnp.maximum(i - 1, 0)), blk(lambda i: i),
                     blk(lambda i: jnp.minimum(i + 1, nb - 1))]
        args += [kv, kv, kv]
    return pl.pallas_call(
        functools.partial(_attn_kernel, nb=nb, band=band, n_ctx=n_ctx),
        grid=(nb,),
        in_specs=in_specs,
        out_specs=pl.BlockSpec((BLOCK_Q, D_ATTN), lambda i: (i, 0)),
        out_shape=jax.ShapeDtypeStruct((n, D_ATTN), BF16),
        compiler_params=_params("arbitrary"),
        name="attention",
    )(*args)


def _merge_kernel(*refs, row, moe):
    if moe == "comb":
        (x_ref, hf_ref, hb_ref, gr_ref, o_ref, mod_ref, gg_ref, wo_ref, gf_ref, wrh_ref, wrl_ref,
         xo_ref, h2_ref, comb_ref) = refs
    elif moe == "route":
        (x_ref, hf_ref, hb_ref, gr_ref, o_ref, mod_ref, gg_ref, wo_ref, gf_ref, wrh_ref, wrl_ref,
         xo_ref, h2_ref, mi_ref, mf_ref, cnt_ref) = refs
    else:
        x_ref, hf_ref, hb_ref, gr_ref, o_ref, mod_ref, gg_ref, wo_ref, gf_ref, xo_ref, h2_ref = refs
    mod = lambda off: mod_ref[0, row:row + 1, off:off + D_MODEL]
    dot = lambda a, b: jnp.dot(a, b, preferred_element_type=F32)
    tm = x_ref.shape[0]
    picks = []
    for rows in _row_blocks(tm):
        y_rnn = ((hf_ref[rows, :].astype(F32) + hb_ref[rows, :].astype(F32))
                 * jax.nn.gelu(gr_ref[rows, :].astype(F32)))
        y = jnp.concatenate([_rms(y_rnn) * gg_ref[:, :D_RNN],
                             _rms(o_ref[rows, :].astype(F32)) * gg_ref[:, D_RNN:]], axis=1)
        x = x_ref[rows, :] + mod(GT_M) * dot(y.astype(BF16), wo_ref[0])
        xo_ref[rows, :] = x
        h2 = (_rms(x) * gf_ref[...]) * (1.0 + mod(SC_F)) + mod(SH_F)
        h2_ref[rows, :] = h2.astype(h2_ref.dtype)
        if not moe:
            continue
        hi = h2.astype(BF16)
        lo = (h2 - hi.astype(F32)).astype(BF16)
        logits = dot(hi, wrh_ref[0]) + (dot(hi, wrl_ref[0]) + dot(lo, wrh_ref[0]))
        lane = lax.broadcasted_iota(jnp.int32, logits.shape, 1)
        logits = jnp.where(lane < N_EXPERTS, logits, -jnp.inf)
        v1 = jnp.max(logits, axis=-1, keepdims=True)
        i1 = jnp.min(jnp.where(logits == v1, lane, LANES), axis=-1, keepdims=True)
        rest = jnp.where(lane == i1, -jnp.inf, logits)
        v2 = jnp.max(rest, axis=-1, keepdims=True)
        i2 = jnp.min(jnp.where(rest == v2, lane, LANES), axis=-1, keepdims=True)
        e2 = jnp.exp(v2 - v1)
        w1 = 1.0 / (1.0 + e2)
        w2 = e2 / (1.0 + e2)
        if moe == "comb":
            comb_ref[rows, :] = jnp.where(lane == i1, w1, 0.0) + jnp.where(lane == i2, w2, 0.0)
        else:
            mf_ref[rows, :] = jnp.where(lane == 0, w1, jnp.where(lane == 1, w2, 0.0))[:, :META_W]
            picks.append((i1, i2))
    if moe == "route":
        i1 = jnp.concatenate([p[0] for p in picks], axis=0)
        i2 = jnp.concatenate([p[1] for p in picks], axis=0)
        lane = lax.broadcasted_iota(jnp.int32, (tm, LANES), 1)
        chosen = (lane == i1) | (lane == i2)
        before = (lax.broadcasted_iota(jnp.int32, (tm, tm), 0)
                  > lax.broadcasted_iota(jnp.int32, (tm, tm), 1))
        rank = dot(jnp.where(before, 1.0, 0.0).astype(BF16), jnp.where(chosen, 1.0, 0.0).astype(BF16))
        r1 = jnp.sum(jnp.where(lane == i1, rank, 0.0), axis=-1, keepdims=True).astype(jnp.int32)
        r2 = jnp.sum(jnp.where(lane == i2, rank, 0.0), axis=-1, keepdims=True).astype(jnp.int32)
        meta_i = jnp.where(lane == 0, i1, jnp.where(lane == 1, i2,
                           jnp.where(lane == 2, r1, jnp.where(lane == 3, r2, 0))))
        mi_ref[0] = meta_i.T[:META_W, :]
        counts = jnp.sum(jnp.where(chosen, 1.0, 0.0), axis=0, keepdims=True)
        cnt_ref[0] = jnp.broadcast_to(counts, cnt_ref.shape[1:]).astype(jnp.int32)


META_W = 8


def merge(x, hf, hb, gr, o, mod, g_grp, w_out, g_ffn, router, *, layer, moe_layer, row, tm, moe):
    n = x.shape[0]
    tile = lambda w: pl.BlockSpec((tm, w), lambda i: (i, 0))
    whole = lambda a: pl.BlockSpec(a.shape, lambda i: (0,) * a.ndim)
    in_specs = [tile(D_MODEL), tile(D_RNN), tile(D_RNN), tile(D_RNN), tile(D_ATTN),
                whole(mod), whole(g_grp),
                pl.BlockSpec((1, D_MODEL, D_MODEL), lambda i: (layer, 0, 0)), whole(g_ffn)]
    args = [x, hf, hb, gr, o, mod, g_grp, w_out, g_ffn]
    out_specs = [tile(D_MODEL), tile(D_MODEL)]
    out_shape = [jax.ShapeDtypeStruct((n, D_MODEL), F32), jax.ShapeDtypeStruct((n, D_MODEL), BF16)]
    if moe:
        in_specs += [pl.BlockSpec((1, D_MODEL, LANES), lambda i: (moe_layer, 0, 0))] * 2
        args += list(router)
    if moe == "comb":
        out_specs.append(tile(LANES))
        out_shape.append(jax.ShapeDtypeStruct((n, LANES), F32))
    elif moe == "route":
        out_specs += [pl.BlockSpec((1, META_W, tm), lambda i: (i, 0, 0)), tile(META_W),
                      pl.BlockSpec((1, SUBLANES, LANES), lambda i: (i, 0, 0))]
        out_shape += [jax.ShapeDtypeStruct((n // tm, META_W, tm), jnp.int32),
                      jax.ShapeDtypeStruct((n, META_W), F32),
                      jax.ShapeDtypeStruct((n // tm, SUBLANES, LANES), jnp.int32)]
    return pl.pallas_call(
        functools.partial(_merge_kernel, row=row, moe=moe),
        grid=(n // tm,),
        in_specs=in_specs,
        out_specs=out_specs,
        out_shape=out_shape,
        compiler_params=_params("arbitrary"),
        name="merge",
    )(*args)


def _ffn_kernel(*refs, row, n_exp, nf, moe, final):
    refs = list(refs)
    h_ref, x_ref, mod_ref = refs[:3]
    pos = 3
    comb_ref = None
    if moe:
        comb_ref = refs[pos]
        pos += 1
    gfin_ref = None
    if final:
        gfin_ref = refs[pos]
        pos += 1
    wg_ref, wu_ref, wd_ref, o_ref, acc_ref = refs[pos:pos + 5]
    tot_ref = refs[pos + 5] if moe else acc_ref
    e = pl.program_id(1)
    f = pl.program_id(2)

    @pl.when(f == 0)
    def _():
        acc_ref[...] = jnp.zeros_like(acc_ref)

    if moe:
        @pl.when((e == 0) & (f == 0))
        def _():
            tot_ref[...] = jnp.zeros_like(tot_ref)

    h = h_ref[...]
    a = jnp.dot(h, wg_ref[0], preferred_element_type=F32)
    b = jnp.dot(h, wu_ref[0], preferred_element_type=F32)
    act = (a * jax.nn.sigmoid(a)) * b
    acc_ref[...] += jnp.dot(act.astype(BF16), wd_ref[0], preferred_element_type=F32)

    if moe:
        @pl.when(f == nf - 1)
        def _():
            comb = comb_ref[...]
            lane = lax.broadcasted_iota(jnp.int32, comb.shape, 1)
            ce = jnp.sum(jnp.where(lane == e, comb, 0.0), axis=-1, keepdims=True)
            tot_ref[...] += ce * acc_ref[...]

    @pl.when((e == n_exp - 1) & (f == nf - 1))
    def _():
        x = x_ref[...] + mod_ref[0, row:row + 1, GT_F:GT_F + D_MODEL] * tot_ref[...]
        if final:
            x = _rms(x) * gfin_ref[...]
        o_ref[...] = x


def ffn(h2, x, mod, comb, g_final, wg, wu, wd, *, first, n_exp, row, tm, tf):
    n = x.shape[0]
    d_ff = wg.shape[2]
    nf = d_ff // tf
    moe = comb is not None
    final = g_final is not None
    tile = lambda w: pl.BlockSpec((tm, w), lambda t, e, f: (t, 0))
    in_specs = [tile(D_MODEL), tile(D_MODEL), pl.BlockSpec(mod.shape, lambda t, e, f: (0, 0, 0))]
    args = [h2, x, mod]
    if moe:
        in_specs.append(tile(LANES))
        args.append(comb)
    if final:
        in_specs.append(pl.BlockSpec((1, D_MODEL), lambda t, e, f: (0, 0)))
        args.append(g_final)
    in_specs += [
        pl.BlockSpec((1, D_MODEL, tf), lambda t, e, f: (first + e, 0, f)),
        pl.BlockSpec((1, D_MODEL, tf), lambda t, e, f: (first + e, 0, f)),
        pl.BlockSpec((1, tf, D_MODEL), lambda t, e, f: (first + e, f, 0)),
    ]
    args += [wg, wu, wd]
    scratch = [pltpu.VMEM((tm, D_MODEL), F32)]
    if moe:
        scratch.append(pltpu.VMEM((tm, D_MODEL), F32))
    return pl.pallas_call(
        functools.partial(_ffn_kernel, row=row, n_exp=n_exp, nf=nf, moe=moe, final=final),
        grid=(n // tm, n_exp, nf),
        in_specs=in_specs,
        out_specs=tile(D_MODEL),
        out_shape=jax.ShapeDtypeStruct((n, D_MODEL), F32),
        scratch_shapes=scratch,
        compiler_params=_params("arbitrary", "arbitrary", "arbitrary"),
        name="ffn",
    )(*args)


MOE_TILE = 512
MOE_TOKENS = 512
SEG_ALIGN = 16
PERM_ROWS = 2 * MOE_TOKENS + N_EXPERTS * SEG_ALIGN
CHUNKS = (64, 32, 16)


def _block_tables(cnt_tile, meta_i, tm):
    i32 = jnp.int32
    n_tiles = cnt_tile.shape[0]
    n = n_tiles * tm
    cnt = cnt_tile[:, 0, :N_EXPERTS]
    c16 = (cnt + SEG_ALIGN - 1) // SEG_ALIGN * SEG_ALIGN
    seg_off = jnp.cumsum(c16, axis=1) - c16
    run = jnp.cumsum(c16, axis=0) - c16
    total = jnp.sum(c16, axis=0)
    tiles = (total + MOE_TILE - 1) // MOE_TILE
    cum = jnp.cumsum(tiles)
    start = (cum - tiles) * MOE_TILE
    dst = start[None, :] + run
    experts = jnp.arange(N_EXPERTS, dtype=i32)
    pick = lambda e: jnp.sum(jnp.where(e[:, None, :] == experts[None, :, None], seg_off[:, :, None], 0), axis=1)
    d1 = (meta_i[:, 2] + pick(meta_i[:, 0])).astype(i32)
    d2 = (meta_i[:, 3] + pick(meta_i[:, 1])).astype(i32)
    d_rows = jnp.pad(jnp.stack([d1, d2], axis=1), ((0, 0), (0, SUBLANES - 2), (0, 0)))
    d_cols = jnp.pad(jnp.stack([d1.reshape(n), d2.reshape(n)], axis=1), ((0, 0), (0, META_W - 2)))
    n_pad = _moe_rows(n)
    n_used = cum[-1]
    t = jnp.arange(n_pad // MOE_TILE, dtype=i32)
    te = jnp.sum((t[:, None] >= cum[None, :]).astype(i32), axis=1)
    used = t < n_used
    tile_expert = jnp.where(used, te, jnp.max(jnp.where(used, te, 0))).astype(i32)
    flat = lambda a: a.reshape(-1).astype(i32)
    return dict(seg_off=flat(seg_off), c16=flat(c16), dst=flat(dst), zero_start=(start + total).astype(i32),
                zero_len=(tiles * MOE_TILE - total).astype(i32), n_used=n_used.reshape(1).astype(i32),
                tile_expert=tile_expert, d_rows=d_rows, d_cols=d_cols)


def _moe_rows(n):
    worst = 2 * n + (n // MOE_TOKENS) * N_EXPERTS * SEG_ALIGN + N_EXPERTS * MOE_TILE
    return (worst + MOE_TILE - 1) // MOE_TILE * MOE_TILE


def _segment_copies(src, dst, src_off, dst_off, rows, sems, buf, act):
    big, mid, small = CHUNKS
    n_big = lax.shift_right_logical(rows, big.bit_length() - 1)
    sem = lambda c: sems.at[buf * len(CHUNKS) + c]

    def copy(size, k_src, k_dst, c):
        return pltpu.make_async_copy(src.at[pl.ds(pl.multiple_of(k_src, SEG_ALIGN), size)],
                                     dst.at[pl.ds(pl.multiple_of(k_dst, SEG_ALIGN), size)], sem(c))

    def body(k, c):
        act(copy(big, src_off + k * big, dst_off + k * big, 0))
        return c

    lax.fori_loop(0, n_big, body, 0)
    done = n_big * big

    @pl.when((rows & mid) != 0)
    def _():
        act(copy(mid, src_off + done, dst_off + done, 1))

    @pl.when((rows & small) != 0)
    def _():
        act(copy(small, src_off + done + (rows & mid), dst_off + done + (rows & mid), 2))


ZERO_SIZES = (256, 128, 64, 32, 16)


def _zero_fill(zero_ref, dst, start, rows, sems, act):
    done = 0 * rows
    for s, size in enumerate(ZERO_SIZES):
        @pl.when((rows & size) != 0)
        def _(s=s, size=size, done=done):
            act(pltpu.make_async_copy(zero_ref.at[pl.ds(0, size)],
                                      dst.at[pl.ds(pl.multiple_of(start + done, SEG_ALIGN), size)], sems.at[s]))
        done = done + (rows & size)


def _start(copy):
    copy.start()


def _wait(copy):
    copy.wait()


def _dispatch_kernel(so_ref, c16_ref, dst_ref, zs_ref, zl_ref, nu_ref, h_ref, d_ref, x_hbm,
                      z_ref, zero_ref, sems, zsems, *, tm, n_tiles, n_pad):
    i = pl.program_id(0)
    d1 = d_ref[0, 0:1, :]
    d2 = d_ref[0, 1:2, :]
    r = lax.broadcasted_iota(jnp.int32, (PERM_ROWS, tm), 0)
    perm = jnp.where((r == d1) | (r == d2), 1.0, 0.0).astype(BF16)
    slot = i % 2
    z_ref[slot] = jnp.dot(perm, h_ref[...], preferred_element_type=F32).astype(BF16)

    def move(tile, buf, act):
        for e in range(N_EXPERTS):
            k = tile * N_EXPERTS + e
            _segment_copies(z_ref.at[buf], x_hbm, so_ref[k], dst_ref[k], c16_ref[k], sems, buf, act)

    @pl.when(i > 0)
    def _():
        move(i - 1, 1 - slot, _wait)

    move(i, slot, _start)

    @pl.when(i == n_tiles - 1)
    def _():
        move(i, slot, _wait)
        zero_ref[...] = jnp.zeros_like(zero_ref)
        for act in (_start, _wait):
            for e in range(N_EXPERTS):
                _zero_fill(zero_ref, x_hbm, zs_ref[e], zl_ref[e], zsems, act)

            def tail(t, c):
                act(pltpu.make_async_copy(zero_ref, x_hbm.at[pl.ds(pl.multiple_of(t * MOE_TILE, MOE_TILE),
                                                                   MOE_TILE)], zsems.at[len(ZERO_SIZES)]))
                return c

            lax.fori_loop(nu_ref[0], n_pad // MOE_TILE, tail, 0)


def moe_dispatch(tab, h2, *, tm):
    n = h2.shape[0]
    n_tiles = n // tm
    n_pad = _moe_rows(n)
    return pl.pallas_call(
        functools.partial(_dispatch_kernel, tm=tm, n_tiles=n_tiles, n_pad=n_pad),
        grid_spec=pltpu.PrefetchScalarGridSpec(
            num_scalar_prefetch=6,
            grid=(n_tiles,),
            in_specs=[pl.BlockSpec((tm, D_MODEL), lambda i, *_: (i, 0)),
                      pl.BlockSpec((1, SUBLANES, tm), lambda i, *_: (i, 0, 0))],
            out_specs=pl.BlockSpec(memory_space=pl.ANY),
            scratch_shapes=[pltpu.VMEM((2, PERM_ROWS, D_MODEL), BF16), pltpu.VMEM((MOE_TILE, D_MODEL), BF16),
                            pltpu.SemaphoreType.DMA((2 * len(CHUNKS),)),
                            pltpu.SemaphoreType.DMA((len(ZERO_SIZES) + 1,))],
        ),
        out_shape=jax.ShapeDtypeStruct((n_pad, D_MODEL), BF16),
        compiler_params=_params("arbitrary"),
        name="moe_dispatch",
    )(tab["seg_off"], tab["c16"], tab["dst"], tab["zero_start"], tab["zero_len"], tab["n_used"],
      h2, tab["d_rows"])


def _grouped_ffn_kernel(te_ref, nt_ref, x_ref, wg_ref, wu_ref, wd_ref, y_ref):
    t = pl.program_id(0)

    @pl.when(t < nt_ref[0])
    def _():
        x = x_ref[...]
        a = jnp.dot(x, wg_ref[0], preferred_element_type=F32)
        b = jnp.dot(x, wu_ref[0], preferred_element_type=F32)
        act = (a * jax.nn.sigmoid(a)) * b
        y_ref[...] = jnp.dot(act.astype(BF16), wd_ref[0], preferred_element_type=F32).astype(y_ref.dtype)

    @pl.when(t >= nt_ref[0])
    def _():
        y_ref[...] = jnp.zeros_like(y_ref)


def grouped_ffn(tile_expert, n_used, xs, wg, wu, wd, *, first):
    n_pad = xs.shape[0]
    d_ff = wg.shape[2]
    rows = pl.BlockSpec((MOE_TILE, D_MODEL), lambda t, te, nt: (t, 0))
    return pl.pallas_call(
        _grouped_ffn_kernel,
        grid_spec=pltpu.PrefetchScalarGridSpec(
            num_scalar_prefetch=2,
            grid=(n_pad // MOE_TILE,),
            in_specs=[rows,
                      pl.BlockSpec((1, D_MODEL, d_ff), lambda t, te, nt: (first + te[t], 0, 0)),
                      pl.BlockSpec((1, D_MODEL, d_ff), lambda t, te, nt: (first + te[t], 0, 0)),
                      pl.BlockSpec((1, d_ff, D_MODEL), lambda t, te, nt: (first + te[t], 0, 0))],
            out_specs=rows,
        ),
        out_shape=jax.ShapeDtypeStruct((n_pad, D_MODEL), BF16),
        compiler_params=_params("arbitrary"),
        name="grouped_ffn",
    )(tile_expert, n_used, xs, wg, wu, wd)


def _combine_kernel(*refs, row, tm, final):
    if final:
        so_ref, c16_ref, dst_ref, x_ref, dc_ref, mf_ref, mod_ref, gfin_ref, y_hbm, o_ref, yt_ref, sems = refs
    else:
        so_ref, c16_ref, dst_ref, x_ref, dc_ref, mf_ref, mod_ref, y_hbm, o_ref, yt_ref, sems = refs
    i = pl.program_id(0)
    n_steps = pl.num_programs(0)
    slot = i % 2

    def fetch(tile, buf, act):
        for e in range(N_EXPERTS):
            k = tile * N_EXPERTS + e
            _segment_copies(y_hbm, yt_ref.at[buf], dst_ref[k], so_ref[k], c16_ref[k], sems, buf, act)

    @pl.when(i == 0)
    def _():
        yt_ref[...] = jnp.zeros_like(yt_ref)
        fetch(0, 0, _start)

    fetch(i, slot, _wait)

    @pl.when(i + 1 < n_steps)
    def _():
        fetch(i + 1, 1 - slot, _start)

    yt = yt_ref[slot]
    r = lax.broadcasted_iota(jnp.int32, (tm, PERM_ROWS), 1)
    pick = lambda col: jnp.dot(jnp.where(r == dc_ref[:, col:col + 1], 1.0, 0.0).astype(BF16), yt,
                               preferred_element_type=F32)
    tot = mf_ref[:, 0:1] * pick(0) + mf_ref[:, 1:2] * pick(1)
    x = x_ref[...] + mod_ref[0, row:row + 1, GT_F:GT_F + D_MODEL] * tot
    if final:
        x = _rms(x) * gfin_ref[...]
    o_ref[...] = x


def moe_combine(tab, x, meta_f, mod, g_final, ys, *, row, tm):
    n = x.shape[0]
    final = g_final is not None
    tile = lambda w: pl.BlockSpec((tm, w), lambda i, *_: (i, 0))
    in_specs = [tile(D_MODEL), tile(META_W), tile(META_W), pl.BlockSpec(mod.shape, lambda i, *_: (0, 0, 0))]
    args = [x, tab["d_cols"], meta_f, mod]
    if final:
        in_specs.append(pl.BlockSpec((1, D_MODEL), lambda i, *_: (0, 0)))
        args.append(g_final)
    in_specs.append(pl.BlockSpec(memory_space=pl.ANY))
    args.append(ys)
    return pl.pallas_call(
        functools.partial(_combine_kernel, row=row, tm=tm, final=final),
        grid_spec=pltpu.PrefetchScalarGridSpec(
            num_scalar_prefetch=3,
            grid=(n // tm,),
            in_specs=in_specs,
            out_specs=tile(D_MODEL),
            scratch_shapes=[pltpu.VMEM((2, PERM_ROWS, D_MODEL), BF16),
                            pltpu.SemaphoreType.DMA((2 * len(CHUNKS),))],
        ),
        out_shape=jax.ShapeDtypeStruct((n, D_MODEL), F32),
        compiler_params=_params("arbitrary"),
        name="moe_combine",
    )(tab["seg_off"], tab["c16"], tab["dst"], *args)


def _gate_weights(w_rg):
    depth = w_rg.shape[0]
    per = RNN_CHUNK // RNN_BLOCK
    nch = D_RNN // RNN_CHUNK
    wr = w_rg.astype(BF16).reshape(depth, 2, 2, nch, per, RNN_BLOCK, 1, RNN_BLOCK)
    on_diag = jnp.eye(per, dtype=bool).reshape(per, 1, per, 1)
    bd = jnp.where(on_diag, wr, jnp.zeros((), BF16))
    bd = bd.reshape(depth, 2, 2, nch, RNN_CHUNK, RNN_CHUNK)
    return jnp.concatenate([bd[:, :, 0], bd[:, :, 1]], axis=-1)


def _tile_rows(n, want):
    return want if n % want == 0 else n


def kernel(x, c, ctx, c_ctx, w_mod, b_mod, g_mix, g_ffn, w_in, conv_w, conv_b, w_rg, b_rg, lam,
           sink, g_grp, w_out, w_ffn_gate, w_ffn_up, w_ffn_down, w_router, w_exp_gate, w_exp_up,
           w_exp_down, g_final):
    assert x.shape[0] == 1 and ctx.shape[0] == 1
    depth = w_mod.shape[0]
    seq, n_ctx = x.shape[1], ctx.shape[1]
    xl, xc = x[0], ctx[0]

    cond8 = jnp.zeros((SUBLANES, D_MODEL), F32).at[0].set(c[0]).at[1].set(c_ctx)
    mods = adaln_all(cond8, w_mod, b_mod)
    cos, sin = rope_tables(seq)

    w_in_b = w_in.astype(BF16)
    w_out_b = w_out.astype(BF16)
    wg_gate = _gate_weights(w_rg)
    ffn_w = (w_ffn_gate.astype(BF16), w_ffn_up.astype(BF16), w_ffn_down.astype(BF16))
    exp_w = tuple(w.astype(BF16).reshape((-1,) + w.shape[2:]) for w in (w_exp_gate, w_exp_up, w_exp_down))
    wr_pad = jnp.pad(w_router, ((0, 0), (0, 0), (0, LANES - N_EXPERTS)))
    wr_hi = wr_pad.astype(BF16)
    wr_lo = (wr_pad - wr_hi.astype(F32)).astype(BF16)
    zeros_h0 = jnp.zeros((SUBLANES, D_RNN), F32)

    tm = _tile_rows(seq, 512)
    tm_ffn = _tile_rows(seq, 1024)

    for l in range(depth):
        last = l == depth - 1
        moe = l % 2 == 1
        j = l // 2
        mod = mods[l:l + 1]
        gm, gf, gg = g_mix[l:l + 1], g_ffn[l:l + 1], g_grp[l:l + 1]

        xr_c, gr_c, q_c, kv_c = in_proj(xc, mod, gm, w_in_b, None, None, layer=l, row=1, tm=n_ctx)
        xr_l, gr_l, q_l, kv_l = in_proj(xl, mod, gm, w_in_b, cos, sin, layer=l, row=0, tm=tm_ffn)

        rnn_args = (conv_w[l], conv_b[l:l + 1], wg_gate, b_rg[l], lam[l])
        hf_c, hb_c, h_ctx_end = rnn_bidir(xr_c, *rnn_args, zeros_h0, layer=l, tm=n_ctx)
        hf_l, hb_l, _ = rnn_bidir(xr_l, *rnn_args, h_ctx_end, layer=l, tm=tm_ffn)

        o_l = attention(q_l, kv_l, kv_c, sink[l], band=True)

        router = (wr_hi, wr_lo) if moe else None
        weights = exp_w if moe else ffn_w
        first = j * N_EXPERTS if moe else j
        tf = 512 if moe else 1024
        merge_l = functools.partial(merge, layer=l, moe_layer=j)

        g_fin = g_final.reshape(1, D_MODEL) if last else None
        if moe:
            xl, h2, meta_i, meta_f, cnt = merge_l(xl, hf_l, hb_l, gr_l, o_l, mod, gg, w_out_b, gf, router,
                                                  row=0, tm=tm, moe="route")
            assert tm == MOE_TOKENS
            tab = _block_tables(cnt, meta_i, tm)
            xs = moe_dispatch(tab, h2, tm=tm)
            ys = grouped_ffn(tab["tile_expert"], tab["n_used"], xs, *weights, first=first)
            xl = moe_combine(tab, xl, meta_f, mod, g_fin, ys, row=0, tm=tm)
        else:
            xl, h2 = merge_l(xl, hf_l, hb_l, gr_l, o_l, mod, gg, w_out_b, gf, None, row=0, tm=tm, moe=None)
            xl = ffn(h2, xl, mod, None, g_fin, *weights, first=first, n_exp=1, row=0, tm=tm_ffn, tf=tf)

        if not last:
            o_c = attention(q_c, kv_c, kv_c, sink[l], band=False)
            res = merge_l(xc, hf_c, hb_c, gr_c, o_c, mod, gg, w_out_b, gf, router, row=1, tm=n_ctx,
                          moe="comb" if moe else None)
            xc, h2c = res[0], res[1]
            comb_c = res[2] if moe else None
            xc = ffn(h2c, xc, mod, comb_c, None, *weights, first=first, n_exp=N_EXPERTS if moe else 1,
                     row=1, tm=n_ctx, tf=tf)

    return xl[None]
```

```python
import functools

import jax
import jax.numpy as jnp
from jax import lax
from jax.experimental import pallas as pl
from jax.experimental.pallas import tpu as pltpu

F32 = jnp.float32
BF16 = jnp.bfloat16

D_MODEL = 1024
D_RNN = 512
D_ATTN = 512
D_KV = 128
HEAD_DIM = 64
N_HEADS = 8
N_KV_HEADS = 2
HEADS_PER_KV = N_HEADS // N_KV_HEADS
D_IN = 2 * D_RNN + D_ATTN + 2 * D_KV
RNN_BLOCK = 64
RNN_CHUNK = 256
CONV_W = 4
LRU_C = 8.0
WINDOW = 128
BLOCK_Q = 128
GRID_W = 64
ROPE_BASE = 10000.0
N_EXPERTS = 8
EPS = 1e-6
NEG_INF = -1e30

LANES = 128
SUBLANES = 8
VMEM_LIMIT = 56 * 1024 * 1024

SH_M, SC_M, GT_M, SH_F, SC_F, GT_F = (k * D_MODEL for k in range(6))


def _params(*sem):
    return pltpu.CompilerParams(dimension_semantics=sem, vmem_limit_bytes=VMEM_LIMIT)


def _rms(x):
    return x * lax.rsqrt(jnp.mean(x * x, axis=-1, keepdims=True) + EPS)


SUB_ROWS = 256


def _row_blocks(tm):
    step = SUB_ROWS if tm % SUB_ROWS == 0 else tm
    return [slice(r, r + step) for r in range(0, tm, step)]


def _adaln_kernel(c_ref, w_ref, b_ref, o_ref):
    c = c_ref[...]
    s = c * jax.nn.sigmoid(c)
    w = w_ref[0]
    s_hi, w_hi = s.astype(BF16), w.astype(BF16)
    s_lo = (s - s_hi.astype(F32)).astype(BF16)
    w_lo = (w - w_hi.astype(F32)).astype(BF16)
    dot = lambda a, b: jnp.dot(a, b, preferred_element_type=F32)
    o_ref[0] = dot(s_hi, w_hi) + (dot(s_hi, w_lo) + dot(s_lo, w_hi)) + b_ref[0]


def adaln_all(cond8, w_mod, b_mod):
    depth = w_mod.shape[0]
    nc = 1536
    return pl.pallas_call(
        _adaln_kernel,
        grid=(depth, 6 * D_MODEL // nc),
        in_specs=[
            pl.BlockSpec((SUBLANES, D_MODEL), lambda l, j: (0, 0)),
            pl.BlockSpec((1, D_MODEL, nc), lambda l, j: (l, 0, j)),
            pl.BlockSpec((1, 1, nc), lambda l, j: (l, 0, j)),
        ],
        out_specs=pl.BlockSpec((1, SUBLANES, nc), lambda l, j: (l, 0, j)),
        out_shape=jax.ShapeDtypeStruct((depth, SUBLANES, 6 * D_MODEL), F32),
        compiler_params=_params("arbitrary", "arbitrary"),
        name="adaln",
    )(cond8, w_mod, b_mod.reshape(depth, 1, 6 * D_MODEL))


ROPE_ROWS = SUBLANES * GRID_W


def _rope_kernel(f_ref, c_ref, s_ref):
    i = pl.program_id(0)
    lane = lax.broadcasted_iota(jnp.int32, (GRID_W, LANES), 1)
    col_axis = ((lane >> 5) & 1) == 1
    sign = jnp.where(((lane >> 4) & 1) == 0, -1.0, 1.0).astype(F32)
    freq = f_ref[...]
    rowpos = (i * SUBLANES + lax.broadcasted_iota(jnp.int32, (SUBLANES, LANES), 0)).astype(F32)
    colpos = lax.broadcasted_iota(jnp.int32, (GRID_W, LANES), 0).astype(F32)
    ang_r = rowpos * freq
    ang_c = colpos * freq
    cr, sr = jnp.cos(ang_r), jnp.sin(ang_r)
    cc, sc = jnp.cos(ang_c), jnp.sin(ang_c) * sign
    for g in range(SUBLANES):
        rows = slice(g * GRID_W, (g + 1) * GRID_W)
        c_ref[rows, :] = jnp.where(col_axis, cc, jnp.broadcast_to(cr[g:g + 1, :], (GRID_W, LANES)))
        s_ref[rows, :] = jnp.where(col_axis, sc, jnp.broadcast_to(sr[g:g + 1, :], (GRID_W, LANES)) * sign)


def rope_tables(seq):
    axis_dim = HEAD_DIM // 2
    freqs = ROPE_BASE ** (-jnp.arange(0, axis_dim, 2, dtype=F32) / axis_dim)
    freq_lane = jnp.tile(freqs, LANES // freqs.shape[0]).reshape(1, LANES)
    return pl.pallas_call(
        _rope_kernel,
        grid=(seq // ROPE_ROWS,),
        in_specs=[pl.BlockSpec((1, LANES), lambda i: (0, 0))],
        out_specs=[pl.BlockSpec((ROPE_ROWS, LANES), lambda i: (i, 0))] * 2,
        out_shape=[jax.ShapeDtypeStruct((seq, LANES), F32)] * 2,
        compiler_params=_params("arbitrary"),
        name="rope_tables",
    )(freq_lane)


def _rope_chunk(x, cos, sin):
    lane = lax.broadcasted_iota(jnp.int32, x.shape, 1)
    first_half = ((lane >> 4) & 1) == 0
    partner = jnp.where(first_half, pltpu.roll(x, LANES - 16, axis=1), pltpu.roll(x, 16, axis=1))
    return x * cos + partner * sin


def _in_proj_kernel(*refs, row, rope):
    if rope:
        x_ref, mod_ref, g_ref, w_ref, cos_ref, sin_ref, xr_ref, gr_ref, q_ref, kv_ref = refs
    else:
        x_ref, mod_ref, g_ref, w_ref, xr_ref, gr_ref, q_ref, kv_ref = refs
    shift = mod_ref[0, row:row + 1, SH_M:SH_M + D_MODEL]
    scale = mod_ref[0, row:row + 1, SC_M:SC_M + D_MODEL]
    q0 = 2 * D_RNN
    for rows in _row_blocks(x_ref.shape[0]):
        x = x_ref[rows, :]
        h = (_rms(x) * g_ref[...]) * (1.0 + scale) + shift
        p = jnp.dot(h.astype(BF16), w_ref[0], preferred_element_type=F32)
        xr_ref[rows, :] = p[:, 0:D_RNN]
        gr_ref[rows, :] = p[:, D_RNN:2 * D_RNN].astype(gr_ref.dtype)
        if rope:
            cos, sin = cos_ref[rows, :], sin_ref[rows, :]
        for c in range(D_ATTN // LANES):
            qc = p[:, q0 + c * LANES:q0 + (c + 1) * LANES]
            if rope:
                qc = _rope_chunk(qc, cos, sin)
            q_ref[rows, c * LANES:(c + 1) * LANES] = (qc * (HEAD_DIM ** -0.5)).astype(BF16)
        k = p[:, q0 + D_ATTN:q0 + D_ATTN + D_KV]
        v = p[:, q0 + D_ATTN + D_KV:q0 + D_ATTN + 2 * D_KV]
        if rope:
            k = _rope_chunk(k, cos, sin)
        kv_ref[rows, 0:LANES] = k.astype(BF16)
        kv_ref[rows, LANES:2 * LANES] = pltpu.roll(k, HEAD_DIM, axis=1).astype(BF16)
        kv_ref[rows, 2 * LANES:3 * LANES] = v.astype(BF16)
        kv_ref[rows, 3 * LANES:4 * LANES] = pltpu.roll(v, HEAD_DIM, axis=1).astype(BF16)


def in_proj(x, mod, g_mix, w_in, cos, sin, *, layer, row, tm):
    n = x.shape[0]
    rope = cos is not None
    tile = lambda w: pl.BlockSpec((tm, w), lambda i: (i, 0))
    in_specs = [
        tile(D_MODEL),
        pl.BlockSpec((1, SUBLANES, 6 * D_MODEL), lambda i: (0, 0, 0)),
        pl.BlockSpec((1, D_MODEL), lambda i: (0, 0)),
        pl.BlockSpec((1, D_MODEL, D_IN), lambda i: (layer, 0, 0)),
    ]
    args = [x, mod, g_mix, w_in]
    if rope:
        in_specs += [tile(LANES), tile(LANES)]
        args += [cos, sin]
    return pl.pallas_call(
        functools.partial(_in_proj_kernel, row=row, rope=rope),
        grid=(n // tm,),
        in_specs=in_specs,
        out_specs=[tile(D_RNN), tile(D_RNN), tile(D_ATTN), tile(4 * LANES)],
        out_shape=[
            jax.ShapeDtypeStruct((n, D_RNN), F32),
            jax.ShapeDtypeStruct((n, D_RNN), BF16),
            jax.ShapeDtypeStruct((n, D_ATTN), BF16),
            jax.ShapeDtypeStruct((n, 4 * LANES), BF16),
        ],
        compiler_params=_params("arbitrary"),
        name="in_proj",
    )(*args)


RNN_SLABS = D_RNN // LANES
SEG_PAD = 4


def _rnn_kernel(pf_ref, mf_ref, nf_ref, pb_ref, mb_ref, nb_ref, cw_ref, cb_ref, wg_ref, bg_ref,
                lam_ref, h0_ref, hf_ref, hb_ref, hl_ref, ext_ref, xs_ref, u_ref, a_ref, b_ref, hs_ref,
                cf_ref, cbk_ref, *, tm, nt):
    i = pl.program_id(0)
    seg = tm // SUBLANES
    pitch = seg + SEG_PAD
    slab = lambda c: slice(c * LANES, (c + 1) * LANES)

    @pl.when(i == 0)
    def _():
        cf_ref[...] = jnp.broadcast_to(h0_ref[0:1, :], (SUBLANES, D_RNN))
        cbk_ref[...] = jnp.broadcast_to(h0_ref[1:2, :], (SUBLANES, D_RNN))

    softplus_neg_lam = jax.nn.softplus(-lam_ref[...])
    seg_id = lax.broadcasted_iota(jnp.int32, (SUBLANES, D_RNN), 0)

    def run(d, prev_ref, main_ref, next_ref, at_start, at_end, out_ref, carry_ref):
        ext_ref[0:SUBLANES, :] = jnp.where(at_start, 0.0, prev_ref[...])
        ext_ref[SUBLANES:tm + SUBLANES, :] = main_ref[...]
        ext_ref[tm + SUBLANES:tm + 2 * SUBLANES, :] = jnp.where(at_end, 0.0, next_ref[...])
        halo = SUBLANES - CONV_W // 2
        for j in range(SUBLANES):
            for c in range(RNN_SLABS):
                xs_ref[c, j * pitch:j * pitch + seg + SUBLANES, :] = (
                    ext_ref[j * seg + halo:j * seg + halo + seg + SUBLANES, slab(c)])
        for g in range(seg):
            for c in range(RNN_SLABS):
                acc = cb_ref[:, slab(c)]
                for k in range(CONV_W):
                    acc = acc + xs_ref[c, pl.ds(g + k, SUBLANES, stride=pitch), :] * cw_ref[k:k + 1, slab(c)]
                u_ref[g * SUBLANES:(g + 1) * SUBLANES, slab(c)] = acc
        u = u_ref[...]
        ub = u.astype(BF16)
        for c in range(D_RNN // RNN_CHUNK):
            ch = slice(c * RNN_CHUNK, (c + 1) * RNN_CHUNK)
            g = jnp.dot(ub[:, ch], wg_ref[0, d, c], preferred_element_type=F32)
            r = jax.nn.sigmoid(g[:, :RNN_CHUNK] + bg_ref[d, 0:1, ch])
            ig = jax.nn.sigmoid(g[:, RNN_CHUNK:] + bg_ref[d, 1:2, ch])
            log_a = (-LRU_C * r) * softplus_neg_lam[d:d + 1, ch]
            a = jnp.exp(log_a)
            mult = jnp.sqrt(jnp.maximum(1.0 - a * a, 0.0))
            a_ref[:, ch] = a
            b_ref[:, ch] = mult * (ig * u[:, ch])

        def scan(n, hp):
            g = n if d == 0 else seg - 1 - n
            off = pl.multiple_of(g * SUBLANES, SUBLANES)
            a_g = a_ref[pl.ds(off, SUBLANES), :]
            h = a_g * hp[0] + b_ref[pl.ds(off, SUBLANES), :]
            p = a_g * hp[1]
            b_ref[pl.ds(off, SUBLANES), :] = h
            a_ref[pl.ds(off, SUBLANES), :] = p
            return h, p

        zeros = jnp.zeros((SUBLANES, D_RNN), F32)
        h_end, p_end = lax.fori_loop(0, seg, scan, (zeros, zeros + 1.0), unroll=8)

        carry = carry_ref[...]
        enter = zeros
        for n in range(SUBLANES):
            j = n if d == 0 else SUBLANES - 1 - n
            enter = jnp.where(seg_id == j, carry, enter)
            carry = jnp.broadcast_to(h_end[j:j + 1, :] + p_end[j:j + 1, :] * carry[0:1, :], (SUBLANES, D_RNN))
        carry_ref[...] = carry

        for g in range(seg):
            rows = slice(g * SUBLANES, (g + 1) * SUBLANES)
            h = b_ref[rows, :] + a_ref[rows, :] * enter
            for c in range(RNN_SLABS):
                hs_ref[c, pl.ds(g, SUBLANES, stride=pitch), :] = h[:, slab(c)]
        for j in range(SUBLANES):
            for c in range(RNN_SLABS):
                out_ref[j * seg:(j + 1) * seg, slab(c)] = (
                    hs_ref[c, j * pitch:j * pitch + seg, :].astype(out_ref.dtype))

    run(0, pf_ref, mf_ref, nf_ref, i == 0, i == nt - 1, hf_ref, cf_ref)
    run(1, pb_ref, mb_ref, nb_ref, i == nt - 1, i == 0, hb_ref, cbk_ref)

    @pl.when(i == nt - 1)
    def _():
        rows = lax.broadcasted_iota(jnp.int32, (SUBLANES, D_RNN), 0)
        hl_ref[...] = jnp.where(rows == 0, cf_ref[...], jnp.where(rows == 1, cbk_ref[...], 0.0))


def rnn_bidir(xr, conv_w, conv_b, wg, bg, lam, h0, *, layer, tm):
    n = xr.shape[0]
    nt = n // tm
    per = tm // SUBLANES
    last8 = n // SUBLANES - 1
    fwd = lambda i: i
    bwd = lambda i: nt - 1 - i
    main = lambda t: pl.BlockSpec((tm, D_RNN), lambda i: (t(i), 0))
    prev = lambda t: pl.BlockSpec((SUBLANES, D_RNN), lambda i: (jnp.maximum(t(i) * per - 1, 0), 0))
    nxt = lambda t: pl.BlockSpec((SUBLANES, D_RNN), lambda i: (jnp.minimum((t(i) + 1) * per, last8), 0))
    whole = lambda a: pl.BlockSpec(a.shape, lambda i: (0,) * a.ndim)
    return pl.pallas_call(
        functools.partial(_rnn_kernel, tm=tm, nt=nt),
        grid=(nt,),
        in_specs=[prev(fwd), main(fwd), nxt(fwd), prev(bwd), main(bwd), nxt(bwd),
                  whole(conv_w), whole(conv_b),
                  pl.BlockSpec((1,) + wg.shape[1:], lambda i: (layer, 0, 0, 0, 0)),
                  whole(bg), whole(lam), whole(h0)],
        out_specs=[main(fwd), main(bwd), pl.BlockSpec((SUBLANES, D_RNN), lambda i: (0, 0))],
        out_shape=[
            jax.ShapeDtypeStruct((n, D_RNN), BF16),
            jax.ShapeDtypeStruct((n, D_RNN), BF16),
            jax.ShapeDtypeStruct((SUBLANES, D_RNN), F32),
        ],
        scratch_shapes=[
            pltpu.VMEM((tm + 2 * SUBLANES, D_RNN), F32),
            pltpu.VMEM((RNN_SLABS, tm + SUBLANES * SEG_PAD + SUBLANES, LANES), F32),
            pltpu.VMEM((tm, D_RNN), F32),
            pltpu.VMEM((tm, D_RNN), F32),
            pltpu.VMEM((tm, D_RNN), F32),
            pltpu.VMEM((RNN_SLABS, tm + SUBLANES * SEG_PAD, LANES), F32),
            pltpu.VMEM((SUBLANES, D_RNN), F32),
            pltpu.VMEM((SUBLANES, D_RNN), F32),
        ],
        compiler_params=_params("arbitrary"),
        name="rnn_bidir",
    )(xr, xr, xr, xr, xr, xr, conv_w, conv_b, wg, bg, lam, h0)


Q_PER_STEP = 4


def _attn_kernel(*refs, nb, qps, band, n_ctx):
    if band:
        sink_ref, q_ref, kvx_ref, *band_refs, o_ref = refs
    else:
        sink_ref, q_ref, kvx_ref, o_ref = refs
    rows2 = 2 * BLOCK_Q
    upper = lax.broadcasted_iota(jnp.int32, (rows2, 1), 0) >= BLOCK_Q
    zero = jnp.zeros((), BF16)

    for sub in range(qps):
        i = pl.program_id(0) * qps + sub
        qrows = slice(sub * BLOCK_Q, (sub + 1) * BLOCK_Q)
        if band:
            kv = jnp.concatenate([kvx_ref[...]] + [band_refs[sub + b][...] for b in range(3)], axis=0)
        else:
            kv = kvx_ref[...]
        nk = kv.shape[0]
        low = lax.broadcasted_iota(jnp.int32, (nk, LANES), 1) < HEAD_DIM
        k, ks, v, vs = (kv[:, j * LANES:(j + 1) * LANES] for j in range(4))
        k_low = (jnp.where(low, k, zero), jnp.where(low, ks, zero))
        k_high = (jnp.where(low, zero, ks), jnp.where(low, zero, k))
        v_low = (jnp.where(low, v, zero), jnp.where(low, vs, zero))
        v_high = (jnp.where(low, zero, vs), jnp.where(low, zero, v))
        if band:
            r = lax.broadcasted_iota(jnp.int32, (rows2, nk), 0) & (BLOCK_Q - 1)
            jb = lax.broadcasted_iota(jnp.int32, (rows2, nk), 1) - n_ctx
            valid = (jb < 0) | ((jb >= r) & (jb <= r + 2 * WINDOW)
                                & ((jb >= BLOCK_Q) | (i > 0)) & ((jb < 2 * BLOCK_Q) | (i < nb - 1)))

        def probs(q2, kmat, sink_a, sink_b):
            s = lax.dot_general(q2, kmat, (((1,), (1,)), ((), ())), preferred_element_type=F32)
            if band:
                s = jnp.where(valid, s, NEG_INF)
            sink = jnp.where(upper, sink_b, sink_a)
            m = jnp.maximum(jnp.max(s, axis=-1, keepdims=True), sink)
            p = jnp.exp(s - m)
            den = jnp.sum(p, axis=-1, keepdims=True) + jnp.exp(sink - m)
            return p.astype(BF16), 1.0 / den

        for hk in range(N_KV_HEADS):
            c0, c1 = 2 * hk, 2 * hk + 1
            q2 = jnp.concatenate([q_ref[qrows, c0 * LANES:(c0 + 1) * LANES],
                                  q_ref[qrows, c1 * LANES:(c1 + 1) * LANES]], axis=0)
            h0 = hk * HEADS_PER_KV
            p_low, inv_low = probs(q2, k_low[hk], sink_ref[h0], sink_ref[h0 + 2])
            p_high, inv_high = probs(q2, k_high[hk], sink_ref[h0 + 1], sink_ref[h0 + 3])
            o2 = (jnp.dot(p_low, v_low[hk], preferred_element_type=F32) * inv_low
                  + jnp.dot(p_high, v_high[hk], preferred_element_type=F32) * inv_high)
            o_ref[qrows, c0 * LANES:(c0 + 1) * LANES] = o2[:BLOCK_Q].astype(o_ref.dtype)
            o_ref[qrows, c1 * LANES:(c1 + 1) * LANES] = o2[BLOCK_Q:].astype(o_ref.dtype)


def attention(q, kv, kv_ctx, sink, *, band):
    n = q.shape[0]
    nb = n // BLOCK_Q
    n_ctx = kv_ctx.shape[0]
    qps = Q_PER_STEP if nb % Q_PER_STEP == 0 else 1
    rows = qps * BLOCK_Q
    in_specs = [
        pl.BlockSpec(memory_space=pltpu.SMEM),
        pl.BlockSpec((rows, D_ATTN), lambda i: (i, 0)),
        pl.BlockSpec((n_ctx, 4 * LANES), lambda i: (0, 0)),
    ]
    args = [sink, q, kv_ctx]
    if band:
        blk = lambda b: pl.BlockSpec(
            (BLOCK_Q, 4 * LANES), lambda i: (jnp.clip(i * qps - 1 + b, 0, nb - 1), 0))
        in_specs += [blk(b) for b in range(qps + 2)]
        args += [kv] * (qps + 2)
    return pl.pallas_call(
        functools.partial(_attn_kernel, nb=nb, qps=qps, band=band, n_ctx=n_ctx),
        grid=(nb // qps,),
        in_specs=in_specs,
        out_specs=pl.BlockSpec((rows, D_ATTN), lambda i: (i, 0)),
        out_shape=jax.ShapeDtypeStruct((n, D_ATTN), BF16),
        compiler_params=_params("arbitrary"),
        name="attention",
    )(*args)


def _merge_kernel(*refs, row, moe):
    if moe == "comb":
        (x_ref, hf_ref, hb_ref, gr_ref, o_ref, mod_ref, gg_ref, wo_ref, gf_ref, wrh_ref, wrl_ref,
         xo_ref, h2_ref, comb_ref) = refs
    elif moe == "route":
        (x_ref, hf_ref, hb_ref, gr_ref, o_ref, mod_ref, gg_ref, wo_ref, gf_ref, wrh_ref, wrl_ref,
         xo_ref, h2_ref, mi_ref, mf_ref, cnt_ref) = refs
    else:
        x_ref, hf_ref, hb_ref, gr_ref, o_ref, mod_ref, gg_ref, wo_ref, gf_ref, xo_ref, h2_ref = refs
    mod = lambda off: mod_ref[0, row:row + 1, off:off + D_MODEL]
    dot = lambda a, b: jnp.dot(a, b, preferred_element_type=F32)
    tm = x_ref.shape[0]
    picks = []
    for rows in _row_blocks(tm):
        y_rnn = ((hf_ref[rows, :].astype(F32) + hb_ref[rows, :].astype(F32))
                 * jax.nn.gelu(gr_ref[rows, :].astype(F32)))
        y = jnp.concatenate([_rms(y_rnn) * gg_ref[:, :D_RNN],
                             _rms(o_ref[rows, :].astype(F32)) * gg_ref[:, D_RNN:]], axis=1)
        x = x_ref[rows, :] + mod(GT_M) * dot(y.astype(BF16), wo_ref[0])
        xo_ref[rows, :] = x
        h2 = (_rms(x) * gf_ref[...]) * (1.0 + mod(SC_F)) + mod(SH_F)
        h2_ref[rows, :] = h2.astype(h2_ref.dtype)
        if not moe:
            continue
        hi = h2.astype(BF16)
        lo = (h2 - hi.astype(F32)).astype(BF16)
        logits = dot(hi, wrh_ref[0]) + (dot(hi, wrl_ref[0]) + dot(lo, wrh_ref[0]))
        lane = lax.broadcasted_iota(jnp.int32, logits.shape, 1)
        logits = jnp.where(lane < N_EXPERTS, logits, -jnp.inf)
        v1 = jnp.max(logits, axis=-1, keepdims=True)
        i1 = jnp.min(jnp.where(logits == v1, lane, LANES), axis=-1, keepdims=True)
        rest = jnp.where(lane == i1, -jnp.inf, logits)
        v2 = jnp.max(rest, axis=-1, keepdims=True)
        i2 = jnp.min(jnp.where(rest == v2, lane, LANES), axis=-1, keepdims=True)
        e2 = jnp.exp(v2 - v1)
        w1 = 1.0 / (1.0 + e2)
        w2 = e2 / (1.0 + e2)
        if moe == "comb":
            comb_ref[rows, :] = jnp.where(lane == i1, w1, 0.0) + jnp.where(lane == i2, w2, 0.0)
        else:
            mf_ref[rows, :] = jnp.where(lane == 0, w1, jnp.where(lane == 1, w2, 0.0))[:, :META_W]
            picks.append((i1, i2))
    if moe == "route":
        i1 = jnp.concatenate([p[0] for p in picks], axis=0)
        i2 = jnp.concatenate([p[1] for p in picks], axis=0)
        lane = lax.broadcasted_iota(jnp.int32, (tm, LANES), 1)
        chosen = (lane == i1) | (lane == i2)
        before = (lax.broadcasted_iota(jnp.int32, (tm, tm), 0)
                  > lax.broadcasted_iota(jnp.int32, (tm, tm), 1))
        rank = dot(jnp.where(before, 1.0, 0.0).astype(BF16), jnp.where(chosen, 1.0, 0.0).astype(BF16))
        r1 = jnp.sum(jnp.where(lane == i1, rank, 0.0), axis=-1, keepdims=True).astype(jnp.int32)
        r2 = jnp.sum(jnp.where(lane == i2, rank, 0.0), axis=-1, keepdims=True).astype(jnp.int32)
        meta_i = jnp.where(lane == 0, i1, jnp.where(lane == 1, i2,
                           jnp.where(lane == 2, r1, jnp.where(lane == 3, r2, 0))))
        mi_ref[0] = meta_i.T[:META_W, :]
        counts = jnp.sum(jnp.where(chosen, 1.0, 0.0), axis=0, keepdims=True)
        cnt_ref[0] = jnp.broadcast_to(counts, cnt_ref.shape[1:]).astype(jnp.int32)


META_W = 8


def merge(x, hf, hb, gr, o, mod, g_grp, w_out, g_ffn, router, *, layer, moe_layer, row, tm, moe):
    n = x.shape[0]
    tile = lambda w: pl.BlockSpec((tm, w), lambda i: (i, 0))
    whole = lambda a: pl.BlockSpec(a.shape, lambda i: (0,) * a.ndim)
    in_specs = [tile(D_MODEL), tile(D_RNN), tile(D_RNN), tile(D_RNN), tile(D_ATTN),
                whole(mod), whole(g_grp),
                pl.BlockSpec((1, D_MODEL, D_MODEL), lambda i: (layer, 0, 0)), whole(g_ffn)]
    args = [x, hf, hb, gr, o, mod, g_grp, w_out, g_ffn]
    out_specs = [tile(D_MODEL), tile(D_MODEL)]
    out_shape = [jax.ShapeDtypeStruct((n, D_MODEL), F32), jax.ShapeDtypeStruct((n, D_MODEL), BF16)]
    if moe:
        in_specs += [pl.BlockSpec((1, D_MODEL, LANES), lambda i: (moe_layer, 0, 0))] * 2
        args += list(router)
    if moe == "comb":
        out_specs.append(tile(LANES))
        out_shape.append(jax.ShapeDtypeStruct((n, LANES), F32))
    elif moe == "route":
        out_specs += [pl.BlockSpec((1, META_W, tm), lambda i: (i, 0, 0)), tile(META_W),
                      pl.BlockSpec((1, SUBLANES, LANES), lambda i: (i, 0, 0))]
        out_shape += [jax.ShapeDtypeStruct((n // tm, META_W, tm), jnp.int32),
                      jax.ShapeDtypeStruct((n, META_W), F32),
                      jax.ShapeDtypeStruct((n // tm, SUBLANES, LANES), jnp.int32)]
    return pl.pallas_call(
        functools.partial(_merge_kernel, row=row, moe=moe),
        grid=(n // tm,),
        in_specs=in_specs,
        out_specs=out_specs,
        out_shape=out_shape,
        compiler_params=_params("arbitrary"),
        name="merge",
    )(*args)


def _ffn_kernel(*refs, row, n_exp, nf, moe, final):
    refs = list(refs)
    h_ref, x_ref, mod_ref = refs[:3]
    pos = 3
    comb_ref = None
    if moe:
        comb_ref = refs[pos]
        pos += 1
    gfin_ref = None
    if final:
        gfin_ref = refs[pos]
        pos += 1
    wg_ref, wu_ref, wd_ref, o_ref, acc_ref = refs[pos:pos + 5]
    tot_ref = refs[pos + 5] if moe else acc_ref
    e = pl.program_id(1)
    f = pl.program_id(2)

    @pl.when(f == 0)
    def _():
        acc_ref[...] = jnp.zeros_like(acc_ref)

    if moe:
        @pl.when((e == 0) & (f == 0))
        def _():
            tot_ref[...] = jnp.zeros_like(tot_ref)

    h = h_ref[...]
    a = jnp.dot(h, wg_ref[0], preferred_element_type=F32)
    b = jnp.dot(h, wu_ref[0], preferred_element_type=F32)
    act = (a * jax.nn.sigmoid(a)) * b
    acc_ref[...] += jnp.dot(act.astype(BF16), wd_ref[0], preferred_element_type=F32)

    if moe:
        @pl.when(f == nf - 1)
        def _():
            comb = comb_ref[...]
            lane = lax.broadcasted_iota(jnp.int32, comb.shape, 1)
            ce = jnp.sum(jnp.where(lane == e, comb, 0.0), axis=-1, keepdims=True)
            tot_ref[...] += ce * acc_ref[...]

    @pl.when((e == n_exp - 1) & (f == nf - 1))
    def _():
        x = x_ref[...] + mod_ref[0, row:row + 1, GT_F:GT_F + D_MODEL] * tot_ref[...]
        if final:
            x = _rms(x) * gfin_ref[...]
        o_ref[...] = x


def ffn(h2, x, mod, comb, g_final, wg, wu, wd, *, first, n_exp, row, tm, tf):
    n = x.shape[0]
    d_ff = wg.shape[2]
    nf = d_ff // tf
    moe = comb is not None
    final = g_final is not None
    tile = lambda w: pl.BlockSpec((tm, w), lambda t, e, f: (t, 0))
    in_specs = [tile(D_MODEL), tile(D_MODEL), pl.BlockSpec(mod.shape, lambda t, e, f: (0, 0, 0))]
    args = [h2, x, mod]
    if moe:
        in_specs.append(tile(LANES))
        args.append(comb)
    if final:
        in_specs.append(pl.BlockSpec((1, D_MODEL), lambda t, e, f: (0, 0)))
        args.append(g_final)
    in_specs += [
        pl.BlockSpec((1, D_MODEL, tf), lambda t, e, f: (first + e, 0, f)),
        pl.BlockSpec((1, D_MODEL, tf), lambda t, e, f: (first + e, 0, f)),
        pl.BlockSpec((1, tf, D_MODEL), lambda t, e, f: (first + e, f, 0)),
    ]
    args += [wg, wu, wd]
    scratch = [pltpu.VMEM((tm, D_MODEL), F32)]
    if moe:
        scratch.append(pltpu.VMEM((tm, D_MODEL), F32))
    return pl.pallas_call(
        functools.partial(_ffn_kernel, row=row, n_exp=n_exp, nf=nf, moe=moe, final=final),
        grid=(n // tm, n_exp, nf),
        in_specs=in_specs,
        out_specs=tile(D_MODEL),
        out_shape=jax.ShapeDtypeStruct((n, D_MODEL), F32),
        scratch_shapes=scratch,
        compiler_params=_params("arbitrary", "arbitrary", "arbitrary"),
        name="ffn",
    )(*args)


MOE_TILE = 512
MOE_TOKENS = 512
SEG_ALIGN = 16
PERM_ROWS = 2 * MOE_TOKENS + N_EXPERTS * SEG_ALIGN
CHUNKS = (64, 32, 16)


def _block_tables(cnt_tile, meta_i, tm):
    i32 = jnp.int32
    n_tiles = cnt_tile.shape[0]
    n = n_tiles * tm
    cnt = cnt_tile[:, 0, :N_EXPERTS]
    c16 = (cnt + SEG_ALIGN - 1) // SEG_ALIGN * SEG_ALIGN
    seg_off = jnp.cumsum(c16, axis=1) - c16
    run = jnp.cumsum(c16, axis=0) - c16
    total = jnp.sum(c16, axis=0)
    tiles = (total + MOE_TILE - 1) // MOE_TILE
    cum = jnp.cumsum(tiles)
    start = (cum - tiles) * MOE_TILE
    dst = start[None, :] + run
    experts = jnp.arange(N_EXPERTS, dtype=i32)
    pick = lambda e: jnp.sum(jnp.where(e[:, None, :] == experts[None, :, None], seg_off[:, :, None], 0), axis=1)
    d1 = (meta_i[:, 2] + pick(meta_i[:, 0])).astype(i32)
    d2 = (meta_i[:, 3] + pick(meta_i[:, 1])).astype(i32)
    d_rows = jnp.pad(jnp.stack([d1, d2], axis=1), ((0, 0), (0, SUBLANES - 2), (0, 0)))
    d_cols = jnp.pad(jnp.stack([d1.reshape(n), d2.reshape(n)], axis=1), ((0, 0), (0, META_W - 2)))
    n_pad = _moe_rows(n)
    n_used = cum[-1]
    t = jnp.arange(n_pad // MOE_TILE, dtype=i32)
    te = jnp.sum((t[:, None] >= cum[None, :]).astype(i32), axis=1)
    used = t < n_used
    tile_expert = jnp.where(used, te, jnp.max(jnp.where(used, te, 0))).astype(i32)
    flat = lambda a: a.reshape(-1).astype(i32)
    return dict(seg_off=flat(seg_off), c16=flat(c16), dst=flat(dst), zero_start=(start + total).astype(i32),
                zero_len=(tiles * MOE_TILE - total).astype(i32), n_used=n_used.reshape(1).astype(i32),
                tile_expert=tile_expert, d_rows=d_rows, d_cols=d_cols)


def _moe_rows(n):
    worst = 2 * n + (n // MOE_TOKENS) * N_EXPERTS * SEG_ALIGN + N_EXPERTS * MOE_TILE
    return (worst + MOE_TILE - 1) // MOE_TILE * MOE_TILE


def _segment_copies(src, dst, src_off, dst_off, rows, sems, buf, act):
    big, mid, small = CHUNKS
    n_big = lax.shift_right_logical(rows, big.bit_length() - 1)
    sem = lambda c: sems.at[buf * len(CHUNKS) + c]

    def copy(size, k_src, k_dst, c):
        return pltpu.make_async_copy(src.at[pl.ds(pl.multiple_of(k_src, SEG_ALIGN), size)],
                                     dst.at[pl.ds(pl.multiple_of(k_dst, SEG_ALIGN), size)], sem(c))

    def body(k, c):
        act(copy(big, src_off + k * big, dst_off + k * big, 0))
        return c

    lax.fori_loop(0, n_big, body, 0)
    done = n_big * big

    @pl.when((rows & mid) != 0)
    def _():
        act(copy(mid, src_off + done, dst_off + done, 1))

    @pl.when((rows & small) != 0)
    def _():
        act(copy(small, src_off + done + (rows & mid), dst_off + done + (rows & mid), 2))


ZERO_SIZES = (256, 128, 64, 32, 16)


def _zero_fill(zero_ref, dst, start, rows, sems, act):
    done = 0 * rows
    for s, size in enumerate(ZERO_SIZES):
        @pl.when((rows & size) != 0)
        def _(s=s, size=size, done=done):
            act(pltpu.make_async_copy(zero_ref.at[pl.ds(0, size)],
                                      dst.at[pl.ds(pl.multiple_of(start + done, SEG_ALIGN), size)], sems.at[s]))
        done = done + (rows & size)


def _start(copy):
    copy.start()


def _wait(copy):
    copy.wait()


def _dispatch_kernel(so_ref, c16_ref, dst_ref, zs_ref, zl_ref, nu_ref, h_ref, d_ref, x_hbm,
                      z_ref, zero_ref, sems, zsems, *, tm, n_tiles, n_pad):
    i = pl.program_id(0)
    d1 = d_ref[0, 0:1, :]
    d2 = d_ref[0, 1:2, :]
    r = lax.broadcasted_iota(jnp.int32, (PERM_ROWS, tm), 0)
    perm = jnp.where((r == d1) | (r == d2), 1.0, 0.0).astype(BF16)
    slot = i % 2
    z_ref[slot] = jnp.dot(perm, h_ref[...], preferred_element_type=F32).astype(BF16)

    def move(tile, buf, act):
        for e in range(N_EXPERTS):
            k = tile * N_EXPERTS + e
            _segment_copies(z_ref.at[buf], x_hbm, so_ref[k], dst_ref[k], c16_ref[k], sems, buf, act)

    @pl.when(i > 0)
    def _():
        move(i - 1, 1 - slot, _wait)

    move(i, slot, _start)

    @pl.when(i == n_tiles - 1)
    def _():
        move(i, slot, _wait)
        zero_ref[...] = jnp.zeros_like(zero_ref)
        for act in (_start, _wait):
            for e in range(N_EXPERTS):
                _zero_fill(zero_ref, x_hbm, zs_ref[e], zl_ref[e], zsems, act)

            def tail(t, c):
                act(pltpu.make_async_copy(zero_ref, x_hbm.at[pl.ds(pl.multiple_of(t * MOE_TILE, MOE_TILE),
                                                                   MOE_TILE)], zsems.at[len(ZERO_SIZES)]))
                return c

            lax.fori_loop(nu_ref[0], n_pad // MOE_TILE, tail, 0)


def moe_dispatch(tab, h2, *, tm):
    n = h2.shape[0]
    n_tiles = n // tm
    n_pad = _moe_rows(n)
    return pl.pallas_call(
        functools.partial(_dispatch_kernel, tm=tm, n_tiles=n_tiles, n_pad=n_pad),
        grid_spec=pltpu.PrefetchScalarGridSpec(
            num_scalar_prefetch=6,
            grid=(n_tiles,),
            in_specs=[pl.BlockSpec((tm, D_MODEL), lambda i, *_: (i, 0)),
                      pl.BlockSpec((1, SUBLANES, tm), lambda i, *_: (i, 0, 0))],
            out_specs=pl.BlockSpec(memory_space=pl.ANY),
            scratch_shapes=[pltpu.VMEM((2, PERM_ROWS, D_MODEL), BF16), pltpu.VMEM((MOE_TILE, D_MODEL), BF16),
                            pltpu.SemaphoreType.DMA((2 * len(CHUNKS),)),
                            pltpu.SemaphoreType.DMA((len(ZERO_SIZES) + 1,))],
        ),
        out_shape=jax.ShapeDtypeStruct((n_pad, D_MODEL), BF16),
        compiler_params=_params("arbitrary"),
        name="moe_dispatch",
    )(tab["seg_off"], tab["c16"], tab["dst"], tab["zero_start"], tab["zero_len"], tab["n_used"],
      h2, tab["d_rows"])


def _grouped_ffn_kernel(te_ref, nt_ref, x_ref, wg_ref, wu_ref, wd_ref, y_ref):
    t = pl.program_id(0)

    @pl.when(t < nt_ref[0])
    def _():
        x = x_ref[...]
        a = jnp.dot(x, wg_ref[0], preferred_element_type=F32)
        b = jnp.dot(x, wu_ref[0], preferred_element_type=F32)
        act = (a * jax.nn.sigmoid(a)) * b
        y_ref[...] = jnp.dot(act.astype(BF16), wd_ref[0], preferred_element_type=F32).astype(y_ref.dtype)

    @pl.when(t >= nt_ref[0])
    def _():
        y_ref[...] = jnp.zeros_like(y_ref)


def grouped_ffn(tile_expert, n_used, xs, wg, wu, wd, *, first):
    n_pad = xs.shape[0]
    d_ff = wg.shape[2]
    rows = pl.BlockSpec((MOE_TILE, D_MODEL), lambda t, te, nt: (t, 0))
    return pl.pallas_call(
        _grouped_ffn_kernel,
        grid_spec=pltpu.PrefetchScalarGridSpec(
            num_scalar_prefetch=2,
            grid=(n_pad // MOE_TILE,),
            in_specs=[rows,
                      pl.BlockSpec((1, D_MODEL, d_ff), lambda t, te, nt: (first + te[t], 0, 0)),
                      pl.BlockSpec((1, D_MODEL, d_ff), lambda t, te, nt: (first + te[t], 0, 0)),
                      pl.BlockSpec((1, d_ff, D_MODEL), lambda t, te, nt: (first + te[t], 0, 0))],
            out_specs=rows,
        ),
        out_shape=jax.ShapeDtypeStruct((n_pad, D_MODEL), BF16),
        compiler_params=_params("arbitrary"),
        name="grouped_ffn",
    )(tile_expert, n_used, xs, wg, wu, wd)


def _combine_kernel(*refs, row, tm, final):
    if final:
        so_ref, c16_ref, dst_ref, x_ref, dc_ref, mf_ref, mod_ref, gfin_ref, y_hbm, o_ref, yt_ref, sems = refs
    else:
        so_ref, c16_ref, dst_ref, x_ref, dc_ref, mf_ref, mod_ref, y_hbm, o_ref, yt_ref, sems = refs
    i = pl.program_id(0)
    n_steps = pl.num_programs(0)
    slot = i % 2

    def fetch(tile, buf, act):
        for e in range(N_EXPERTS):
            k = tile * N_EXPERTS + e
            _segment_copies(y_hbm, yt_ref.at[buf], dst_ref[k], so_ref[k], c16_ref[k], sems, buf, act)

    @pl.when(i == 0)
    def _():
        yt_ref[...] = jnp.zeros_like(yt_ref)
        fetch(0, 0, _start)

    fetch(i, slot, _wait)

    @pl.when(i + 1 < n_steps)
    def _():
        fetch(i + 1, 1 - slot, _start)

    yt = yt_ref[slot]
    r = lax.broadcasted_iota(jnp.int32, (tm, PERM_ROWS), 1)
    pick = lambda col: jnp.dot(jnp.where(r == dc_ref[:, col:col + 1], 1.0, 0.0).astype(BF16), yt,
                               preferred_element_type=F32)
    tot = mf_ref[:, 0:1] * pick(0) + mf_ref[:, 1:2] * pick(1)
    x = x_ref[...] + mod_ref[0, row:row + 1, GT_F:GT_F + D_MODEL] * tot
    if final:
        x = _rms(x) * gfin_ref[...]
    o_ref[...] = x


def moe_combine(tab, x, meta_f, mod, g_final, ys, *, row, tm):
    n = x.shape[0]
    final = g_final is not None
    tile = lambda w: pl.BlockSpec((tm, w), lambda i, *_: (i, 0))
    in_specs = [tile(D_MODEL), tile(META_W), tile(META_W), pl.BlockSpec(mod.shape, lambda i, *_: (0, 0, 0))]
    args = [x, tab["d_cols"], meta_f, mod]
    if final:
        in_specs.append(pl.BlockSpec((1, D_MODEL), lambda i, *_: (0, 0)))
        args.append(g_final)
    in_specs.append(pl.BlockSpec(memory_space=pl.ANY))
    args.append(ys)
    return pl.pallas_call(
        functools.partial(_combine_kernel, row=row, tm=tm, final=final),
        grid_spec=pltpu.PrefetchScalarGridSpec(
            num_scalar_prefetch=3,
            grid=(n // tm,),
            in_specs=in_specs,
            out_specs=tile(D_MODEL),
            scratch_shapes=[pltpu.VMEM((2, PERM_ROWS, D_MODEL), BF16),
                            pltpu.SemaphoreType.DMA((2 * len(CHUNKS),))],
        ),
        out_shape=jax.ShapeDtypeStruct((n, D_MODEL), F32),
        compiler_params=_params("arbitrary"),
        name="moe_combine",
    )(tab["seg_off"], tab["c16"], tab["dst"], *args)


def _gate_weights(w_rg):
    depth = w_rg.shape[0]
    per = RNN_CHUNK // RNN_BLOCK
    nch = D_RNN // RNN_CHUNK
    wr = w_rg.astype(BF16).reshape(depth, 2, 2, nch, per, RNN_BLOCK, 1, RNN_BLOCK)
    on_diag = jnp.eye(per, dtype=bool).reshape(per, 1, per, 1)
    bd = jnp.where(on_diag, wr, jnp.zeros((), BF16))
    bd = bd.reshape(depth, 2, 2, nch, RNN_CHUNK, RNN_CHUNK)
    return jnp.concatenate([bd[:, :, 0], bd[:, :, 1]], axis=-1)


def _tile_rows(n, want):
    return want if n % want == 0 else n


def kernel(x, c, ctx, c_ctx, w_mod, b_mod, g_mix, g_ffn, w_in, conv_w, conv_b, w_rg, b_rg, lam,
           sink, g_grp, w_out, w_ffn_gate, w_ffn_up, w_ffn_down, w_router, w_exp_gate, w_exp_up,
           w_exp_down, g_final):
    assert x.shape[0] == 1 and ctx.shape[0] == 1
    depth = w_mod.shape[0]
    seq, n_ctx = x.shape[1], ctx.shape[1]
    xl, xc = x[0], ctx[0]

    cond8 = jnp.zeros((SUBLANES, D_MODEL), F32).at[0].set(c[0]).at[1].set(c_ctx)
    mods = adaln_all(cond8, w_mod, b_mod)
    cos, sin = rope_tables(seq)

    w_in_b = w_in.astype(BF16)
    w_out_b = w_out.astype(BF16)
    wg_gate = _gate_weights(w_rg)
    ffn_w = (w_ffn_gate.astype(BF16), w_ffn_up.astype(BF16), w_ffn_down.astype(BF16))
    exp_w = tuple(w.astype(BF16).reshape((-1,) + w.shape[2:]) for w in (w_exp_gate, w_exp_up, w_exp_down))
    wr_pad = jnp.pad(w_router, ((0, 0), (0, 0), (0, LANES - N_EXPERTS)))
    wr_hi = wr_pad.astype(BF16)
    wr_lo = (wr_pad - wr_hi.astype(F32)).astype(BF16)
    zeros_h0 = jnp.zeros((SUBLANES, D_RNN), F32)

    tm = _tile_rows(seq, 512)
    tm_ffn = _tile_rows(seq, 1024)

    for l in range(depth):
        last = l == depth - 1
        moe = l % 2 == 1
        j = l // 2
        mod = mods[l:l + 1]
        gm, gf, gg = g_mix[l:l + 1], g_ffn[l:l + 1], g_grp[l:l + 1]

        xr_c, gr_c, q_c, kv_c = in_proj(xc, mod, gm, w_in_b, None, None, layer=l, row=1, tm=n_ctx)
        xr_l, gr_l, q_l, kv_l = in_proj(xl, mod, gm, w_in_b, cos, sin, layer=l, row=0, tm=tm_ffn)

        rnn_args = (conv_w[l], conv_b[l:l + 1], wg_gate, b_rg[l], lam[l])
        hf_c, hb_c, h_ctx_end = rnn_bidir(xr_c, *rnn_args, zeros_h0, layer=l, tm=n_ctx)
        hf_l, hb_l, _ = rnn_bidir(xr_l, *rnn_args, h_ctx_end, layer=l, tm=tm_ffn)

        o_l = attention(q_l, kv_l, kv_c, sink[l], band=True)

        router = (wr_hi, wr_lo) if moe else None
        weights = exp_w if moe else ffn_w
        first = j * N_EXPERTS if moe else j
        tf = 512 if moe else 1024
        merge_l = functools.partial(merge, layer=l, moe_layer=j)

        g_fin = g_final.reshape(1, D_MODEL) if last else None
        if moe:
            xl, h2, meta_i, meta_f, cnt = merge_l(xl, hf_l, hb_l, gr_l, o_l, mod, gg, w_out_b, gf, router,
                                                  row=0, tm=tm, moe="route")
            assert tm == MOE_TOKENS
            tab = _block_tables(cnt, meta_i, tm)
            xs = moe_dispatch(tab, h2, tm=tm)
            ys = grouped_ffn(tab["tile_expert"], tab["n_used"], xs, *weights, first=first)
            xl = moe_combine(tab, xl, meta_f, mod, g_fin, ys, row=0, tm=tm)
        else:
            xl, h2 = merge_l(xl, hf_l, hb_l, gr_l, o_l, mod, gg, w_out_b, gf, None, row=0, tm=tm, moe=None)
            xl = ffn(h2, xl, mod, None, g_fin, *weights, first=first, n_exp=1, row=0, tm=tm_ffn, tf=tf)

        if not last:
            o_c = attention(q_c, kv_c, kv_c, sink[l], band=False)
            res = merge_l(xc, hf_c, hb_c, gr_c, o_c, mod, gg, w_out_b, gf, router, row=1, tm=n_ctx,
                          moe="comb" if moe else None)
            xc, h2c = res[0], res[1]
            comb_c = res[2] if moe else None
            xc = ffn(h2c, xc, mod, comb_c, None, *weights, first=first, n_exp=N_EXPERTS if moe else 1,
                     row=1, tm=n_ctx, tf=tf)

    return xl[None]
```

```python
import functools

import jax
import jax.numpy as jnp
from jax import lax
from jax.experimental import pallas as pl
from jax.experimental.pallas import tpu as pltpu

F32 = jnp.float32
BF16 = jnp.bfloat16

D_MODEL = 1024
D_RNN = 512
D_ATTN = 512
D_KV = 128
HEAD_DIM = 64
N_HEADS = 8
N_KV_HEADS = 2
HEADS_PER_KV = N_HEADS // N_KV_HEADS
D_IN = 2 * D_RNN + D_ATTN + 2 * D_KV
RNN_BLOCK = 64
RNN_CHUNK = 256
CONV_W = 4
LRU_C = 8.0
WINDOW = 128
BLOCK_Q = 128
GRID_W = 64
ROPE_BASE = 10000.0
N_EXPERTS = 8
EPS = 1e-6
NEG_INF = -1e30

LANES = 128
SUBLANES = 8
VMEM_LIMIT = 56 * 1024 * 1024

SH_M, SC_M, GT_M, SH_F, SC_F, GT_F = (k * D_MODEL for k in range(6))


def _params(*sem):
    return pltpu.CompilerParams(dimension_semantics=sem, vmem_limit_bytes=VMEM_LIMIT)


def _rms(x):
    return x * lax.rsqrt(jnp.mean(x * x, axis=-1, keepdims=True) + EPS)


SUB_ROWS = 256


def _row_blocks(tm):
    step = SUB_ROWS if tm % SUB_ROWS == 0 else tm
    return [slice(r, r + step) for r in range(0, tm, step)]


def _adaln_kernel(c_ref, w_ref, b_ref, o_ref):
    c = c_ref[...]
    s = c * jax.nn.sigmoid(c)
    w = w_ref[0]
    s_hi, w_hi = s.astype(BF16), w.astype(BF16)
    s_lo = (s - s_hi.astype(F32)).astype(BF16)
    w_lo = (w - w_hi.astype(F32)).astype(BF16)
    dot = lambda a, b: jnp.dot(a, b, preferred_element_type=F32)
    o_ref[0] = dot(s_hi, w_hi) + (dot(s_hi, w_lo) + dot(s_lo, w_hi)) + b_ref[0]


def adaln_all(cond8, w_mod, b_mod):
    depth = w_mod.shape[0]
    nc = 1536
    return pl.pallas_call(
        _adaln_kernel,
        grid=(depth, 6 * D_MODEL // nc),
        in_specs=[
            pl.BlockSpec((SUBLANES, D_MODEL), lambda l, j: (0, 0)),
            pl.BlockSpec((1, D_MODEL, nc), lambda l, j: (l, 0, j)),
            pl.BlockSpec((1, 1, nc), lambda l, j: (l, 0, j)),
        ],
        out_specs=pl.BlockSpec((1, SUBLANES, nc), lambda l, j: (l, 0, j)),
        out_shape=jax.ShapeDtypeStruct((depth, SUBLANES, 6 * D_MODEL), F32),
        compiler_params=_params("arbitrary", "arbitrary"),
        name="adaln",
    )(cond8, w_mod, b_mod.reshape(depth, 1, 6 * D_MODEL))


ROPE_ROWS = SUBLANES * GRID_W


def _rope_kernel(f_ref, c_ref, s_ref):
    i = pl.program_id(0)
    lane = lax.broadcasted_iota(jnp.int32, (GRID_W, LANES), 1)
    col_axis = ((lane >> 5) & 1) == 1
    sign = jnp.where(((lane >> 4) & 1) == 0, -1.0, 1.0).astype(F32)
    freq = f_ref[...]
    rowpos = (i * SUBLANES + lax.broadcasted_iota(jnp.int32, (SUBLANES, LANES), 0)).astype(F32)
    colpos = lax.broadcasted_iota(jnp.int32, (GRID_W, LANES), 0).astype(F32)
    ang_r = rowpos * freq
    ang_c = colpos * freq
    cr, sr = jnp.cos(ang_r), jnp.sin(ang_r)
    cc, sc = jnp.cos(ang_c), jnp.sin(ang_c) * sign
    for g in range(SUBLANES):
        rows = slice(g * GRID_W, (g + 1) * GRID_W)
        c_ref[rows, :] = jnp.where(col_axis, cc, jnp.broadcast_to(cr[g:g + 1, :], (GRID_W, LANES)))
        s_ref[rows, :] = jnp.where(col_axis, sc, jnp.broadcast_to(sr[g:g + 1, :], (GRID_W, LANES)) * sign)


def rope_tables(seq):
    axis_dim = HEAD_DIM // 2
    freqs = ROPE_BASE ** (-jnp.arange(0, axis_dim, 2, dtype=F32) / axis_dim)
    freq_lane = jnp.tile(freqs, LANES // freqs.shape[0]).reshape(1, LANES)
    return pl.pallas_call(
        _rope_kernel,
        grid=(seq // ROPE_ROWS,),
        in_specs=[pl.BlockSpec((1, LANES), lambda i: (0, 0))],
        out_specs=[pl.BlockSpec((ROPE_ROWS, LANES), lambda i: (i, 0))] * 2,
        out_shape=[jax.ShapeDtypeStruct((seq, LANES), F32)] * 2,
        compiler_params=_params("arbitrary"),
        name="rope_tables",
    )(freq_lane)


def _rope_chunk(x, cos, sin):
    lane = lax.broadcasted_iota(jnp.int32, x.shape, 1)
    first_half = ((lane >> 4) & 1) == 0
    partner = jnp.where(first_half, pltpu.roll(x, LANES - 16, axis=1), pltpu.roll(x, 16, axis=1))
    return x * cos + partner * sin


def _in_proj_kernel(*refs, row, rope):
    if rope:
        x_ref, mod_ref, g_ref, w_ref, cos_ref, sin_ref, xr_ref, gr_ref, q_ref, kv_ref = refs
    else:
        x_ref, mod_ref, g_ref, w_ref, xr_ref, gr_ref, q_ref, kv_ref = refs
    shift = mod_ref[0, row:row + 1, SH_M:SH_M + D_MODEL]
    scale = mod_ref[0, row:row + 1, SC_M:SC_M + D_MODEL]
    q0 = 2 * D_RNN
    for rows in _row_blocks(x_ref.shape[0]):
        x = x_ref[rows, :]
        h = (_rms(x) * g_ref[...]) * (1.0 + scale) + shift
        p = jnp.dot(h.astype(BF16), w_ref[0], preferred_element_type=F32)
        xr_ref[rows, :] = p[:, 0:D_RNN]
        gr_ref[rows, :] = p[:, D_RNN:2 * D_RNN].astype(gr_ref.dtype)
        if rope:
            cos, sin = cos_ref[rows, :], sin_ref[rows, :]
        for c in range(D_ATTN // LANES):
            qc = p[:, q0 + c * LANES:q0 + (c + 1) * LANES]
            if rope:
                qc = _rope_chunk(qc, cos, sin)
            q_ref[rows, c * LANES:(c + 1) * LANES] = (qc * (HEAD_DIM ** -0.5)).astype(BF16)
        k = p[:, q0 + D_ATTN:q0 + D_ATTN + D_KV]
        v = p[:, q0 + D_ATTN + D_KV:q0 + D_ATTN + 2 * D_KV]
        if rope:
            k = _rope_chunk(k, cos, sin)
        kv_ref[rows, 0:LANES] = k.astype(BF16)
        kv_ref[rows, LANES:2 * LANES] = pltpu.roll(k, HEAD_DIM, axis=1).astype(BF16)
        kv_ref[rows, 2 * LANES:3 * LANES] = v.astype(BF16)
        kv_ref[rows, 3 * LANES:4 * LANES] = pltpu.roll(v, HEAD_DIM, axis=1).astype(BF16)


def in_proj(x, mod, g_mix, w_in, cos, sin, *, layer, row, tm):
    n = x.shape[0]
    rope = cos is not None
    tile = lambda w: pl.BlockSpec((tm, w), lambda i: (i, 0))
    in_specs = [
        tile(D_MODEL),
        pl.BlockSpec((1, SUBLANES, 6 * D_MODEL), lambda i: (0, 0, 0)),
        pl.BlockSpec((1, D_MODEL), lambda i: (0, 0)),
        pl.BlockSpec((1, D_MODEL, D_IN), lambda i: (layer, 0, 0)),
    ]
    args = [x, mod, g_mix, w_in]
    if rope:
        in_specs += [tile(LANES), tile(LANES)]
        args += [cos, sin]
    return pl.pallas_call(
        functools.partial(_in_proj_kernel, row=row, rope=rope),
        grid=(n // tm,),
        in_specs=in_specs,
        out_specs=[tile(D_RNN), tile(D_RNN), tile(D_ATTN), tile(4 * LANES)],
        out_shape=[
            jax.ShapeDtypeStruct((n, D_RNN), F32),
            jax.ShapeDtypeStruct((n, D_RNN), BF16),
            jax.ShapeDtypeStruct((n, D_ATTN), BF16),
            jax.ShapeDtypeStruct((n, 4 * LANES), BF16),
        ],
        compiler_params=_params("arbitrary"),
        name="in_proj",
    )(*args)


RNN_SLABS = D_RNN // LANES
SEG_PAD = 4


def _rnn_kernel(pf_ref, mf_ref, nf_ref, pb_ref, mb_ref, nb_ref, cw_ref, cb_ref, wg_ref, bg_ref,
                lam_ref, h0_ref, hf_ref, hb_ref, hl_ref, ext_ref, xs_ref, u_ref, a_ref, b_ref, hs_ref,
                cf_ref, cbk_ref, *, tm, nt):
    i = pl.program_id(0)
    seg = tm // SUBLANES
    pitch = seg + SEG_PAD
    slab = lambda c: slice(c * LANES, (c + 1) * LANES)

    @pl.when(i == 0)
    def _():
        cf_ref[...] = jnp.broadcast_to(h0_ref[0:1, :], (SUBLANES, D_RNN))
        cbk_ref[...] = jnp.broadcast_to(h0_ref[1:2, :], (SUBLANES, D_RNN))

    softplus_neg_lam = jax.nn.softplus(-lam_ref[...])
    seg_id = lax.broadcasted_iota(jnp.int32, (SUBLANES, D_RNN), 0)

    def run(d, prev_ref, main_ref, next_ref, at_start, at_end, out_ref, carry_ref):
        ext_ref[0:SUBLANES, :] = jnp.where(at_start, 0.0, prev_ref[...])
        ext_ref[SUBLANES:tm + SUBLANES, :] = main_ref[...]
        ext_ref[tm + SUBLANES:tm + 2 * SUBLANES, :] = jnp.where(at_end, 0.0, next_ref[...])
        halo = SUBLANES - CONV_W // 2
        for j in range(SUBLANES):
            for c in range(RNN_SLABS):
                xs_ref[c, j * pitch:j * pitch + seg + SUBLANES, :] = (
                    ext_ref[j * seg + halo:j * seg + halo + seg + SUBLANES, slab(c)])
        for g in range(seg):
            for c in range(RNN_SLABS):
                acc = cb_ref[:, slab(c)]
                for k in range(CONV_W):
                    acc = acc + xs_ref[c, pl.ds(g + k, SUBLANES, stride=pitch), :] * cw_ref[k:k + 1, slab(c)]
                u_ref[g * SUBLANES:(g + 1) * SUBLANES, slab(c)] = acc
        u = u_ref[...]
        ub = u.astype(BF16)
        for c in range(D_RNN // RNN_CHUNK):
            ch = slice(c * RNN_CHUNK, (c + 1) * RNN_CHUNK)
            g = jnp.dot(ub[:, ch], wg_ref[0, d, c], preferred_element_type=F32)
            r = jax.nn.sigmoid(g[:, :RNN_CHUNK] + bg_ref[d, 0:1, ch])
            ig = jax.nn.sigmoid(g[:, RNN_CHUNK:] + bg_ref[d, 1:2, ch])
            log_a = (-LRU_C * r) * softplus_neg_lam[d:d + 1, ch]
            a = jnp.exp(log_a)
            mult = jnp.sqrt(jnp.maximum(1.0 - a * a, 0.0))
            a_ref[:, ch] = a
            b_ref[:, ch] = mult * (ig * u[:, ch])

        def scan(n, hp):
            g = n if d == 0 else seg - 1 - n
            off = pl.multiple_of(g * SUBLANES, SUBLANES)
            a_g = a_ref[pl.ds(off, SUBLANES), :]
            h = a_g * hp[0] + b_ref[pl.ds(off, SUBLANES), :]
            p = a_g * hp[1]
            b_ref[pl.ds(off, SUBLANES), :] = h
            a_ref[pl.ds(off, SUBLANES), :] = p
            return h, p

        zeros = jnp.zeros((SUBLANES, D_RNN), F32)
        h_end, p_end = lax.fori_loop(0, seg, scan, (zeros, zeros + 1.0), unroll=8)

        carry = carry_ref[...]
        enter = zeros
        for n in range(SUBLANES):
            j = n if d == 0 else SUBLANES - 1 - n
            enter = jnp.where(seg_id == j, carry, enter)
            carry = jnp.broadcast_to(h_end[j:j + 1, :] + p_end[j:j + 1, :] * carry[0:1, :], (SUBLANES, D_RNN))
        carry_ref[...] = carry

        for g in range(seg):
            rows = slice(g * SUBLANES, (g + 1) * SUBLANES)
            h = b_ref[rows, :] + a_ref[rows, :] * enter
            for c in range(RNN_SLABS):
                hs_ref[c, pl.ds(g, SUBLANES, stride=pitch), :] = h[:, slab(c)]
        for j in range(SUBLANES):
            for c in range(RNN_SLABS):
                out_ref[j * seg:(j + 1) * seg, slab(c)] = (
                    hs_ref[c, j * pitch:j * pitch + seg, :].astype(out_ref.dtype))

    run(0, pf_ref, mf_ref, nf_ref, i == 0, i == nt - 1, hf_ref, cf_ref)
    run(1, pb_ref, mb_ref, nb_ref, i == nt - 1, i == 0, hb_ref, cbk_ref)

    @pl.when(i == nt - 1)
    def _():
        rows = lax.broadcasted_iota(jnp.int32, (SUBLANES, D_RNN), 0)
        hl_ref[...] = jnp.where(rows == 0, cf_ref[...], jnp.where(rows == 1, cbk_ref[...], 0.0))


def rnn_bidir(xr, conv_w, conv_b, wg, bg, lam, h0, *, layer, tm):
    n = xr.shape[0]
    nt = n // tm
    per = tm // SUBLANES
    last8 = n // SUBLANES - 1
    fwd = lambda i: i
    bwd = lambda i: nt - 1 - i
    main = lambda t: pl.BlockSpec((tm, D_RNN), lambda i: (t(i), 0))
    prev = lambda t: pl.BlockSpec((SUBLANES, D_RNN), lambda i: (jnp.maximum(t(i) * per - 1, 0), 0))
    nxt = lambda t: pl.BlockSpec((SUBLANES, D_RNN), lambda i: (jnp.minimum((t(i) + 1) * per, last8), 0))
    whole = lambda a: pl.BlockSpec(a.shape, lambda i: (0,) * a.ndim)
    return pl.pallas_call(
        functools.partial(_rnn_kernel, tm=tm, nt=nt),
        grid=(nt,),
        in_specs=[prev(fwd), main(fwd), nxt(fwd), prev(bwd), main(bwd), nxt(bwd),
                  whole(conv_w), whole(conv_b),
                  pl.BlockSpec((1,) + wg.shape[1:], lambda i: (layer, 0, 0, 0, 0)),
                  whole(bg), whole(lam), whole(h0)],
        out_specs=[main(fwd), main(bwd), pl.BlockSpec((SUBLANES, D_RNN), lambda i: (0, 0))],
        out_shape=[
            jax.ShapeDtypeStruct((n, D_RNN), BF16),
            jax.ShapeDtypeStruct((n, D_RNN), BF16),
            jax.ShapeDtypeStruct((SUBLANES, D_RNN), F32),
        ],
        scratch_shapes=[
            pltpu.VMEM((tm + 2 * SUBLANES, D_RNN), F32),
            pltpu.VMEM((RNN_SLABS, tm + SUBLANES * SEG_PAD + SUBLANES, LANES), F32),
            pltpu.VMEM((tm, D_RNN), F32),
            pltpu.VMEM((tm, D_RNN), F32),
            pltpu.VMEM((tm, D_RNN), F32),
            pltpu.VMEM((RNN_SLABS, tm + SUBLANES * SEG_PAD, LANES), F32),
            pltpu.VMEM((SUBLANES, D_RNN), F32),
            pltpu.VMEM((SUBLANES, D_RNN), F32),
        ],
        compiler_params=_params("arbitrary"),
        name="rnn_bidir",
    )(xr, xr, xr, xr, xr, xr, conv_w, conv_b, wg, bg, lam, h0)


Q_PER_STEP = 4


def _attn_kernel(*refs, nb, qps, band, n_ctx):
    if band:
        sink_ref, q_ref, kvx_ref, *band_refs, o_ref = refs
    else:
        sink_ref, q_ref, kvx_ref, o_ref = refs
    rows2 = 2 * BLOCK_Q
    upper = lax.broadcasted_iota(jnp.int32, (rows2, 1), 0) >= BLOCK_Q
    zero = jnp.zeros((), BF16)

    for sub in range(qps):
        i = pl.program_id(0) * qps + sub
        qrows = slice(sub * BLOCK_Q, (sub + 1) * BLOCK_Q)
        if band:
            kv = jnp.concatenate([kvx_ref[...]] + [band_refs[sub + b][...] for b in range(3)], axis=0)
        else:
            kv = kvx_ref[...]
        nk = kv.shape[0]
        low = lax.broadcasted_iota(jnp.int32, (nk, LANES), 1) < HEAD_DIM
        k, ks, v, vs = (kv[:, j * LANES:(j + 1) * LANES] for j in range(4))
        k_low = (jnp.where(low, k, zero), jnp.where(low, ks, zero))
        k_high = (jnp.where(low, zero, ks), jnp.where(low, zero, k))
        v_low = (jnp.where(low, v, zero), jnp.where(low, vs, zero))
        v_high = (jnp.where(low, zero, vs), jnp.where(low, zero, v))
        if band:
            r = lax.broadcasted_iota(jnp.int32, (rows2, nk), 0) & (BLOCK_Q - 1)
            jb = lax.broadcasted_iota(jnp.int32, (rows2, nk), 1) - n_ctx
            valid = (jb < 0) | ((jb >= r) & (jb <= r + 2 * WINDOW)
                                & ((jb >= BLOCK_Q) | (i > 0)) & ((jb < 2 * BLOCK_Q) | (i < nb - 1)))

        def probs(q2, kmat, sink_a, sink_b):
            s = lax.dot_general(q2, kmat, (((1,), (1,)), ((), ())), preferred_element_type=F32)
            if band:
                s = jnp.where(valid, s, NEG_INF)
            sink = jnp.where(upper, sink_b, sink_a)
            m = jnp.maximum(jnp.max(s, axis=-1, keepdims=True), sink)
            p = jnp.exp(s - m)
            den = jnp.sum(p, axis=-1, keepdims=True) + jnp.exp(sink - m)
            return p.astype(BF16), 1.0 / den

        for hk in range(N_KV_HEADS):
            c0, c1 = 2 * hk, 2 * hk + 1
            q2 = jnp.concatenate([q_ref[qrows, c0 * LANES:(c0 + 1) * LANES],
                                  q_ref[qrows, c1 * LANES:(c1 + 1) * LANES]], axis=0)
            h0 = hk * HEADS_PER_KV
            p_low, inv_low = probs(q2, k_low[hk], sink_ref[h0], sink_ref[h0 + 2])
            p_high, inv_high = probs(q2, k_high[hk], sink_ref[h0 + 1], sink_ref[h0 + 3])
            o2 = (jnp.dot(p_low, v_low[hk], preferred_element_type=F32) * inv_low
                  + jnp.dot(p_high, v_high[hk], preferred_element_type=F32) * inv_high)
            o_ref[qrows, c0 * LANES:(c0 + 1) * LANES] = o2[:BLOCK_Q].astype(o_ref.dtype)
            o_ref[qrows, c1 * LANES:(c1 + 1) * LANES] = o2[BLOCK_Q:].astype(o_ref.dtype)


def attention(q, kv, kv_ctx, sink, *, band):
    n = q.shape[0]
    nb = n // BLOCK_Q
    n_ctx = kv_ctx.shape[0]
    qps = Q_PER_STEP if nb % Q_PER_STEP == 0 else 1
    rows = qps * BLOCK_Q
    in_specs = [
        pl.BlockSpec(memory_space=pltpu.SMEM),
        pl.BlockSpec((rows, D_ATTN), lambda i: (i, 0)),
        pl.BlockSpec((n_ctx, 4 * LANES), lambda i: (0, 0)),
    ]
    args = [sink, q, kv_ctx]
    if band:
        blk = lambda b: pl.BlockSpec(
            (BLOCK_Q, 4 * LANES), lambda i: (jnp.clip(i * qps - 1 + b, 0, nb - 1), 0))
        in_specs += [blk(b) for b in range(qps + 2)]
        args += [kv] * (qps + 2)
    return pl.pallas_call(
        functools.partial(_attn_kernel, nb=nb, qps=qps, band=band, n_ctx=n_ctx),
        grid=(nb // qps,),
        in_specs=in_specs,
        out_specs=pl.BlockSpec((rows, D_ATTN), lambda i: (i, 0)),
        out_shape=jax.ShapeDtypeStruct((n, D_ATTN), BF16),
        compiler_params=_params("arbitrary"),
        name="attention",
    )(*args)


def _merge_kernel(*refs, row, moe):
    if moe == "comb":
        (x_ref, hf_ref, hb_ref, gr_ref, o_ref, mod_ref, gg_ref, wo_ref, gf_ref, wrh_ref, wrl_ref,
         xo_ref, h2_ref, comb_ref) = refs
    elif moe == "route":
        (x_ref, hf_ref, hb_ref, gr_ref, o_ref, mod_ref, gg_ref, wo_ref, gf_ref, wrh_ref, wrl_ref,
         xo_ref, h2_ref, mi_ref, mf_ref, cnt_ref) = refs
    else:
        x_ref, hf_ref, hb_ref, gr_ref, o_ref, mod_ref, gg_ref, wo_ref, gf_ref, xo_ref, h2_ref = refs
    mod = lambda off: mod_ref[0, row:row + 1, off:off + D_MODEL]
    dot = lambda a, b: jnp.dot(a, b, preferred_element_type=F32)
    tm = x_ref.shape[0]
    picks = []
    for rows in _row_blocks(tm):
        y_rnn = ((hf_ref[rows, :].astype(F32) + hb_ref[rows, :].astype(F32))
                 * jax.nn.gelu(gr_ref[rows, :].astype(F32)))
        y = jnp.concatenate([_rms(y_rnn) * gg_ref[:, :D_RNN],
                             _rms(o_ref[rows, :].astype(F32)) * gg_ref[:, D_RNN:]], axis=1)
        x = x_ref[rows, :] + mod(GT_M) * dot(y.astype(BF16), wo_ref[0])
        xo_ref[rows, :] = x
        h2 = (_rms(x) * gf_ref[...]) * (1.0 + mod(SC_F)) + mod(SH_F)
        h2_ref[rows, :] = h2.astype(h2_ref.dtype)
        if not moe:
            continue
        hi = h2.astype(BF16)
        lo = (h2 - hi.astype(F32)).astype(BF16)
        logits = dot(hi, wrh_ref[0]) + (dot(hi, wrl_ref[0]) + dot(lo, wrh_ref[0]))
        lane = lax.broadcasted_iota(jnp.int32, logits.shape, 1)
        logits = jnp.where(lane < N_EXPERTS, logits, -jnp.inf)
        v1 = jnp.max(logits, axis=-1, keepdims=True)
        i1 = jnp.min(jnp.where(logits == v1, lane, LANES), axis=-1, keepdims=True)
        rest = jnp.where(lane == i1, -jnp.inf, logits)
        v2 = jnp.max(rest, axis=-1, keepdims=True)
        i2 = jnp.min(jnp.where(rest == v2, lane, LANES), axis=-1, keepdims=True)
        e2 = jnp.exp(v2 - v1)
        w1 = 1.0 / (1.0 + e2)
        w2 = e2 / (1.0 + e2)
        if moe == "comb":
            comb_ref[rows, :] = jnp.where(lane == i1, w1, 0.0) + jnp.where(lane == i2, w2, 0.0)
        else:
            mf_ref[rows, :] = jnp.where(lane == 0, w1, jnp.where(lane == 1, w2, 0.0))[:, :META_W]
            picks.append((i1, i2))
    if moe == "route":
        i1 = jnp.concatenate([p[0] for p in picks], axis=0)
        i2 = jnp.concatenate([p[1] for p in picks], axis=0)
        lane = lax.broadcasted_iota(jnp.int32, (tm, LANES), 1)
        chosen = (lane == i1) | (lane == i2)
        before = (lax.broadcasted_iota(jnp.int32, (tm, tm), 0)
                  > lax.broadcasted_iota(jnp.int32, (tm, tm), 1))
        rank = dot(jnp.where(before, 1.0, 0.0).astype(BF16), jnp.where(chosen, 1.0, 0.0).astype(BF16))
        r1 = jnp.sum(jnp.where(lane == i1, rank, 0.0), axis=-1, keepdims=True).astype(jnp.int32)
        r2 = jnp.sum(jnp.where(lane == i2, rank, 0.0), axis=-1, keepdims=True).astype(jnp.int32)
        meta_i = jnp.where(lane == 0, i1, jnp.where(lane == 1, i2,
                           jnp.where(lane == 2, r1, jnp.where(lane == 3, r2, 0))))
        mi_ref[0] = meta_i.T[:META_W, :]
        counts = jnp.sum(jnp.where(chosen, 1.0, 0.0), axis=0, keepdims=True)
        cnt_ref[0] = jnp.broadcast_to(counts, cnt_ref.shape[1:]).astype(jnp.int32)


META_W = 8


def merge(x, hf, hb, gr, o, mod, g_grp, w_out, g_ffn, router, *, layer, moe_layer, row, tm, moe):
    n = x.shape[0]
    tile = lambda w: pl.BlockSpec((tm, w), lambda i: (i, 0))
    whole = lambda a: pl.BlockSpec(a.shape, lambda i: (0,) * a.ndim)
    in_specs = [tile(D_MODEL), tile(D_RNN), tile(D_RNN), tile(D_RNN), tile(D_ATTN),
                whole(mod), whole(g_grp),
                pl.BlockSpec((1, D_MODEL, D_MODEL), lambda i: (layer, 0, 0)), whole(g_ffn)]
    args = [x, hf, hb, gr, o, mod, g_grp, w_out, g_ffn]
    out_specs = [tile(D_MODEL), tile(D_MODEL)]
    out_shape = [jax.ShapeDtypeStruct((n, D_MODEL), F32), jax.ShapeDtypeStruct((n, D_MODEL), BF16)]
    if moe:
        in_specs += [pl.BlockSpec((1, D_MODEL, LANES), lambda i: (moe_layer, 0, 0))] * 2
        args += list(router)
    if moe == "comb":
        out_specs.append(tile(LANES))
        out_shape.append(jax.ShapeDtypeStruct((n, LANES), F32))
    elif moe == "route":
        out_specs += [pl.BlockSpec((1, META_W, tm), lambda i: (i, 0, 0)), tile(META_W),
                      pl.BlockSpec((1, SUBLANES, LANES), lambda i: (i, 0, 0))]
        out_shape += [jax.ShapeDtypeStruct((n // tm, META_W, tm), jnp.int32),
                      jax.ShapeDtypeStruct((n, META_W), F32),
                      jax.ShapeDtypeStruct((n // tm, SUBLANES, LANES), jnp.int32)]
    return pl.pallas_call(
        functools.partial(_merge_kernel, row=row, moe=moe),
        grid=(n // tm,),
        in_specs=in_specs,
        out_specs=out_specs,
        out_shape=out_shape,
        compiler_params=_params("arbitrary"),
        name="merge",
    )(*args)


def _ffn_kernel(*refs, row, n_exp, nf, moe, final):
    refs = list(refs)
    h_ref, x_ref, mod_ref = refs[:3]
    pos = 3
    comb_ref = None
    if moe:
        comb_ref = refs[pos]
        pos += 1
    gfin_ref = None
    if final:
        gfin_ref = refs[pos]
        pos += 1
    wg_ref, wu_ref, wd_ref, o_ref, acc_ref = refs[pos:pos + 5]
    tot_ref = refs[pos + 5] if moe else acc_ref
    e = pl.program_id(1)
    f = pl.program_id(2)

    @pl.when(f == 0)
    def _():
        acc_ref[...] = jnp.zeros_like(acc_ref)

    if moe:
        @pl.when((e == 0) & (f == 0))
        def _():
            tot_ref[...] = jnp.zeros_like(tot_ref)

    h = h_ref[...]
    a = jnp.dot(h, wg_ref[0].astype(BF16), preferred_element_type=F32)
    b = jnp.dot(h, wu_ref[0].astype(BF16), preferred_element_type=F32)
    act = (a * jax.nn.sigmoid(a)) * b
    acc_ref[...] += jnp.dot(act.astype(BF16), wd_ref[0].astype(BF16), preferred_element_type=F32)

    if moe:
        @pl.when(f == nf - 1)
        def _():
            comb = comb_ref[...]
            lane = lax.broadcasted_iota(jnp.int32, comb.shape, 1)
            ce = jnp.sum(jnp.where(lane == e, comb, 0.0), axis=-1, keepdims=True)
            tot_ref[...] += ce * acc_ref[...]

    @pl.when((e == n_exp - 1) & (f == nf - 1))
    def _():
        x = x_ref[...] + mod_ref[0, row:row + 1, GT_F:GT_F + D_MODEL] * tot_ref[...]
        if final:
            x = _rms(x) * gfin_ref[...]
        o_ref[...] = x


def ffn(h2, x, mod, comb, g_final, wg, wu, wd, *, first, n_exp, row, tm, tf):
    n = x.shape[0]
    d_ff = wg.shape[2]
    nf = d_ff // tf
    moe = comb is not None
    final = g_final is not None
    tile = lambda w: pl.BlockSpec((tm, w), lambda t, e, f: (t, 0))
    in_specs = [tile(D_MODEL), tile(D_MODEL), pl.BlockSpec(mod.shape, lambda t, e, f: (0, 0, 0))]
    args = [h2, x, mod]
    if moe:
        in_specs.append(tile(LANES))
        args.append(comb)
    if final:
        in_specs.append(pl.BlockSpec((1, D_MODEL), lambda t, e, f: (0, 0)))
        args.append(g_final)
    in_specs += [
        pl.BlockSpec((1, D_MODEL, tf), lambda t, e, f: (first + e, 0, f)),
        pl.BlockSpec((1, D_MODEL, tf), lambda t, e, f: (first + e, 0, f)),
        pl.BlockSpec((1, tf, D_MODEL), lambda t, e, f: (first + e, f, 0)),
    ]
    args += [wg, wu, wd]
    scratch = [pltpu.VMEM((tm, D_MODEL), F32)]
    if moe:
        scratch.append(pltpu.VMEM((tm, D_MODEL), F32))
    return pl.pallas_call(
        functools.partial(_ffn_kernel, row=row, n_exp=n_exp, nf=nf, moe=moe, final=final),
        grid=(n // tm, n_exp, nf),
        in_specs=in_specs,
        out_specs=tile(D_MODEL),
        out_shape=jax.ShapeDtypeStruct((n, D_MODEL), F32),
        scratch_shapes=scratch,
        compiler_params=_params("arbitrary", "arbitrary", "arbitrary"),
        name="ffn",
    )(*args)


MOE_TILE = 512
MOE_TOKENS = 512
SEG_ALIGN = 16
PERM_ROWS = 2 * MOE_TOKENS + N_EXPERTS * SEG_ALIGN
CHUNKS = (64, 32, 16)


def _block_tables(cnt_tile, meta_i, tm):
    i32 = jnp.int32
    n_tiles = cnt_tile.shape[0]
    n = n_tiles * tm
    cnt = cnt_tile[:, 0, :N_EXPERTS]
    c16 = (cnt + SEG_ALIGN - 1) // SEG_ALIGN * SEG_ALIGN
    seg_off = jnp.cumsum(c16, axis=1) - c16
    run = jnp.cumsum(c16, axis=0) - c16
    total = jnp.sum(c16, axis=0)
    tiles = (total + MOE_TILE - 1) // MOE_TILE
    cum = jnp.cumsum(tiles)
    start = (cum - tiles) * MOE_TILE
    dst = start[None, :] + run
    experts = jnp.arange(N_EXPERTS, dtype=i32)
    pick = lambda e: jnp.sum(jnp.where(e[:, None, :] == experts[None, :, None], seg_off[:, :, None], 0), axis=1)
    d1 = (meta_i[:, 2] + pick(meta_i[:, 0])).astype(i32)
    d2 = (meta_i[:, 3] + pick(meta_i[:, 1])).astype(i32)
    d_rows = jnp.pad(jnp.stack([d1, d2], axis=1), ((0, 0), (0, SUBLANES - 2), (0, 0)))
    d_cols = jnp.pad(jnp.stack([d1.reshape(n), d2.reshape(n)], axis=1), ((0, 0), (0, META_W - 2)))
    n_pad = _moe_rows(n)
    n_used = cum[-1]
    t = jnp.arange(n_pad // MOE_TILE, dtype=i32)
    te = jnp.sum((t[:, None] >= cum[None, :]).astype(i32), axis=1)
    used = t < n_used
    tile_expert = jnp.where(used, te, jnp.max(jnp.where(used, te, 0))).astype(i32)
    flat = lambda a: a.reshape(-1).astype(i32)
    return dict(seg_off=flat(seg_off), c16=flat(c16), dst=flat(dst), zero_start=(start + total).astype(i32),
                zero_len=(tiles * MOE_TILE - total).astype(i32), n_used=n_used.reshape(1).astype(i32),
                tile_expert=tile_expert, d_rows=d_rows, d_cols=d_cols)


def _moe_rows(n):
    worst = 2 * n + (n // MOE_TOKENS) * N_EXPERTS * SEG_ALIGN + N_EXPERTS * MOE_TILE
    return (worst + MOE_TILE - 1) // MOE_TILE * MOE_TILE


def _segment_copies(src, dst, src_off, dst_off, rows, sems, buf, act):
    big, mid, small = CHUNKS
    n_big = lax.shift_right_logical(rows, big.bit_length() - 1)
    sem = lambda c: sems.at[buf * len(CHUNKS) + c]

    def copy(size, k_src, k_dst, c):
        return pltpu.make_async_copy(src.at[pl.ds(pl.multiple_of(k_src, SEG_ALIGN), size)],
                                     dst.at[pl.ds(pl.multiple_of(k_dst, SEG_ALIGN), size)], sem(c))

    def body(k, c):
        act(copy(big, src_off + k * big, dst_off + k * big, 0))
        return c

    lax.fori_loop(0, n_big, body, 0)
    done = n_big * big

    @pl.when((rows & mid) != 0)
    def _():
        act(copy(mid, src_off + done, dst_off + done, 1))

    @pl.when((rows & small) != 0)
    def _():
        act(copy(small, src_off + done + (rows & mid), dst_off + done + (rows & mid), 2))


ZERO_SIZES = (256, 128, 64, 32, 16)


def _zero_fill(zero_ref, dst, start, rows, sems, act):
    done = 0 * rows
    for s, size in enumerate(ZERO_SIZES):
        @pl.when((rows & size) != 0)
        def _(s=s, size=size, done=done):
            act(pltpu.make_async_copy(zero_ref.at[pl.ds(0, size)],
                                      dst.at[pl.ds(pl.multiple_of(start + done, SEG_ALIGN), size)], sems.at[s]))
        done = done + (rows & size)


def _start(copy):
    copy.start()


def _wait(copy):
    copy.wait()


def _dispatch_kernel(so_ref, c16_ref, dst_ref, zs_ref, zl_ref, nu_ref, h_ref, d_ref, x_hbm,
                      z_ref, zero_ref, sems, zsems, *, tm, n_tiles, n_pad):
    i = pl.program_id(0)
    d1 = d_ref[0, 0:1, :]
    d2 = d_ref[0, 1:2, :]
    r = lax.broadcasted_iota(jnp.int32, (PERM_ROWS, tm), 0)
    perm = jnp.where((r == d1) | (r == d2), 1.0, 0.0).astype(BF16)
    slot = i % 2
    z_ref[slot] = jnp.dot(perm, h_ref[...], preferred_element_type=F32).astype(BF16)

    def move(tile, buf, act):
        for e in range(N_EXPERTS):
            k = tile * N_EXPERTS + e
            _segment_copies(z_ref.at[buf], x_hbm, so_ref[k], dst_ref[k], c16_ref[k], sems, buf, act)

    @pl.when(i > 0)
    def _():
        move(i - 1, 1 - slot, _wait)

    move(i, slot, _start)

    @pl.when(i == n_tiles - 1)
    def _():
        move(i, slot, _wait)
        zero_ref[...] = jnp.zeros_like(zero_ref)
        for act in (_start, _wait):
            for e in range(N_EXPERTS):
                _zero_fill(zero_ref, x_hbm, zs_ref[e], zl_ref[e], zsems, act)

            def tail(t, c):
                act(pltpu.make_async_copy(zero_ref, x_hbm.at[pl.ds(pl.multiple_of(t * MOE_TILE, MOE_TILE),
                                                                   MOE_TILE)], zsems.at[len(ZERO_SIZES)]))
                return c

            lax.fori_loop(nu_ref[0], n_pad // MOE_TILE, tail, 0)


def moe_dispatch(tab, h2, *, tm):
    n = h2.shape[0]
    n_tiles = n // tm
    n_pad = _moe_rows(n)
    return pl.pallas_call(
        functools.partial(_dispatch_kernel, tm=tm, n_tiles=n_tiles, n_pad=n_pad),
        grid_spec=pltpu.PrefetchScalarGridSpec(
            num_scalar_prefetch=6,
            grid=(n_tiles,),
            in_specs=[pl.BlockSpec((tm, D_MODEL), lambda i, *_: (i, 0)),
                      pl.BlockSpec((1, SUBLANES, tm), lambda i, *_: (i, 0, 0))],
            out_specs=pl.BlockSpec(memory_space=pl.ANY),
            scratch_shapes=[pltpu.VMEM((2, PERM_ROWS, D_MODEL), BF16), pltpu.VMEM((MOE_TILE, D_MODEL), BF16),
                            pltpu.SemaphoreType.DMA((2 * len(CHUNKS),)),
                            pltpu.SemaphoreType.DMA((len(ZERO_SIZES) + 1,))],
        ),
        out_shape=jax.ShapeDtypeStruct((n_pad, D_MODEL), BF16),
        compiler_params=_params("arbitrary"),
        name="moe_dispatch",
    )(tab["seg_off"], tab["c16"], tab["dst"], tab["zero_start"], tab["zero_len"], tab["n_used"],
      h2, tab["d_rows"])


def _grouped_ffn_kernel(te_ref, nt_ref, x_ref, wg_ref, wu_ref, wd_ref, y_ref):
    t = pl.program_id(0)

    @pl.when(t < nt_ref[0])
    def _():
        x = x_ref[...]
        a = jnp.dot(x, wg_ref[0].astype(BF16), preferred_element_type=F32)
        b = jnp.dot(x, wu_ref[0].astype(BF16), preferred_element_type=F32)
        act = (a * jax.nn.sigmoid(a)) * b
        y_ref[...] = jnp.dot(act.astype(BF16), wd_ref[0].astype(BF16),
                             preferred_element_type=F32).astype(y_ref.dtype)

    @pl.when(t >= nt_ref[0])
    def _():
        y_ref[...] = jnp.zeros_like(y_ref)


def grouped_ffn(tile_expert, n_used, xs, wg, wu, wd, *, first):
    n_pad = xs.shape[0]
    d_ff = wg.shape[2]
    rows = pl.BlockSpec((MOE_TILE, D_MODEL), lambda t, te, nt: (t, 0))
    return pl.pallas_call(
        _grouped_ffn_kernel,
        grid_spec=pltpu.PrefetchScalarGridSpec(
            num_scalar_prefetch=2,
            grid=(n_pad // MOE_TILE,),
            in_specs=[rows,
                      pl.BlockSpec((1, D_MODEL, d_ff), lambda t, te, nt: (first + te[t], 0, 0),
                                   pipeline_mode=pl.Buffered(1)),
                      pl.BlockSpec((1, D_MODEL, d_ff), lambda t, te, nt: (first + te[t], 0, 0),
                                   pipeline_mode=pl.Buffered(1)),
                      pl.BlockSpec((1, d_ff, D_MODEL), lambda t, te, nt: (first + te[t], 0, 0),
                                   pipeline_mode=pl.Buffered(1))],
            out_specs=rows,
        ),
        out_shape=jax.ShapeDtypeStruct((n_pad, D_MODEL), BF16),
        compiler_params=_params("arbitrary"),
        name="grouped_ffn",
    )(tile_expert, n_used, xs, wg, wu, wd)


def _combine_kernel(*refs, row, tm, final):
    if final:
        so_ref, c16_ref, dst_ref, x_ref, dc_ref, mf_ref, mod_ref, gfin_ref, y_hbm, o_ref, yt_ref, sems = refs
    else:
        so_ref, c16_ref, dst_ref, x_ref, dc_ref, mf_ref, mod_ref, y_hbm, o_ref, yt_ref, sems = refs
    i = pl.program_id(0)
    n_steps = pl.num_programs(0)
    slot = i % 2

    def fetch(tile, buf, act):
        for e in range(N_EXPERTS):
            k = tile * N_EXPERTS + e
            _segment_copies(y_hbm, yt_ref.at[buf], dst_ref[k], so_ref[k], c16_ref[k], sems, buf, act)

    @pl.when(i == 0)
    def _():
        yt_ref[...] = jnp.zeros_like(yt_ref)
        fetch(0, 0, _start)

    fetch(i, slot, _wait)

    @pl.when(i + 1 < n_steps)
    def _():
        fetch(i + 1, 1 - slot, _start)

    yt = yt_ref[slot]
    r = lax.broadcasted_iota(jnp.int32, (tm, PERM_ROWS), 1)
    pick = lambda col: jnp.dot(jnp.where(r == dc_ref[:, col:col + 1], 1.0, 0.0).astype(BF16), yt,
                               preferred_element_type=F32)
    tot = mf_ref[:, 0:1] * pick(0) + mf_ref[:, 1:2] * pick(1)
    x = x_ref[...] + mod_ref[0, row:row + 1, GT_F:GT_F + D_MODEL] * tot
    if final:
        x = _rms(x) * gfin_ref[...]
    o_ref[...] = x


def moe_combine(tab, x, meta_f, mod, g_final, ys, *, row, tm):
    n = x.shape[0]
    final = g_final is not None
    tile = lambda w: pl.BlockSpec((tm, w), lambda i, *_: (i, 0))
    in_specs = [tile(D_MODEL), tile(META_W), tile(META_W), pl.BlockSpec(mod.shape, lambda i, *_: (0, 0, 0))]
    args = [x, tab["d_cols"], meta_f, mod]
    if final:
        in_specs.append(pl.BlockSpec((1, D_MODEL), lambda i, *_: (0, 0)))
        args.append(g_final)
    in_specs.append(pl.BlockSpec(memory_space=pl.ANY))
    args.append(ys)
    return pl.pallas_call(
        functools.partial(_combine_kernel, row=row, tm=tm, final=final),
        grid_spec=pltpu.PrefetchScalarGridSpec(
            num_scalar_prefetch=3,
            grid=(n // tm,),
            in_specs=in_specs,
            out_specs=tile(D_MODEL),
            scratch_shapes=[pltpu.VMEM((2, PERM_ROWS, D_MODEL), BF16),
                            pltpu.SemaphoreType.DMA((2 * len(CHUNKS),))],
        ),
        out_shape=jax.ShapeDtypeStruct((n, D_MODEL), F32),
        compiler_params=_params("arbitrary"),
        name="moe_combine",
    )(tab["seg_off"], tab["c16"], tab["dst"], *args)


def _gate_weights(w_rg):
    depth = w_rg.shape[0]
    per = RNN_CHUNK // RNN_BLOCK
    nch = D_RNN // RNN_CHUNK
    wr = w_rg.astype(BF16).reshape(depth, 2, 2, nch, per, RNN_BLOCK, 1, RNN_BLOCK)
    on_diag = jnp.eye(per, dtype=bool).reshape(per, 1, per, 1)
    bd = jnp.where(on_diag, wr, jnp.zeros((), BF16))
    bd = bd.reshape(depth, 2, 2, nch, RNN_CHUNK, RNN_CHUNK)
    return jnp.concatenate([bd[:, :, 0], bd[:, :, 1]], axis=-1)


def _tile_rows(n, want):
    return want if n % want == 0 else n


def kernel(x, c, ctx, c_ctx, w_mod, b_mod, g_mix, g_ffn, w_in, conv_w, conv_b, w_rg, b_rg, lam,
           sink, g_grp, w_out, w_ffn_gate, w_ffn_up, w_ffn_down, w_router, w_exp_gate, w_exp_up,
           w_exp_down, g_final):
    assert x.shape[0] == 1 and ctx.shape[0] == 1
    depth = w_mod.shape[0]
    seq, n_ctx = x.shape[1], ctx.shape[1]
    xl, xc = x[0], ctx[0]

    cond8 = jnp.zeros((SUBLANES, D_MODEL), F32).at[0].set(c[0]).at[1].set(c_ctx)
    mods = adaln_all(cond8, w_mod, b_mod)
    cos, sin = rope_tables(seq)

    w_in_b = w_in.astype(BF16)
    w_out_b = w_out.astype(BF16)
    wg_gate = _gate_weights(w_rg)
    ffn_w = (w_ffn_gate.astype(BF16), w_ffn_up.astype(BF16), w_ffn_down.astype(BF16))
    exp_w = tuple(w.reshape((-1,) + w.shape[2:]) for w in (w_exp_gate, w_exp_up, w_exp_down))
    wr_pad = jnp.pad(w_router, ((0, 0), (0, 0), (0, LANES - N_EXPERTS)))
    wr_hi = wr_pad.astype(BF16)
    wr_lo = (wr_pad - wr_hi.astype(F32)).astype(BF16)
    zeros_h0 = jnp.zeros((SUBLANES, D_RNN), F32)

    tm = _tile_rows(seq, 512)
    tm_ffn = _tile_rows(seq, 1024)

    for l in range(depth):
        last = l == depth - 1
        moe = l % 2 == 1
        j = l // 2
        mod = mods[l:l + 1]
        gm, gf, gg = g_mix[l:l + 1], g_ffn[l:l + 1], g_grp[l:l + 1]

        xr_c, gr_c, q_c, kv_c = in_proj(xc, mod, gm, w_in_b, None, None, layer=l, row=1, tm=n_ctx)
        xr_l, gr_l, q_l, kv_l = in_proj(xl, mod, gm, w_in_b, cos, sin, layer=l, row=0, tm=tm_ffn)

        rnn_args = (conv_w[l], conv_b[l:l + 1], wg_gate, b_rg[l], lam[l])
        hf_c, hb_c, h_ctx_end = rnn_bidir(xr_c, *rnn_args, zeros_h0, layer=l, tm=n_ctx)
        hf_l, hb_l, _ = rnn_bidir(xr_l, *rnn_args, h_ctx_end, layer=l, tm=tm_ffn)

        o_l = attention(q_l, kv_l, kv_c, sink[l], band=True)

        router = (wr_hi, wr_lo) if moe else None
        weights = exp_w if moe else ffn_w
        first = j * N_EXPERTS if moe else j
        tf = 512 if moe else 1024
        merge_l = functools.partial(merge, layer=l, moe_layer=j)

        g_fin = g_final.reshape(1, D_MODEL) if last else None
        if moe:
            xl, h2, meta_i, meta_f, cnt = merge_l(xl, hf_l, hb_l, gr_l, o_l, mod, gg, w_out_b, gf, router,
                                                  row=0, tm=tm, moe="route")
            assert tm == MOE_TOKENS
            tab = _block_tables(cnt, meta_i, tm)
            xs = moe_dispatch(tab, h2, tm=tm)
            ys = grouped_ffn(tab["tile_expert"], tab["n_used"], xs, *weights, first=first)
            xl = moe_combine(tab, xl, meta_f, mod, g_fin, ys, row=0, tm=tm)
        else:
            xl, h2 = merge_l(xl, hf_l, hb_l, gr_l, o_l, mod, gg, w_out_b, gf, None, row=0, tm=tm, moe=None)
            xl = ffn(h2, xl, mod, None, g_fin, *weights, first=first, n_exp=1, row=0, tm=tm_ffn, tf=tf)

        if not last:
            o_c = attention(q_c, kv_c, kv_c, sink[l], band=False)
            res = merge_l(xc, hf_c, hb_c, gr_c, o_c, mod, gg, w_out_b, gf, router, row=1, tm=n_ctx,
                          moe="comb" if moe else None)
            xc, h2c = res[0], res[1]
            comb_c = res[2] if moe else None
            xc = ffn(h2c, xc, mod, comb_c, None, *weights, first=first, n_exp=N_EXPERTS if moe else 1,
                     row=1, tm=n_ctx, tf=tf)

    return xl[None]
```
